```python
import math
import jax, jax.numpy as jnp
from jax import lax
import numpy as np

D_MODEL = 1024
BATCH = 8
SEQ = 4096
DEPTH = 4
DEC_BATCH = 16
DEC_SEQ = 16
PAST_LEN = 4096

CHUNK = 64
N_EVEN = (DEPTH + 1) // 2
N_ODD = DEPTH // 2
D_A = D_MODEL // 2
CONV_A = 31
D_B = D_MODEL // 2
CONV_B = 3
POOL_WINDOWS = (2, 4, 8, 16)
N_POOL_GROUPS = 4
D_C = D_MODEL // 4
POOL_GROUP = D_C // N_POOL_GROUPS
POOL_HIST = 15
HEAD_DIM = 64
N_Q_HEADS = (D_MODEL - D_C) // HEAD_DIM
N_KV_HEADS = 4
GQA = N_Q_HEADS // N_KV_HEADS
D_ATT = N_Q_HEADS * HEAD_DIM
D_KV = N_KV_HEADS * HEAD_DIM
WINDOW = 128
WIN_CHUNKS = WINDOW // CHUNK
NUM_BUCKETS = 32
MAX_DISTANCE = 128
ATT_SCALE = HEAD_DIM ** -0.5
NEG_INF = -1e30
D_FF = 2816
CONV_F = 3
LN_EPS = 1e-5
ALPHA = (2 * DEPTH) ** 0.25
BETA = (8 * DEPTH) ** -0.25

kernel_name = 'hybrid_streaming_conv_pool_swa_encoder_step'


def layer_norm(x, g, b):
    xf = x.astype(jnp.float32)
    mu = jnp.mean(xf, axis=-1, keepdims=True)
    var = jnp.mean(jnp.square(xf - mu), axis=-1, keepdims=True)
    y = (xf - mu) * lax.rsqrt(var + LN_EPS) * g.astype(jnp.float32) + b.astype(jnp.float32)
    return y.astype(x.dtype)


def dwconv_valid(xp, w, bias=None):
    c = xp.shape[-1]
    y = lax.conv_general_dilated(xp, w[:, None, :].astype(xp.dtype), window_strides=(1,), padding='VALID',
                                 dimension_numbers=('NWC', 'WIO', 'NWC'), feature_group_count=c)
    if bias is not None:
        y = y + bias.astype(y.dtype)
    return y


def even_mixer(h, hist_a, hist_b, w_in, conv_a_w, conv_a_b, ln_a_g, ln_a_b, conv_b_w, w_out):
    z = h @ w_in
    a_val, a_gate, b_gate, c_gate, b_val = jnp.split(z, [D_A, 2 * D_A, 2 * D_A + D_B, 2 * D_A + 2 * D_B], axis=-1)
    u = a_val * jax.nn.sigmoid(a_gate)
    up = jnp.concatenate([hist_a, u], axis=1)
    a = jax.nn.silu(layer_norm(dwconv_valid(up, conv_a_w, conv_a_b), ln_a_g, ln_a_b))
    v = c_gate * b_val
    vp = jnp.concatenate([hist_b, v], axis=1)
    bo = b_gate * dwconv_valid(vp, conv_b_w)
    out = jnp.concatenate([a, bo], axis=-1) @ w_out
    return out, up[:, -(CONV_A - 1):], vp[:, -(CONV_B - 1):]


def multiscale_pool(cp, pos):
    t = cp.shape[1] - POOL_HIST
    cf = cp.astype(jnp.float32)
    cs = jnp.concatenate([jnp.zeros_like(cf[:, :1]), jnp.cumsum(cf, axis=1)], axis=1)
    end = cs[:, POOL_HIST + 1:]
    outs = []
    for g, win in enumerate(POOL_WINDOWS):
        sl = slice(g * POOL_GROUP, (g + 1) * POOL_GROUP)
        start = cs[:, POOL_HIST + 1 - win:POOL_HIST + 1 - win + t, sl]
        cnt = jnp.minimum(pos + 1, win).astype(jnp.float32)[None, :, None]
        outs.append((end[..., sl] - start) / cnt)
    pooled = jnp.concatenate(outs, axis=-1)
    return (pooled - cf[:, POOL_HIST:]).astype(cp.dtype)


def t5_bucket(rel):
    nb = NUM_BUCKETS // 2
    max_exact = nb // 2
    ret = jnp.where(rel > 0, nb, 0)
    n = jnp.abs(rel)
    nf = jnp.maximum(n, 1).astype(jnp.float32)
    large = max_exact + (jnp.log(nf / max_exact) / math.log(MAX_DISTANCE / max_exact) * (nb - max_exact)).astype(jnp.int32)
    large = jnp.minimum(large, nb - 1)
    return ret + jnp.where(n < max_exact, n, large)


def position_bias(rel, rel_bias):
    b = rel_bias[t5_bucket(rel)].astype(jnp.float32)
    return jnp.transpose(b, (2, 0, 1)).reshape(N_KV_HEADS, GQA, rel.shape[0], rel.shape[1])


def sink_attention(q, k, v, bias, sinks):
    s = jnp.einsum('...qhgd,...shd->...hgqs', q, k).astype(jnp.float32) * ATT_SCALE + bias
    sink = sinks.astype(jnp.float32).reshape(N_KV_HEADS, GQA, 1, 1)
    m = jnp.maximum(jnp.max(s, axis=-1, keepdims=True), sink)
    p = jnp.exp(s - m)
    p = p / (jnp.sum(p, axis=-1, keepdims=True) + jnp.exp(sink - m))
    return jnp.einsum('...hgqs,...shd->...qhgd', p.astype(v.dtype), v)


def banded_attention(q, k, v, sinks, rel_bias):
    b, t = q.shape[:2]
    nc = t // CHUNK
    qb = q.reshape(b, nc, CHUNK, N_KV_HEADS, GQA, HEAD_DIM)
    pad = jnp.zeros((b, WINDOW, N_KV_HEADS, HEAD_DIM), k.dtype)

    def band(a):
        ac = jnp.concatenate([pad, a], axis=1).reshape(b, nc + WIN_CHUNKS, CHUNK, N_KV_HEADS, HEAD_DIM)
        return jnp.concatenate([ac[:, j:j + nc] for j in range(WIN_CHUNKS + 1)], axis=2)

    kb, vb = band(k), band(v)
    qi = jnp.arange(CHUNK)
    kj = jnp.arange(WINDOW + CHUNK)
    bias = position_bias(kj[None, :] - WINDOW - qi[:, None], rel_bias)
    valid = (jnp.arange(nc)[:, None] + (kj // CHUNK - WIN_CHUNKS)[None, :]) >= 0
    bias = jnp.where(valid[:, None, None, None, :], bias[None], NEG_INF)
    o = sink_attention(qb, kb, vb, bias, sinks)
    return o.reshape(b, t, D_ATT), k[:, -WINDOW:], v[:, -WINDOW:]


def cached_attention(q, k, v, cache_k, cache_v, sinks, rel_bias):
    b, t = q.shape[:2]
    kall = jnp.concatenate([cache_k, k], axis=1)
    vall = jnp.concatenate([cache_v, v], axis=1)
    rel = jnp.arange(WINDOW + t)[None, :] - WINDOW - jnp.arange(t)[:, None]
    o = sink_attention(q, kall, vall, position_bias(rel, rel_bias), sinks)
    return o.reshape(b, t, D_ATT), kall[:, -WINDOW:], vall[:, -WINDOW:]


def odd_mixer(h, pos, hist_c, cache_k, cache_v, w_in, w_pool, pool_scale, sinks, rel_bias, w_out):
    b, t = h.shape[:2]
    z = h @ w_in
    c, q, k, v = jnp.split(z, [D_C, D_C + D_ATT, D_C + D_ATT + D_KV], axis=-1)
    cp = jnp.concatenate([hist_c, c], axis=1)
    pooled = multiscale_pool(cp, pos).reshape(b, t, N_POOL_GROUPS, POOL_GROUP)
    pooled = jnp.einsum('btgi,gio->btgo', pooled, w_pool).reshape(b, t, D_C) * pool_scale
    q = q.reshape(b, t, N_KV_HEADS, GQA, HEAD_DIM)
    k = k.reshape(b, t, N_KV_HEADS, HEAD_DIM)
    v = v.reshape(b, t, N_KV_HEADS, HEAD_DIM)
    if cache_k is None:
        att, nk, nv = banded_attention(q, k, v, sinks, rel_bias)
    else:
        att, nk, nv = cached_attention(q, k, v, cache_k, cache_v, sinks, rel_bias)
    out = jnp.concatenate([pooled, att], axis=-1) @ w_out
    return out, cp[:, -POOL_HIST:], nk, nv


def conv_ffn(h, hist_f, w_up, conv_w, conv_b, w_down):
    g, val = jnp.split(h @ w_up, [D_FF], axis=-1)
    gp = jnp.concatenate([hist_f, g], axis=1)
    g = dwconv_valid(gp, conv_w, conv_b)
    return (jax.nn.gelu(g) * val) @ w_down, gp[:, -(CONV_F - 1):]


def trunk(x, pos, hist_a, hist_b, hist_c, cache_k, cache_v, hist_f, w):
    st_a, st_b, st_c, st_k, st_v, st_f = [], [], [], [], [], []
    for layer in range(DEPTH):
        i = layer // 2
        if layer % 2 == 0:
            mix, sa, sb = even_mixer(x, hist_a[i], hist_b[i], w['w_in_ab'][i], w['conv_a_w'][i], w['conv_a_b'][i],
                                     w['ln_a_g'][i], w['ln_a_b'][i], w['conv_b_w'][i], w['w_out_ab'][i])
            st_a.append(sa)
            st_b.append(sb)
        else:
            ck = None if cache_k is None else cache_k[i]
            cv = None if cache_v is None else cache_v[i]
            mix, sc, sk, sv = odd_mixer(x, pos, hist_c[i], ck, cv, w['w_in_cd'][i], w['w_pool'][i], w['pool_scale'][i],
                                        w['attn_sinks'][i], w['rel_bias'], w['w_out_cd'][i])
            st_c.append(sc)
            st_k.append(sk)
            st_v.append(sv)
        x = layer_norm(ALPHA * x + mix, w['ln_mix_g'][layer], w['ln_mix_b'][layer])
        f, sf = conv_ffn(x, hist_f[layer], w['w_ffn_up'][layer], w['ffn_conv_w'][layer], w['ffn_conv_b'][layer],
                         w['w_ffn_down'][layer])
        st_f.append(sf)
        x = layer_norm(ALPHA * x + f, w['ln_ffn_g'][layer], w['ln_ffn_b'][layer])
    return x, jnp.stack(st_a), jnp.stack(st_b), jnp.stack(st_c), jnp.stack(st_k), jnp.stack(st_v), jnp.stack(st_f)


def setup_inputs(seed: int = 0) -> dict:
    key = jax.random.key(seed)
    ks = jax.random.split(key, 32)
    nrm = lambda k, shape, s: jax.random.normal(k, shape, jnp.float32) * s
    d_in_ab = 2 * D_A + 3 * D_B
    d_in_cd = D_C + D_ATT + 2 * D_KV
    return {
        'x_prompt': nrm(ks[0], (BATCH, SEQ, D_MODEL), 1.0),
        'x_sample': nrm(ks[1], (DEC_BATCH, DEC_SEQ, D_MODEL), 1.0),
        'state_conv_a': nrm(ks[2], (N_EVEN, DEC_BATCH, CONV_A - 1, D_A), 0.5),
        'state_conv_b': nrm(ks[3], (N_EVEN, DEC_BATCH, CONV_B - 1, D_B), 0.5),
        'state_pool_c': nrm(ks[4], (N_ODD, DEC_BATCH, POOL_HIST, D_C), 1.0),
        'cache_k_d': nrm(ks[5], (N_ODD, DEC_BATCH, WINDOW, N_KV_HEADS, HEAD_DIM), 1.0),
        'cache_v_d': nrm(ks[6], (N_ODD, DEC_BATCH, WINDOW, N_KV_HEADS, HEAD_DIM), 1.0),
        'state_ffn_conv': nrm(ks[7], (DEPTH, DEC_BATCH, CONV_F - 1, D_FF), 1.0),
        'w_in_ab': nrm(ks[8], (N_EVEN, D_MODEL, d_in_ab), D_MODEL ** -0.5),
        'conv_a_w': nrm(ks[9], (N_EVEN, CONV_A, D_A), CONV_A ** -0.5),
        'conv_a_b': nrm(ks[10], (N_EVEN, D_A), 0.02),
        'ln_a_g': 1.0 + nrm(ks[11], (N_EVEN, D_A), 0.02),
        'ln_a_b': nrm(ks[12], (N_EVEN, D_A), 0.02),
        'conv_b_w': nrm(ks[13], (N_EVEN, CONV_B, D_B), CONV_B ** -0.5),
        'w_out_ab': nrm(ks[14], (N_EVEN, D_A + D_B, D_MODEL), (D_A + D_B) ** -0.5 * BETA),
        'w_in_cd': nrm(ks[15], (N_ODD, D_MODEL, d_in_cd), D_MODEL ** -0.5),
        'w_pool': nrm(ks[16], (N_ODD, N_POOL_GROUPS, POOL_GROUP, POOL_GROUP), POOL_GROUP ** -0.5),
        'pool_scale': 1.0 + nrm(ks[17], (N_ODD, D_C), 0.02),
        'attn_sinks': nrm(ks[18], (N_ODD, N_Q_HEADS), 0.5),
        'rel_bias': nrm(ks[19], (NUM_BUCKETS, N_Q_HEADS), 0.5),
        'w_out_cd': nrm(ks[20], (N_ODD, D_C + D_ATT, D_MODEL), (D_C + D_ATT) ** -0.5 * BETA),
        'w_ffn_up': nrm(ks[21], (DEPTH, D_MODEL, 2 * D_FF), D_MODEL ** -0.5),
        'ffn_conv_w': nrm(ks[22], (DEPTH, CONV_F, D_FF), CONV_F ** -0.5),
        'ffn_conv_b': nrm(ks[23], (DEPTH, D_FF), 0.02),
        'w_ffn_down': nrm(ks[24], (DEPTH, D_FF, D_MODEL), D_FF ** -0.5 * BETA),
        'ln_mix_g': 1.0 + nrm(ks[25], (DEPTH, D_MODEL), 0.02),
        'ln_mix_b': nrm(ks[26], (DEPTH, D_MODEL), 0.02),
        'ln_ffn_g': 1.0 + nrm(ks[27], (DEPTH, D_MODEL), 0.02),
        'ln_ffn_b': nrm(ks[28], (DEPTH, D_MODEL), 0.02),
    }


def reference(x_prompt, x_sample, state_conv_a, state_conv_b, state_pool_c, cache_k_d, cache_v_d, state_ffn_conv,
              w_in_ab, conv_a_w, conv_a_b, ln_a_g, ln_a_b, conv_b_w, w_out_ab,
              w_in_cd, w_pool, pool_scale, attn_sinks, rel_bias, w_out_cd,
              w_ffn_up, ffn_conv_w, ffn_conv_b, w_ffn_down,
              ln_mix_g, ln_mix_b, ln_ffn_g, ln_ffn_b):
    w = dict(w_in_ab=w_in_ab, conv_a_w=conv_a_w, conv_a_b=conv_a_b, ln_a_g=ln_a_g, ln_a_b=ln_a_b,
             conv_b_w=conv_b_w, w_out_ab=w_out_ab, w_in_cd=w_in_cd, w_pool=w_pool, pool_scale=pool_scale,
             attn_sinks=attn_sinks, rel_bias=rel_bias, w_out_cd=w_out_cd, w_ffn_up=w_ffn_up,
             ffn_conv_w=ffn_conv_w, ffn_conv_b=ffn_conv_b, w_ffn_down=w_ffn_down,
             ln_mix_g=ln_mix_g, ln_mix_b=ln_mix_b, ln_ffn_g=ln_ffn_g, ln_ffn_b=ln_ffn_b)
    b, t = x_prompt.shape[:2]
    dt = x_prompt.dtype
    y_prompt, p_a, p_b, p_c, p_k, p_v, p_f = trunk(
        x_prompt, jnp.arange(t, dtype=jnp.int32),
        jnp.zeros((N_EVEN, b, CONV_A - 1, D_A), dt), jnp.zeros((N_EVEN, b, CONV_B - 1, D_B), dt),
        jnp.zeros((N_ODD, b, POOL_HIST, D_C), dt), None, None,
        jnp.zeros((DEPTH, b, CONV_F - 1, D_FF), dt), w)
    y_sample, s_a, s_b, s_c, s_k, s_v, s_f = trunk(
        x_sample, PAST_LEN + jnp.arange(x_sample.shape[1], dtype=jnp.int32),
        state_conv_a, state_conv_b, state_pool_c, cache_k_d, cache_v_d, state_ffn_conv, w)
    return (y_prompt, y_sample, p_a, p_b, p_c, p_k, p_v, p_f, s_a, s_b, s_c, s_k, s_v, s_f)
```

```python
import functools
import math

import jax
import jax.numpy as jnp
from jax import lax
from jax.experimental import pallas as pl
from jax.experimental.pallas import tpu as pltpu

F32 = jnp.float32
BF16 = jnp.bfloat16

D_MODEL = 1024
DEPTH = 4
PAST_LEN = 4096
CHUNK = 64
D_A = D_MODEL // 2
CONV_A = 31
D_B = D_MODEL // 2
CONV_B = 3
POOL_WINDOWS = (2, 4, 8, 16)
D_C = D_MODEL // 4
POOL_GROUP = D_C // 4
POOL_HIST = 15
HEAD_DIM = 64
N_Q_HEADS = (D_MODEL - D_C) // HEAD_DIM
N_KV_HEADS = 4
GQA = N_Q_HEADS // N_KV_HEADS
D_ATT = N_Q_HEADS * HEAD_DIM
D_KV = N_KV_HEADS * HEAD_DIM
WINDOW = 128
NUM_BUCKETS = 32
MAX_DISTANCE = 128
ATT_SCALE = HEAD_DIM ** -0.5
NEG_INF = -1e30
D_FF = 2816
CONV_F = 3
LN_EPS = 1e-5
ALPHA = (2 * DEPTH) ** 0.25

LANES = 128
SUBLANES = 8
HIST_A_PAD = 32
HIST_B_PAD = SUBLANES
HIST_C_PAD = 16
HIST_F_PAD = SUBLANES
FF_CHUNK = 256
N_FF_CHUNKS = D_FF // FF_CHUNK
N_PAIRS = N_KV_HEADS // 2
PAIR_ROWS = 2 * GQA
CONV_ROW_BLOCK = 32
VMEM_LIMIT = 56 * 1024 * 1024


def _layer_norm(v, g, b):
    mu = jnp.mean(v, axis=-1, keepdims=True)
    d = v - mu
    var = jnp.mean(d * d, axis=-1, keepdims=True)
    return d * lax.rsqrt(var + LN_EPS) * g + b


def _dot(a, b):
    return jnp.dot(a, b, preferred_element_type=F32)


def _ffn_kernel(x_ref, hist_ref, wg_ref, wv_ref, cw_ref, cb_ref, wd_ref, lg_ref, lb_ref,
                y_ref, st_ref, xb_scr, g_scr, acc_scr, *, nb, tt):
    m = nb * tt

    @pl.when(pl.program_id(1) == 0)
    def _():
        st_ref[...] = hist_ref[...]

    x = x_ref[...].reshape(m, D_MODEL)
    xb_scr[...] = x.astype(BF16)
    acc_scr[...] = jnp.zeros_like(acc_scr)

    def body(j, carry):
        xb = xb_scr[...]
        g = _dot(xb, wg_ref[j]).reshape(nb, tt, FF_CHUNK)
        v = _dot(xb, wv_ref[j]).reshape(nb, tt, FF_CHUNK)
        g_scr[:, 0:HIST_F_PAD, :] = st_ref[j]
        g_scr[:, HIST_F_PAD:HIST_F_PAD + tt, :] = g
        st_ref[j] = g_scr[:, tt:tt + HIST_F_PAD, :]
        cw = cw_ref[j]
        conv = (cw[0:1, :][None] * g_scr[:, HIST_F_PAD - 2:HIST_F_PAD - 2 + tt, :]
                + cw[1:2, :][None] * g_scr[:, HIST_F_PAD - 1:HIST_F_PAD - 1 + tt, :]
                + cw[2:3, :][None] * g + cb_ref[j][None])
        h = jax.nn.gelu(conv) * v
        hb = h.reshape(m, FF_CHUNK).astype(BF16)
        acc_scr[...] += _dot(hb, wd_ref[j])
        return carry

    lax.fori_loop(0, N_FF_CHUNKS, body, 0)
    y = ALPHA * x + acc_scr[...]
    y_ref[...] = _layer_norm(y, lg_ref[...], lb_ref[...]).reshape(nb, tt, D_MODEL)


def _ffn_call(x, hist, wg, wv, cw, cb, wd, lg, lb, *, nb, tt):
    b, s, _ = x.shape
    grid = (b // nb, s // tt)
    const = lambda *shape: pl.BlockSpec(shape, lambda i, t: (0,) * len(shape))
    return pl.pallas_call(
        functools.partial(_ffn_kernel, nb=nb, tt=tt),
        grid=grid,
        in_specs=[
            pl.BlockSpec((nb, tt, D_MODEL), lambda i, t: (i, t, 0)),
            pl.BlockSpec((N_FF_CHUNKS, nb, HIST_F_PAD, FF_CHUNK), lambda i, t: (0, i, 0, 0)),
            const(N_FF_CHUNKS, D_MODEL, FF_CHUNK),
            const(N_FF_CHUNKS, D_MODEL, FF_CHUNK),
            const(N_FF_CHUNKS, CONV_F, FF_CHUNK),
            const(N_FF_CHUNKS, 1, FF_CHUNK),
            const(N_FF_CHUNKS, FF_CHUNK, D_MODEL),
            const(1, D_MODEL),
            const(1, D_MODEL),
        ],
        out_specs=[
            pl.BlockSpec((nb, tt, D_MODEL), lambda i, t: (i, t, 0)),
            pl.BlockSpec((N_FF_CHUNKS, nb, HIST_F_PAD, FF_CHUNK), lambda i, t: (0, i, 0, 0)),
        ],
        out_shape=[
            jax.ShapeDtypeStruct((b, s, D_MODEL), F32),
            jax.ShapeDtypeStruct((N_FF_CHUNKS, b, HIST_F_PAD, FF_CHUNK), F32),
        ],
        scratch_shapes=[
            pltpu.VMEM((nb * tt, D_MODEL), BF16),
            pltpu.VMEM((nb, HIST_F_PAD + tt, FF_CHUNK), F32),
            pltpu.VMEM((nb * tt, D_MODEL), F32),
        ],
        compiler_params=pltpu.CompilerParams(
            dimension_semantics=("parallel", "arbitrary"), vmem_limit_bytes=VMEM_LIMIT),
    )(x, hist, wg, wv, cw, cb, wd, lg, lb)


def _even_kernel(x_ref, ha_ref, hb_ref, win_ref, caw_ref, cab_ref, lag_ref, lab_ref, cbw_ref,
                 wout_ref, lg_ref, lb_ref, y_ref, sta_ref, stb_ref, u_scr, a_scr, *, nb, tt):
    m = nb * tt

    @pl.when(pl.program_id(1) == 0)
    def _():
        sta_ref[...] = ha_ref[...]
        stb_ref[...] = hb_ref[...]

    x = x_ref[...].reshape(m, D_MODEL)
    xb = x.astype(BF16)

    def proj(i):
        return _dot(xb, win_ref[:, i * D_A:(i + 1) * D_A])

    u = proj(0) * jax.nn.sigmoid(proj(1))
    u_scr[:, 0:HIST_A_PAD, :] = sta_ref[...]
    u_scr[:, HIST_A_PAD:HIST_A_PAD + tt, :] = u.reshape(nb, tt, D_A)
    sta_ref[...] = u_scr[:, tt:tt + HIST_A_PAD, :]
    rb = min(CONV_ROW_BLOCK, tt)
    first = HIST_A_PAD - (CONV_A - 1)
    cab = cab_ref[...]
    lag = lag_ref[...]
    lab = lab_ref[...]
    for n in range(nb):
        for r0 in range(0, tt, rb):
            acc = jnp.broadcast_to(cab, (rb, D_A))
            for k in range(CONV_A):
                acc = acc + caw_ref[k:k + 1, :] * u_scr[n, first + r0 + k:first + r0 + k + rb, :]
            a = jax.nn.silu(_layer_norm(acc, lag, lab))
            a_scr[n * tt + r0:n * tt + r0 + rb, :] = a.astype(BF16)

    v = (proj(3) * proj(4)).reshape(nb, tt, D_B)
    u_scr[:, 0:HIST_B_PAD, :] = stb_ref[...]
    u_scr[:, HIST_B_PAD:HIST_B_PAD + tt, :] = v
    stb_ref[...] = u_scr[:, tt:tt + HIST_B_PAD, :]
    cbw = cbw_ref[...]
    conv_b = (cbw[0:1, :][None] * u_scr[:, HIST_B_PAD - 2:HIST_B_PAD - 2 + tt, :]
              + cbw[1:2, :][None] * u_scr[:, HIST_B_PAD - 1:HIST_B_PAD - 1 + tt, :]
              + cbw[2:3, :][None] * v)
    bo = proj(2) * conv_b.reshape(m, D_B)

    mix = _dot(a_scr[...], wout_ref[0:D_A, :]) + _dot(bo.astype(BF16), wout_ref[D_A:D_A + D_B, :])
    y = ALPHA * x + mix
    y_ref[...] = _layer_norm(y, lg_ref[...], lb_ref[...]).reshape(nb, tt, D_MODEL)


def _even_call(x, ha, hb, win, caw, cab, lag, lab, cbw, wout, lg, lb, *, nb, tt):
    b, s, _ = x.shape
    grid = (b // nb, s // tt)
    const = lambda *shape: pl.BlockSpec(shape, lambda i, t: (0,) * len(shape))
    return pl.pallas_call(
        functools.partial(_even_kernel, nb=nb, tt=tt),
        grid=grid,
        in_specs=[
            pl.BlockSpec((nb, tt, D_MODEL), lambda i, t: (i, t, 0)),
            pl.BlockSpec((nb, HIST_A_PAD, D_A), lambda i, t: (i, 0, 0)),
            pl.BlockSpec((nb, HIST_B_PAD, D_B), lambda i, t: (i, 0, 0)),
            const(D_MODEL, 2 * D_A + 3 * D_B),
            const(CONV_A, D_A),
            const(1, D_A),
            const(1, D_A),
            const(1, D_A),
            const(CONV_B, D_B),
            const(D_A + D_B, D_MODEL),
            const(1, D_MODEL),
            const(1, D_MODEL),
        ],
        out_specs=[
            pl.BlockSpec((nb, tt, D_MODEL), lambda i, t: (i, t, 0)),
            pl.BlockSpec((nb, HIST_A_PAD, D_A), lambda i, t: (i, 0, 0)),
            pl.BlockSpec((nb, HIST_B_PAD, D_B), lambda i, t: (i, 0, 0)),
        ],
        out_shape=[
            jax.ShapeDtypeStruct((b, s, D_MODEL), F32),
            jax.ShapeDtypeStruct((b, HIST_A_PAD, D_A), F32),
            jax.ShapeDtypeStruct((b, HIST_B_PAD, D_B), F32),
        ],
        scratch_shapes=[
            pltpu.VMEM((nb, HIST_A_PAD + tt, D_A), F32),
            pltpu.VMEM((nb * tt, D_A), BF16),
        ],
        compiler_params=pltpu.CompilerParams(
            dimension_semantics=("parallel", "arbitrary"), vmem_limit_bytes=VMEM_LIMIT),
    )(x, ha, hb, win, caw, cab, lag, lab, cbw, wout, lg, lb)


def _odd_kernel(x_ref, hc_ref, ck_ref, cv_ref, win_ref, wpool_ref, pscale_ref, bias_ref, sink_ref,
                wout_ref, lg_ref, lb_ref, y_ref, stc_ref, nk_ref, nv_ref,
                c_scr, k_scr, v_scr, att_scr, *, nb, tt, cq, pos0, masked_history):
    m = nb * tt
    kw = WINDOW + cq
    t_idx = pl.program_id(1)

    @pl.when(t_idx == 0)
    def _():
        stc_ref[...] = hc_ref[...]
        nk_ref[...] = ck_ref[...]
        nv_ref[...] = cv_ref[...]

    x = x_ref[...].reshape(m, D_MODEL)
    xb = x.astype(BF16)
    q_off = D_C
    k_off = D_C + D_ATT
    v_off = k_off + D_KV

    c = _dot(xb, win_ref[:, 0:D_C]).reshape(nb, tt, D_C)
    c_scr[:, 0:HIST_C_PAD, :] = stc_ref[...]
    c_scr[:, HIST_C_PAD:HIST_C_PAD + tt, :] = c
    stc_ref[...] = c_scr[:, tt:tt + HIST_C_PAD, :]

    def window_sum(lane0, lo, hi):
        acc = None
        for j in range(lo, hi):
            r = c_scr[:, HIST_C_PAD - j:HIST_C_PAD - j + tt, lane0:lane0 + LANES]
            acc = r if acc is None else acc + r
        return acc

    row = lax.broadcasted_iota(jnp.int32, (nb, tt, LANES), 1)
    lane = lax.broadcasted_iota(jnp.int32, (nb, tt, LANES), 2)
    pos1 = row + (pos0 + 1) + t_idx * tt
    low = lane < POOL_GROUP
    pooled = []
    for tile, (w_small, w_big) in enumerate(((2, 4), (8, 16))):
        s_small = window_sum(tile * LANES, 0, w_small)
        s_big = s_small + window_sum(tile * LANES, w_small, w_big)
        total = jnp.where(low, s_small, s_big)
        cnt = jnp.minimum(pos1, jnp.where(low, w_small, w_big)).astype(F32)
        pooled.append(total / cnt - c[:, :, tile * LANES:(tile + 1) * LANES])
    pooled = jnp.concatenate(pooled, axis=-1).reshape(m, D_C)
    pooled = _dot(pooled.astype(BF16), wpool_ref[...]) * pscale_ref[...]

    k_scr[:, 0:WINDOW, :] = nk_ref[...]
    v_scr[:, 0:WINDOW, :] = nv_ref[...]
    k_scr[:, WINDOW:WINDOW + tt, :] = _dot(xb, win_ref[:, k_off:k_off + D_KV]).reshape(nb, tt, D_KV)
    v_scr[:, WINDOW:WINDOW + tt, :] = _dot(xb, win_ref[:, v_off:v_off + D_KV]).reshape(nb, tt, D_KV)
    nk_ref[...] = k_scr[:, tt:tt + WINDOW, :]
    nv_ref[...] = v_scr[:, tt:tt + WINDOW, :]
    q = _dot(xb, win_ref[:, q_off:q_off + D_ATT]).reshape(nb, tt, D_ATT)

    rows = PAIR_ROWS * cq
    q_lane = lax.broadcasted_iota(jnp.int32, (cq, LANES), 1)
    key_idx = lax.broadcasted_iota(jnp.int32, (rows, kw), 1)
    out_lane = lax.broadcasted_iota(jnp.int32, (cq, LANES), 1)
    for n in range(nb):
        for ci in range(tt // cq):
            r0 = ci * cq
            for p in range(N_PAIRS):
                tiles = [q[n, r0:r0 + cq, (p * GQA + g) * LANES:(p * GQA + g + 1) * LANES]
                         for g in range(GQA)]
                lhs = ([jnp.where(q_lane < HEAD_DIM, tl, 0.0) for tl in tiles]
                       + [jnp.where(q_lane >= HEAD_DIM, tl, 0.0) for tl in tiles])
                lhs = jnp.concatenate(lhs, axis=0).astype(BF16)
                kt = k_scr[n, r0:r0 + kw, p * LANES:(p + 1) * LANES].astype(BF16)
                vt = v_scr[n, r0:r0 + kw, p * LANES:(p + 1) * LANES].astype(BF16)
                s = lax.dot_general(lhs, kt, (((1,), (1,)), ((), ())),
                                    preferred_element_type=F32)
                s = s * ATT_SCALE + bias_ref[p]
                if masked_history:
                    first_valid = WINDOW - (t_idx * tt + r0)
                    s = jnp.where(key_idx >= first_valid, s, NEG_INF)
                sink = sink_ref[p]
                mx = jnp.maximum(jnp.max(s, axis=-1, keepdims=True), sink)
                e = jnp.exp(s - mx)
                den = jnp.sum(e, axis=-1, keepdims=True) + jnp.exp(sink - mx)
                pv = _dot(e.astype(BF16), vt) * (1.0 / den)
                for g in range(GQA):
                    o = jnp.where(out_lane < HEAD_DIM, pv[g * cq:(g + 1) * cq, :],
                                  pv[(GQA + g) * cq:(GQA + g + 1) * cq, :])
                    col = (p * GQA + g) * LANES
                    att_scr[n * tt + r0:n * tt + r0 + cq, col:col + LANES] = o.astype(BF16)

    mix = (_dot(pooled.astype(BF16), wout_ref[0:D_C, :])
           + _dot(att_scr[...], wout_ref[D_C:D_C + D_ATT, :]))
    y = ALPHA * x + mix
    y_ref[...] = _layer_norm(y, lg_ref[...], lb_ref[...]).reshape(nb, tt, D_MODEL)


def _odd_call(x, hc, ck, cv, win, wpool, pscale, bias, sink, wout, lg, lb, *,
              nb, tt, cq, pos0, masked_history):
    b, s, _ = x.shape
    grid = (b // nb, s // tt)
    kw = WINDOW + cq
    rows = PAIR_ROWS * cq
    const = lambda *shape: pl.BlockSpec(shape, lambda i, t: (0,) * len(shape))
    return pl.pallas_call(
        functools.partial(_odd_kernel, nb=nb, tt=tt, cq=cq, pos0=pos0,
                          masked_history=masked_history),
        grid=grid,
        in_specs=[
            pl.BlockSpec((nb, tt, D_MODEL), lambda i, t: (i, t, 0)),
            pl.BlockSpec((nb, HIST_C_PAD, D_C), lambda i, t: (i, 0, 0)),
            pl.BlockSpec((nb, WINDOW, D_KV), lambda i, t: (i, 0, 0)),
            pl.BlockSpec((nb, WINDOW, D_KV), lambda i, t: (i, 0, 0)),
            const(D_MODEL, D_C + D_ATT + 2 * D_KV),
            const(D_C, D_C),
            const(1, D_C),
            const(N_PAIRS, rows, kw),
            const(N_PAIRS, rows, 1),
            const(D_C + D_ATT, D_MODEL),
            const(1, D_MODEL),
            const(1, D_MODEL),
        ],
        out_specs=[
            pl.BlockSpec((nb, tt, D_MODEL), lambda i, t: (i, t, 0)),
            pl.BlockSpec((nb, HIST_C_PAD, D_C), lambda i, t: (i, 0, 0)),
            pl.BlockSpec((nb, WINDOW, D_KV), lambda i, t: (i, 0, 0)),
            pl.BlockSpec((nb, WINDOW, D_KV), lambda i, t: (i, 0, 0)),
        ],
        out_shape=[
            jax.ShapeDtypeStruct((b, s, D_MODEL), F32),
            jax.ShapeDtypeStruct((b, HIST_C_PAD, D_C), F32),
            jax.ShapeDtypeStruct((b, WINDOW, D_KV), F32),
            jax.ShapeDtypeStruct((b, WINDOW, D_KV), F32),
        ],
        scratch_shapes=[
            pltpu.VMEM((nb, HIST_C_PAD + tt, D_C), F32),
            pltpu.VMEM((nb, WINDOW + tt, D_KV), F32),
            pltpu.VMEM((nb, WINDOW + tt, D_KV), F32),
            pltpu.VMEM((nb * tt, D_ATT), BF16),
        ],
        compiler_params=pltpu.CompilerParams(
            dimension_semantics=("parallel", "arbitrary"), vmem_limit_bytes=VMEM_LIMIT),
    )(x, hc, ck, cv, win, wpool, pscale, bias, sink, wout, lg, lb)


def _bias_kernel(bucket_ref, rb_ref, out_ref):
    bucket = bucket_ref[...]
    for p in range(N_PAIRS):
        for r in range(PAIR_ROWS):
            head = (2 * p + r // GQA) * GQA + r % GQA
            acc = jnp.zeros((CHUNK, WINDOW + CHUNK), F32)
            for b in range(NUM_BUCKETS):
                acc = jnp.where(bucket == b, rb_ref[b, head], acc)
            out_ref[p, r * CHUNK:(r + 1) * CHUNK, :] = acc


def _bias_call(bucket, rel_bias):
    return pl.pallas_call(
        _bias_kernel,
        in_specs=[
            pl.BlockSpec(memory_space=pltpu.VMEM),
            pl.BlockSpec(memory_space=pltpu.SMEM),
        ],
        out_specs=pl.BlockSpec(memory_space=pltpu.VMEM),
        out_shape=jax.ShapeDtypeStruct((N_PAIRS, PAIR_ROWS * CHUNK, WINDOW + CHUNK), F32),
    )(bucket, rel_bias)


def _t5_bucket(rel):
    nb = NUM_BUCKETS // 2
    max_exact = nb // 2
    ret = jnp.where(rel > 0, nb, 0)
    n = jnp.abs(rel)
    nf = jnp.maximum(n, 1).astype(jnp.float32)
    large = max_exact + (jnp.log(nf / max_exact) / math.log(MAX_DISTANCE / max_exact)
                         * (nb - max_exact)).astype(jnp.int32)
    large = jnp.minimum(large, nb - 1)
    return ret + jnp.where(n < max_exact, n, large)


def _head_order():
    order = []
    for p in range(N_PAIRS):
        for g in range(GQA):
            order += [(2 * p) * GQA + g, (2 * p + 1) * GQA + g]
    return order


def _pad_rows(h, rows):
    pad = rows - h.shape[-2]
    cfg = [(0, 0)] * (h.ndim - 2) + [(pad, 0), (0, 0)]
    return jnp.pad(h, cfg)


def _prepare(w):
    order = _head_order()
    cols = jnp.asarray([h * HEAD_DIM + d for h in order for d in range(HEAD_DIM)], jnp.int32)
    p = {}
    p['w_in_ab'] = w['w_in_ab'].astype(BF16)
    p['w_out_ab'] = w['w_out_ab'].astype(BF16)
    win = w['w_in_cd']
    q_cols = jnp.take(win[:, :, D_C:D_C + D_ATT], cols, axis=2)
    p['w_in_cd'] = jnp.concatenate([win[:, :, :D_C], q_cols, win[:, :, D_C + D_ATT:]], axis=2).astype(BF16)
    wout = w['w_out_cd']
    att_rows = jnp.take(wout[:, D_C:, :], cols, axis=1)
    p['w_out_cd'] = jnp.concatenate([wout[:, :D_C, :], att_rows], axis=1).astype(BF16)
    n_odd = win.shape[0]
    wp = jnp.zeros((n_odd, D_C, D_C), F32)
    for g in range(4):
        sl = slice(g * POOL_GROUP, (g + 1) * POOL_GROUP)
        wp = wp.at[:, sl, sl].set(w['w_pool'][:, g])
    p['w_pool'] = wp.astype(BF16)
    up = w['w_ffn_up'].astype(BF16)
    depth = up.shape[0]
    split = lambda a: a.reshape(depth, D_MODEL, N_FF_CHUNKS, FF_CHUNK).transpose(0, 2, 1, 3)
    p['w_up_g'] = split(up[:, :, :D_FF])
    p['w_up_v'] = split(up[:, :, D_FF:])
    p['ffn_conv_w'] = w['ffn_conv_w'].reshape(depth, CONV_F, N_FF_CHUNKS, FF_CHUNK).transpose(0, 2, 1, 3)
    p['ffn_conv_b'] = w['ffn_conv_b'].reshape(depth, N_FF_CHUNKS, 1, FF_CHUNK)
    p['w_down'] = w['w_ffn_down'].astype(BF16).reshape(depth, N_FF_CHUNKS, FF_CHUNK, D_MODEL)
    heads = jnp.asarray([[(2 * pp + r // GQA) * GQA + r % GQA for r in range(PAIR_ROWS)]
                         for pp in range(N_PAIRS)], jnp.int32)
    p['sinks'] = jnp.take(w['attn_sinks'], heads, axis=1)
    for name in ('conv_a_w', 'conv_a_b', 'ln_a_g', 'ln_a_b', 'conv_b_w', 'pool_scale',
                 'ln_mix_g', 'ln_mix_b', 'ln_ffn_g', 'ln_ffn_b'):
        p[name] = w[name]
    return p


def _trunk(x, pos0, hist_a, hist_b, hist_c, cache_k, cache_v, hist_f, p, bias, *,
           nb, tt, cq, masked_history):
    b = x.shape[0]
    row = lambda v: v.reshape(1, -1)
    bias_t = bias.reshape(N_PAIRS, PAIR_ROWS, CHUNK, WINDOW + CHUNK)[:, :, :cq, :WINDOW + cq]
    bias_t = bias_t.reshape(N_PAIRS, PAIR_ROWS * cq, WINDOW + cq)
    st_a, st_b, st_c, st_k, st_v, st_f = [], [], [], [], [], []
    for layer in range(DEPTH):
        i = layer // 2
        if layer % 2 == 0:
            x, sa, sb = _even_call(
                x, _pad_rows(hist_a[i], HIST_A_PAD), _pad_rows(hist_b[i], HIST_B_PAD),
                p['w_in_ab'][i], p['conv_a_w'][i], row(p['conv_a_b'][i]), row(p['ln_a_g'][i]),
                row(p['ln_a_b'][i]), p['conv_b_w'][i], p['w_out_ab'][i],
                row(p['ln_mix_g'][layer]), row(p['ln_mix_b'][layer]), nb=nb, tt=tt)
            st_a.append(sa[:, HIST_A_PAD - (CONV_A - 1):])
            st_b.append(sb[:, HIST_B_PAD - (CONV_B - 1):])
        else:
            sink = jnp.repeat(p['sinks'][i], cq, axis=1)[..., None]
            x, sc, sk, sv = _odd_call(
                x, _pad_rows(hist_c[i], HIST_C_PAD),
                cache_k[i].reshape(b, WINDOW, D_KV), cache_v[i].reshape(b, WINDOW, D_KV),
                p['w_in_cd'][i], p['w_pool'][i], row(p['pool_scale'][i]), bias_t, sink,
                p['w_out_cd'][i], row(p['ln_mix_g'][layer]), row(p['ln_mix_b'][layer]),
                nb=nb, tt=tt, cq=cq, pos0=pos0, masked_history=masked_history)
            st_c.append(sc[:, HIST_C_PAD - POOL_HIST:])
            st_k.append(sk.reshape(b, WINDOW, N_KV_HEADS, HEAD_DIM))
            st_v.append(sv.reshape(b, WINDOW, N_KV_HEADS, HEAD_DIM))
        hf = _pad_rows(hist_f[layer], HIST_F_PAD)
        hf = hf.reshape(b, HIST_F_PAD, N_FF_CHUNKS, FF_CHUNK).transpose(2, 0, 1, 3)
        x, sf = _ffn_call(
            x, hf, p['w_up_g'][layer], p['w_up_v'][layer], p['ffn_conv_w'][layer],
            p['ffn_conv_b'][layer], p['w_down'][layer],
            row(p['ln_ffn_g'][layer]), row(p['ln_ffn_b'][layer]), nb=nb, tt=tt)
        sf = sf[:, :, HIST_F_PAD - (CONV_F - 1):, :].transpose(1, 2, 0, 3)
        st_f.append(sf.reshape(b, CONV_F - 1, D_FF))
    return (x, jnp.stack(st_a), jnp.stack(st_b), jnp.stack(st_c), jnp.stack(st_k),
            jnp.stack(st_v), jnp.stack(st_f))


def _tile_rows(s):
    for cand in (256, 128, 64):
        if s % cand == 0:
            return cand
    return s


def kernel(x_prompt, x_sample, state_conv_a, state_conv_b, state_pool_c, cache_k_d, cache_v_d, state_ffn_conv, w_in_ab, conv_a_w, conv_a_b, ln_a_g, ln_a_b, conv_b_w, w_out_ab, w_in_cd, w_pool, pool_scale, attn_sinks, rel_bias, w_out_cd, w_ffn_up, ffn_conv_w, ffn_conv_b, w_ffn_down, ln_mix_g, ln_mix_b, ln_ffn_g, ln_ffn_b):
    w = dict(w_in_ab=w_in_ab, conv_a_w=conv_a_w, conv_a_b=conv_a_b, ln_a_g=ln_a_g, ln_a_b=ln_a_b,
             conv_b_w=conv_b_w, w_out_ab=w_out_ab, w_in_cd=w_in_cd, w_pool=w_pool, pool_scale=pool_scale,
             attn_sinks=attn_sinks, w_out_cd=w_out_cd, w_ffn_up=w_ffn_up,
             ffn_conv_w=ffn_conv_w, ffn_conv_b=ffn_conv_b, w_ffn_down=w_ffn_down,
             ln_mix_g=ln_mix_g, ln_mix_b=ln_mix_b, ln_ffn_g=ln_ffn_g, ln_ffn_b=ln_ffn_b)
    p = _prepare(w)
    n_even, n_odd, depth = w_in_ab.shape[0], w_in_cd.shape[0], w_ffn_up.shape[0]
    rel = (jnp.arange(WINDOW + CHUNK)[None, :] - WINDOW - jnp.arange(CHUNK)[:, None]).astype(jnp.int32)
    bias = _bias_call(_t5_bucket(rel).astype(jnp.int32), rel_bias)

    b, s = x_prompt.shape[:2]
    zeros = lambda *shape: jnp.zeros(shape, F32)
    prompt = _trunk(
        x_prompt, 0, zeros(n_even, b, CONV_A - 1, D_A), zeros(n_even, b, CONV_B - 1, D_B),
        zeros(n_odd, b, POOL_HIST, D_C), zeros(n_odd, b, WINDOW, N_KV_HEADS, HEAD_DIM),
        zeros(n_odd, b, WINDOW, N_KV_HEADS, HEAD_DIM), zeros(depth, b, CONV_F - 1, D_FF),
        p, bias, nb=1, tt=_tile_rows(s), cq=CHUNK, masked_history=True)
    bs, ss = x_sample.shape[:2]
    sample = _trunk(
        x_sample, PAST_LEN, state_conv_a, state_conv_b, state_pool_c, cache_k_d, cache_v_d,
        state_ffn_conv, p, bias, nb=bs, tt=ss, cq=ss, masked_history=False)
    return (prompt[0], sample[0]) + tuple(prompt[1:]) + tuple(sample[1:])
```

```python
import functools
import math

import jax
import jax.numpy as jnp
from jax import lax
from jax.experimental import pallas as pl
from jax.experimental.pallas import tpu as pltpu

F32 = jnp.float32
BF16 = jnp.bfloat16

D_MODEL = 1024
DEPTH = 4
PAST_LEN = 4096
CHUNK = 64
D_A = D_MODEL // 2
CONV_A = 31
D_B = D_MODEL // 2
CONV_B = 3
POOL_WINDOWS = (2, 4, 8, 16)
D_C = D_MODEL // 4
POOL_GROUP = D_C // 4
POOL_HIST = 15
HEAD_DIM = 64
N_Q_HEADS = (D_MODEL - D_C) // HEAD_DIM
N_KV_HEADS = 4
GQA = N_Q_HEADS // N_KV_HEADS
D_ATT = N_Q_HEADS * HEAD_DIM
D_KV = N_KV_HEADS * HEAD_DIM
WINDOW = 128
NUM_BUCKETS = 32
MAX_DISTANCE = 128
ATT_SCALE = HEAD_DIM ** -0.5
NEG_INF = -1e30
D_FF = 2816
CONV_F = 3
LN_EPS = 1e-5
ALPHA = (2 * DEPTH) ** 0.25

LANES = 128
SUBLANES = 8
HIST_A_PAD = 32
HIST_B_PAD = SUBLANES
HIST_C_PAD = 16
HIST_F_PAD = SUBLANES
FF_CHUNK = 256
N_FF_CHUNKS = D_FF // FF_CHUNK
N_PAIRS = N_KV_HEADS // 2
PAIR_ROWS = 2 * GQA
CONV_ROW_BLOCK = 32
VMEM_LIMIT = 56 * 1024 * 1024


def _layer_norm(v, g, b):
    mu = jnp.mean(v, axis=-1, keepdims=True)
    d = v - mu
    var = jnp.mean(d * d, axis=-1, keepdims=True)
    return d * lax.rsqrt(var + LN_EPS) * g + b


def _dot(a, b):
    return jnp.dot(a, b, preferred_element_type=F32)


def _ffn_kernel(x_ref, hist_ref, wg_ref, wv_ref, cw_ref, cb_ref, wd_ref, lg_ref, lb_ref,
                y_ref, st_ref, xb_scr, g_scr0, g_scr1, v_scr0, v_scr1, acc_scr, *, nb, tt):
    m = nb * tt

    @pl.when(pl.program_id(1) == 0)
    def _():
        st_ref[...] = hist_ref[...]

    x = x_ref[...].reshape(m, D_MODEL)
    xb_scr[...] = x.astype(BF16)
    acc_scr[...] = jnp.zeros_like(acc_scr)

    def up(j, g_scr, v_scr):
        xb = xb_scr[...]
        g_scr[:, 0:HIST_F_PAD, :] = st_ref[j]
        g_scr[:, HIST_F_PAD:HIST_F_PAD + tt, :] = _dot(xb, wg_ref[j]).reshape(nb, tt, FF_CHUNK)
        v_scr[...] = _dot(xb, wv_ref[j]).reshape(nb, tt, FF_CHUNK)
        st_ref[j] = g_scr[:, tt:tt + HIST_F_PAD, :]

    def down(j, g_scr, v_scr):
        cw = cw_ref[j]
        conv = (cw[0:1, :][None] * g_scr[:, HIST_F_PAD - 2:HIST_F_PAD - 2 + tt, :]
                + cw[1:2, :][None] * g_scr[:, HIST_F_PAD - 1:HIST_F_PAD - 1 + tt, :]
                + cw[2:3, :][None] * g_scr[:, HIST_F_PAD:HIST_F_PAD + tt, :] + cb_ref[j][None])
        h = jax.nn.gelu(conv) * v_scr[...]
        hb = h.reshape(m, FF_CHUNK).astype(BF16)
        acc_scr[...] += _dot(hb, wd_ref[j])

    up(0, g_scr0, v_scr0)

    def body(i, carry):
        j = 2 * i
        up(j + 1, g_scr1, v_scr1)
        down(j, g_scr0, v_scr0)
        up(j + 2, g_scr0, v_scr0)
        down(j + 1, g_scr1, v_scr1)
        return carry

    lax.fori_loop(0, (N_FF_CHUNKS - 1) // 2, body, 0)
    down(N_FF_CHUNKS - 1, g_scr0, v_scr0)
    y = ALPHA * x + acc_scr[...]
    y_ref[...] = _layer_norm(y, lg_ref[...], lb_ref[...]).reshape(nb, tt, D_MODEL)


def _ffn_call(x, hist, wg, wv, cw, cb, wd, lg, lb, *, nb, tt):
    b, s, _ = x.shape
    grid = (b // nb, s // tt)
    const = lambda *shape: pl.BlockSpec(shape, lambda i, t: (0,) * len(shape))
    return pl.pallas_call(
        functools.partial(_ffn_kernel, nb=nb, tt=tt),
        grid=grid,
        in_specs=[
            pl.BlockSpec((nb, tt, D_MODEL), lambda i, t: (i, t, 0)),
            pl.BlockSpec((N_FF_CHUNKS, nb, HIST_F_PAD, FF_CHUNK), lambda i, t: (0, i, 0, 0)),
            const(N_FF_CHUNKS, D_MODEL, FF_CHUNK),
            const(N_FF_CHUNKS, D_MODEL, FF_CHUNK),
            const(N_FF_CHUNKS, CONV_F, FF_CHUNK),
            const(N_FF_CHUNKS, 1, FF_CHUNK),
            const(N_FF_CHUNKS, FF_CHUNK, D_MODEL),
            const(1, D_MODEL),
            const(1, D_MODEL),
        ],
        out_specs=[
            pl.BlockSpec((nb, tt, D_MODEL), lambda i, t: (i, t, 0)),
            pl.BlockSpec((N_FF_CHUNKS, nb, HIST_F_PAD, FF_CHUNK), lambda i, t: (0, i, 0, 0)),
        ],
        out_shape=[
            jax.ShapeDtypeStruct((b, s, D_MODEL), F32),
            jax.ShapeDtypeStruct((N_FF_CHUNKS, b, HIST_F_PAD, FF_CHUNK), F32),
        ],
        scratch_shapes=[
            pltpu.VMEM((nb * tt, D_MODEL), BF16),
            pltpu.VMEM((nb, HIST_F_PAD + tt, FF_CHUNK), F32),
            pltpu.VMEM((nb, HIST_F_PAD + tt, FF_CHUNK), F32),
            pltpu.VMEM((nb, tt, FF_CHUNK), F32),
            pltpu.VMEM((nb, tt, FF_CHUNK), F32),
            pltpu.VMEM((nb * tt, D_MODEL), F32),
        ],
        compiler_params=pltpu.CompilerParams(
            dimension_semantics=("parallel", "arbitrary"), vmem_limit_bytes=VMEM_LIMIT),
    )(x, hist, wg, wv, cw, cb, wd, lg, lb)


def _even_kernel(x_ref, ha_ref, hb_ref, win_ref, caw_ref, cab_ref, lag_ref, lab_ref, cbw_ref,
                 wout_ref, lg_ref, lb_ref, y_ref, sta_ref, stb_ref, u_scr, a_scr, *, nb, tt):
    m = nb * tt

    @pl.when(pl.program_id(1) == 0)
    def _():
        sta_ref[...] = ha_ref[...]
        stb_ref[...] = hb_ref[...]

    x = x_ref[...].reshape(m, D_MODEL)
    xb = x.astype(BF16)

    def proj(i):
        return _dot(xb, win_ref[:, i * D_A:(i + 1) * D_A])

    u = proj(0) * jax.nn.sigmoid(proj(1))
    u_scr[:, 0:HIST_A_PAD, :] = sta_ref[...]
    u_scr[:, HIST_A_PAD:HIST_A_PAD + tt, :] = u.reshape(nb, tt, D_A)
    sta_ref[...] = u_scr[:, tt:tt + HIST_A_PAD, :]
    rb = min(CONV_ROW_BLOCK, tt)
    first = HIST_A_PAD - (CONV_A - 1)
    cab = cab_ref[...]
    lag = lag_ref[...]
    lab = lab_ref[...]
    for n in range(nb):
        for r0 in range(0, tt, rb):
            acc = jnp.broadcast_to(cab, (rb, D_A))
            for k in range(CONV_A):
                acc = acc + caw_ref[k:k + 1, :] * u_scr[n, first + r0 + k:first + r0 + k + rb, :]
            a = jax.nn.silu(_layer_norm(acc, lag, lab))
            a_scr[n * tt + r0:n * tt + r0 + rb, :] = a.astype(BF16)

    v = (proj(3) * proj(4)).reshape(nb, tt, D_B)
    u_scr[:, 0:HIST_B_PAD, :] = stb_ref[...]
    u_scr[:, HIST_B_PAD:HIST_B_PAD + tt, :] = v
    stb_ref[...] = u_scr[:, tt:tt + HIST_B_PAD, :]
    cbw = cbw_ref[...]
    conv_b = (cbw[0:1, :][None] * u_scr[:, HIST_B_PAD - 2:HIST_B_PAD - 2 + tt, :]
              + cbw[1:2, :][None] * u_scr[:, HIST_B_PAD - 1:HIST_B_PAD - 1 + tt, :]
              + cbw[2:3, :][None] * v)
    bo = proj(2) * conv_b.reshape(m, D_B)

    mix = _dot(a_scr[...], wout_ref[0:D_A, :]) + _dot(bo.astype(BF16), wout_ref[D_A:D_A + D_B, :])
    y = ALPHA * x + mix
    y_ref[...] = _layer_norm(y, lg_ref[...], lb_ref[...]).reshape(nb, tt, D_MODEL)


def _even_call(x, ha, hb, win, caw, cab, lag, lab, cbw, wout, lg, lb, *, nb, tt):
    b, s, _ = x.shape
    grid = (b // nb, s // tt)
    const = lambda *shape: pl.BlockSpec(shape, lambda i, t: (0,) * len(shape))
    return pl.pallas_call(
        functools.partial(_even_kernel, nb=nb, tt=tt),
        grid=grid,
        in_specs=[
            pl.BlockSpec((nb, tt, D_MODEL), lambda i, t: (i, t, 0)),
            pl.BlockSpec((nb, HIST_A_PAD, D_A), lambda i, t: (i, 0, 0)),
            pl.BlockSpec((nb, HIST_B_PAD, D_B), lambda i, t: (i, 0, 0)),
            const(D_MODEL, 2 * D_A + 3 * D_B),
            const(CONV_A, D_A),
            const(1, D_A),
            const(1, D_A),
            const(1, D_A),
            const(CONV_B, D_B),
            const(D_A + D_B, D_MODEL),
            const(1, D_MODEL),
            const(1, D_MODEL),
        ],
        out_specs=[
            pl.BlockSpec((nb, tt, D_MODEL), lambda i, t: (i, t, 0)),
            pl.BlockSpec((nb, HIST_A_PAD, D_A), lambda i, t: (i, 0, 0)),
            pl.BlockSpec((nb, HIST_B_PAD, D_B), lambda i, t: (i, 0, 0)),
        ],
        out_shape=[
            jax.ShapeDtypeStruct((b, s, D_MODEL), F32),
            jax.ShapeDtypeStruct((b, HIST_A_PAD, D_A), F32),
            jax.ShapeDtypeStruct((b, HIST_B_PAD, D_B), F32),
        ],
        scratch_shapes=[
            pltpu.VMEM((nb, HIST_A_PAD + tt, D_A), F32),
            pltpu.VMEM((nb * tt, D_A), BF16),
        ],
        compiler_params=pltpu.CompilerParams(
            dimension_semantics=("parallel", "arbitrary"), vmem_limit_bytes=VMEM_LIMIT),
    )(x, ha, hb, win, caw, cab, lag, lab, cbw, wout, lg, lb)


def _odd_kernel(x_ref, hc_ref, ck_ref, cv_ref, win_ref, wpool_ref, pscale_ref, bias_ref, sink_ref,
                wout_ref, lg_ref, lb_ref, y_ref, stc_ref, nk_ref, nv_ref,
                c_scr, k_scr, v_scr, att_scr, *, nb, tt, cq, pos0, masked_history):
    m = nb * tt
    kw = WINDOW + cq
    t_idx = pl.program_id(1)

    @pl.when(t_idx == 0)
    def _():
        stc_ref[...] = hc_ref[...]
        nk_ref[...] = ck_ref[...]
        nv_ref[...] = cv_ref[...]

    x = x_ref[...].reshape(m, D_MODEL)
    xb = x.astype(BF16)
    q_off = D_C
    k_off = D_C + D_ATT
    v_off = k_off + D_KV

    c = _dot(xb, win_ref[:, 0:D_C]).reshape(nb, tt, D_C)
    c_scr[:, 0:HIST_C_PAD, :] = stc_ref[...]
    c_scr[:, HIST_C_PAD:HIST_C_PAD + tt, :] = c
    stc_ref[...] = c_scr[:, tt:tt + HIST_C_PAD, :]

    def window_sum(lane0, lo, hi):
        acc = None
        for j in range(lo, hi):
            r = c_scr[:, HIST_C_PAD - j:HIST_C_PAD - j + tt, lane0:lane0 + LANES]
            acc = r if acc is None else acc + r
        return acc

    row = lax.broadcasted_iota(jnp.int32, (nb, tt, LANES), 1)
    lane = lax.broadcasted_iota(jnp.int32, (nb, tt, LANES), 2)
    pos1 = row + (pos0 + 1) + t_idx * tt
    low = lane < POOL_GROUP
    pooled = []
    for tile, (w_small, w_big) in enumerate(((2, 4), (8, 16))):
        s_small = window_sum(tile * LANES, 0, w_small)
        s_big = s_small + window_sum(tile * LANES, w_small, w_big)
        total = jnp.where(low, s_small, s_big)
        cnt = jnp.minimum(pos1, jnp.where(low, w_small, w_big)).astype(F32)
        pooled.append(total / cnt - c[:, :, tile * LANES:(tile + 1) * LANES])
    pooled = jnp.concatenate(pooled, axis=-1).reshape(m, D_C)
    pooled = _dot(pooled.astype(BF16), wpool_ref[...]) * pscale_ref[...]

    k_scr[:, 0:WINDOW, :] = nk_ref[...]
    v_scr[:, 0:WINDOW, :] = nv_ref[...]
    k_scr[:, WINDOW:WINDOW + tt, :] = _dot(xb, win_ref[:, k_off:k_off + D_KV]).reshape(nb, tt, D_KV)
    v_scr[:, WINDOW:WINDOW + tt, :] = _dot(xb, win_ref[:, v_off:v_off + D_KV]).reshape(nb, tt, D_KV)
    nk_ref[...] = k_scr[:, tt:tt + WINDOW, :]
    nv_ref[...] = v_scr[:, tt:tt + WINDOW, :]
    q = _dot(xb, win_ref[:, q_off:q_off + D_ATT]).reshape(nb, tt, D_ATT)

    rows = PAIR_ROWS * cq
    q_lane = lax.broadcasted_iota(jnp.int32, (cq, LANES), 1)
    key_idx = lax.broadcasted_iota(jnp.int32, (rows, kw), 1)
    out_lane = lax.broadcasted_iota(jnp.int32, (cq, LANES), 1)
    for n in range(nb):
        for ci in range(tt // cq):
            r0 = ci * cq
            for p in range(N_PAIRS):
                tiles = [q[n, r0:r0 + cq, (p * GQA + g) * LANES:(p * GQA + g + 1) * LANES]
                         for g in range(GQA)]
                lhs = ([jnp.where(q_lane < HEAD_DIM, tl, 0.0) for tl in tiles]
                       + [jnp.where(q_lane >= HEAD_DIM, tl, 0.0) for tl in tiles])
                lhs = jnp.concatenate(lhs, axis=0).astype(BF16)
                kt = k_scr[n, r0:r0 + kw, p * LANES:(p + 1) * LANES].astype(BF16)
                vt = v_scr[n, r0:r0 + kw, p * LANES:(p + 1) * LANES].astype(BF16)
                s = lax.dot_general(lhs, kt, (((1,), (1,)), ((), ())),
                                    preferred_element_type=F32)
                s = s * ATT_SCALE + bias_ref[p]
                if masked_history:
                    first_valid = WINDOW - (t_idx * tt + r0)
                    s = jnp.where(key_idx >= first_valid, s, NEG_INF)
                sink = sink_ref[p]
                mx = jnp.maximum(jnp.max(s, axis=-1, keepdims=True), sink)
                e = jnp.exp(s - mx)
                den = jnp.sum(e, axis=-1, keepdims=True) + jnp.exp(sink - mx)
                pv = _dot(e.astype(BF16), vt) * (1.0 / den)
                for g in range(GQA):
                    o = jnp.where(out_lane < HEAD_DIM, pv[g * cq:(g + 1) * cq, :],
                                  pv[(GQA + g) * cq:(GQA + g + 1) * cq, :])
                    col = (p * GQA + g) * LANES
                    att_scr[n * tt + r0:n * tt + r0 + cq, col:col + LANES] = o.astype(BF16)

    mix = (_dot(pooled.astype(BF16), wout_ref[0:D_C, :])
           + _dot(att_scr[...], wout_ref[D_C:D_C + D_ATT, :]))
    y = ALPHA * x + mix
    y_ref[...] = _layer_norm(y, lg_ref[...], lb_ref[...]).reshape(nb, tt, D_MODEL)


def _odd_call(x, hc, ck, cv, win, wpool, pscale, bias, sink, wout, lg, lb, *,
              nb, tt, cq, pos0, masked_history):
    b, s, _ = x.shape
    grid = (b // nb, s // tt)
    kw = WINDOW + cq
    rows = PAIR_ROWS * cq
    const = lambda *shape: pl.BlockSpec(shape, lambda i, t: (0,) * len(shape))
    return pl.pallas_call(
        functools.partial(_odd_kernel, nb=nb, tt=tt, cq=cq, pos0=pos0,
                          masked_history=masked_history),
        grid=grid,
        in_specs=[
            pl.BlockSpec((nb, tt, D_MODEL), lambda i, t: (i, t, 0)),
            pl.BlockSpec((nb, HIST_C_PAD, D_C), lambda i, t: (i, 0, 0)),
            pl.BlockSpec((nb, WINDOW, D_KV), lambda i, t: (i, 0, 0)),
            pl.BlockSpec((nb, WINDOW, D_KV), lambda i, t: (i, 0, 0)),
            const(D_MODEL, D_C + D_ATT + 2 * D_KV),
            const(D_C, D_C),
            const(1, D_C),
            const(N_PAIRS, rows, kw),
            const(N_PAIRS, rows, 1),
            const(D_C + D_ATT, D_MODEL),
            const(1, D_MODEL),
            const(1, D_MODEL),
        ],
        out_specs=[
            pl.BlockSpec((nb, tt, D_MODEL), lambda i, t: (i, t, 0)),
            pl.BlockSpec((nb, HIST_C_PAD, D_C), lambda i, t: (i, 0, 0)),
            pl.BlockSpec((nb, WINDOW, D_KV), lambda i, t: (i, 0, 0)),
            pl.BlockSpec((nb, WINDOW, D_KV), lambda i, t: (i, 0, 0)),
        ],
        out_shape=[
            jax.ShapeDtypeStruct((b, s, D_MODEL), F32),
            jax.ShapeDtypeStruct((b, HIST_C_PAD, D_C), F32),
            jax.ShapeDtypeStruct((b, WINDOW, D_KV), F32),
            jax.ShapeDtypeStruct((b, WINDOW, D_KV), F32),
        ],
        scratch_shapes=[
            pltpu.VMEM((nb, HIST_C_PAD + tt, D_C), F32),
            pltpu.VMEM((nb, WINDOW + tt, D_KV), F32),
            pltpu.VMEM((nb, WINDOW + tt, D_KV), F32),
            pltpu.VMEM((nb * tt, D_ATT), BF16),
        ],
        compiler_params=pltpu.CompilerParams(
            dimension_semantics=("parallel", "arbitrary"), vmem_limit_bytes=VMEM_LIMIT),
    )(x, hc, ck, cv, win, wpool, pscale, bias, sink, wout, lg, lb)


def _bias_kernel(bucket_ref, rb_ref, out_ref):
    bucket = bucket_ref[...]
    for p in range(N_PAIRS):
        for r in range(PAIR_ROWS):
            head = (2 * p + r // GQA) * GQA + r % GQA
            acc = jnp.zeros((CHUNK, WINDOW + CHUNK), F32)
            for b in range(NUM_BUCKETS):
                acc = jnp.where(bucket == b, rb_ref[b, head], acc)
            out_ref[p, r * CHUNK:(r + 1) * CHUNK, :] = acc


def _bias_call(bucket, rel_bias):
    return pl.pallas_call(
        _bias_kernel,
        in_specs=[
            pl.BlockSpec(memory_space=pltpu.VMEM),
            pl.BlockSpec(memory_space=pltpu.SMEM),
        ],
        out_specs=pl.BlockSpec(memory_space=pltpu.VMEM),
        out_shape=jax.ShapeDtypeStruct((N_PAIRS, PAIR_ROWS * CHUNK, WINDOW + CHUNK), F32),
    )(bucket, rel_bias)


def _t5_bucket(rel):
    nb = NUM_BUCKETS // 2
    max_exact = nb // 2
    ret = jnp.where(rel > 0, nb, 0)
    n = jnp.abs(rel)
    nf = jnp.maximum(n, 1).astype(jnp.float32)
    large = max_exact + (jnp.log(nf / max_exact) / math.log(MAX_DISTANCE / max_exact)
                         * (nb - max_exact)).astype(jnp.int32)
    large = jnp.minimum(large, nb - 1)
    return ret + jnp.where(n < max_exact, n, large)


def _head_order():
    order = []
    for p in range(N_PAIRS):
        for g in range(GQA):
            order += [(2 * p) * GQA + g, (2 * p + 1) * GQA + g]
    return order


def _pad_rows(h, rows):
    pad = rows - h.shape[-2]
    cfg = [(0, 0)] * (h.ndim - 2) + [(pad, 0), (0, 0)]
    return jnp.pad(h, cfg)


def _prepare(w):
    order = _head_order()
    cols = jnp.asarray([h * HEAD_DIM + d for h in order for d in range(HEAD_DIM)], jnp.int32)
    p = {}
    p['w_in_ab'] = w['w_in_ab'].astype(BF16)
    p['w_out_ab'] = w['w_out_ab'].astype(BF16)
    win = w['w_in_cd']
    q_cols = jnp.take(win[:, :, D_C:D_C + D_ATT], cols, axis=2)
    p['w_in_cd'] = jnp.concatenate([win[:, :, :D_C], q_cols, win[:, :, D_C + D_ATT:]], axis=2).astype(BF16)
    wout = w['w_out_cd']
    att_rows = jnp.take(wout[:, D_C:, :], cols, axis=1)
    p['w_out_cd'] = jnp.concatenate([wout[:, :D_C, :], att_rows], axis=1).astype(BF16)
    n_odd = win.shape[0]
    wp = jnp.zeros((n_odd, D_C, D_C), F32)
    for g in range(4):
        sl = slice(g * POOL_GROUP, (g + 1) * POOL_GROUP)
        wp = wp.at[:, sl, sl].set(w['w_pool'][:, g])
    p['w_pool'] = wp.astype(BF16)
    up = w['w_ffn_up'].astype(BF16)
    depth = up.shape[0]
    split = lambda a: a.reshape(depth, D_MODEL, N_FF_CHUNKS, FF_CHUNK).transpose(0, 2, 1, 3)
    p['w_up_g'] = split(up[:, :, :D_FF])
    p['w_up_v'] = split(up[:, :, D_FF:])
    p['ffn_conv_w'] = w['ffn_conv_w'].reshape(depth, CONV_F, N_FF_CHUNKS, FF_CHUNK).transpose(0, 2, 1, 3)
    p['ffn_conv_b'] = w['ffn_conv_b'].reshape(depth, N_FF_CHUNKS, 1, FF_CHUNK)
    p['w_down'] = w['w_ffn_down'].astype(BF16).reshape(depth, N_FF_CHUNKS, FF_CHUNK, D_MODEL)
    heads = jnp.asarray([[(2 * pp + r // GQA) * GQA + r % GQA for r in range(PAIR_ROWS)]
                         for pp in range(N_PAIRS)], jnp.int32)
    p['sinks'] = jnp.take(w['attn_sinks'], heads, axis=1)
    for name in ('conv_a_w', 'conv_a_b', 'ln_a_g', 'ln_a_b', 'conv_b_w', 'pool_scale',
                 'ln_mix_g', 'ln_mix_b', 'ln_ffn_g', 'ln_ffn_b'):
        p[name] = w[name]
    return p


def _trunk(x, pos0, hist_a, hist_b, hist_c, cache_k, cache_v, hist_f, p, bias, *,
           nb, tt, cq, masked_history):
    b = x.shape[0]
    row = lambda v: v.reshape(1, -1)
    bias_t = bias.reshape(N_PAIRS, PAIR_ROWS, CHUNK, WINDOW + CHUNK)[:, :, :cq, :WINDOW + cq]
    bias_t = bias_t.reshape(N_PAIRS, PAIR_ROWS * cq, WINDOW + cq)
    st_a, st_b, st_c, st_k, st_v, st_f = [], [], [], [], [], []
    for layer in range(DEPTH):
        i = layer // 2
        if layer % 2 == 0:
            x, sa, sb = _even_call(
                x, _pad_rows(hist_a[i], HIST_A_PAD), _pad_rows(hist_b[i], HIST_B_PAD),
                p['w_in_ab'][i], p['conv_a_w'][i], row(p['conv_a_b'][i]), row(p['ln_a_g'][i]),
                row(p['ln_a_b'][i]), p['conv_b_w'][i], p['w_out_ab'][i],
                row(p['ln_mix_g'][layer]), row(p['ln_mix_b'][layer]), nb=nb, tt=tt)
            st_a.append(sa[:, HIST_A_PAD - (CONV_A - 1):])
            st_b.append(sb[:, HIST_B_PAD - (CONV_B - 1):])
        else:
            sink = jnp.repeat(p['sinks'][i], cq, axis=1)[..., None]
            x, sc, sk, sv = _odd_call(
                x, _pad_rows(hist_c[i], HIST_C_PAD),
                cache_k[i].reshape(b, WINDOW, D_KV), cache_v[i].reshape(b, WINDOW, D_KV),
                p['w_in_cd'][i], p['w_pool'][i], row(p['pool_scale'][i]), bias_t, sink,
                p['w_out_cd'][i], row(p['ln_mix_g'][layer]), row(p['ln_mix_b'][layer]),
                nb=nb, tt=tt, cq=cq, pos0=pos0, masked_history=masked_history)
            st_c.append(sc[:, HIST_C_PAD - POOL_HIST:])
            st_k.append(sk.reshape(b, WINDOW, N_KV_HEADS, HEAD_DIM))
            st_v.append(sv.reshape(b, WINDOW, N_KV_HEADS, HEAD_DIM))
        hf = _pad_rows(hist_f[layer], HIST_F_PAD)
        hf = hf.reshape(b, HIST_F_PAD, N_FF_CHUNKS, FF_CHUNK).transpose(2, 0, 1, 3)
        x, sf = _ffn_call(
            x, hf, p['w_up_g'][layer], p['w_up_v'][layer], p['ffn_conv_w'][layer],
            p['ffn_conv_b'][layer], p['w_down'][layer],
            row(p['ln_ffn_g'][layer]), row(p['ln_ffn_b'][layer]), nb=nb, tt=tt)
        sf = sf[:, :, HIST_F_PAD - (CONV_F - 1):, :].transpose(1, 2, 0, 3)
        st_f.append(sf.reshape(b, CONV_F - 1, D_FF))
    return (x, jnp.stack(st_a), jnp.stack(st_b), jnp.stack(st_c), jnp.stack(st_k),
            jnp.stack(st_v), jnp.stack(st_f))


def _tile_rows(s):
    for cand in (256, 128, 64):
        if s % cand == 0:
            return cand
    return s


def kernel(x_prompt, x_sample, state_conv_a, state_conv_b, state_pool_c, cache_k_d, cache_v_d, state_ffn_conv, w_in_ab, conv_a_w, conv_a_b, ln_a_g, ln_a_b, conv_b_w, w_out_ab, w_in_cd, w_pool, pool_scale, attn_sinks, rel_bias, w_out_cd, w_ffn_up, ffn_conv_w, ffn_conv_b, w_ffn_down, ln_mix_g, ln_mix_b, ln_ffn_g, ln_ffn_b):
    w = dict(w_in_ab=w_in_ab, conv_a_w=conv_a_w, conv_a_b=conv_a_b, ln_a_g=ln_a_g, ln_a_b=ln_a_b,
             conv_b_w=conv_b_w, w_out_ab=w_out_ab, w_in_cd=w_in_cd, w_pool=w_pool, pool_scale=pool_scale,
             attn_sinks=attn_sinks, w_out_cd=w_out_cd, w_ffn_up=w_ffn_up,
             ffn_conv_w=ffn_conv_w, ffn_conv_b=ffn_conv_b, w_ffn_down=w_ffn_down,
             ln_mix_g=ln_mix_g, ln_mix_b=ln_mix_b, ln_ffn_g=ln_ffn_g, ln_ffn_b=ln_ffn_b)
    p = _prepare(w)
    n_even, n_odd, depth = w_in_ab.shape[0], w_in_cd.shape[0], w_ffn_up.shape[0]
    rel = (jnp.arange(WINDOW + CHUNK)[None, :] - WINDOW - jnp.arange(CHUNK)[:, None]).astype(jnp.int32)
    bias = _bias_call(_t5_bucket(rel).astype(jnp.int32), rel_bias)

    b, s = x_prompt.shape[:2]
    zeros = lambda *shape: jnp.zeros(shape, F32)
    prompt = _trunk(
        x_prompt, 0, zeros(n_even, b, CONV_A - 1, D_A), zeros(n_even, b, CONV_B - 1, D_B),
        zeros(n_odd, b, POOL_HIST, D_C), zeros(n_odd, b, WINDOW, N_KV_HEADS, HEAD_DIM),
        zeros(n_odd, b, WINDOW, N_KV_HEADS, HEAD_DIM), zeros(depth, b, CONV_F - 1, D_FF),
        p, bias, nb=1, tt=_tile_rows(s), cq=CHUNK, masked_history=True)
    bs, ss = x_sample.shape[:2]
    sample = _trunk(
        x_sample, PAST_LEN, state_conv_a, state_conv_b, state_pool_c, cache_k_d, cache_v_d,
        state_ffn_conv, p, bias, nb=bs, tt=ss, cq=ss, masked_history=False)
    return (prompt[0], sample[0]) + tuple(prompt[1:]) + tuple(sample[1:])
```

```python
import functools
import math

import jax
import jax.numpy as jnp
from jax import lax
from jax.experimental import pallas as pl
from jax.experimental.pallas import tpu as pltpu

F32 = jnp.float32
BF16 = jnp.bfloat16

D_MODEL = 1024
DEPTH = 4
PAST_LEN = 4096
CHUNK = 64
D_A = D_MODEL // 2
CONV_A = 31
D_B = D_MODEL // 2
CONV_B = 3
POOL_WINDOWS = (2, 4, 8, 16)
D_C = D_MODEL // 4
POOL_GROUP = D_C // 4
POOL_HIST = 15
HEAD_DIM = 64
N_Q_HEADS = (D_MODEL - D_C) // HEAD_DIM
N_KV_HEADS = 4
GQA = N_Q_HEADS // N_KV_HEADS
D_ATT = N_Q_HEADS * HEAD_DIM
D_KV = N_KV_HEADS * HEAD_DIM
WINDOW = 128
NUM_BUCKETS = 32
MAX_DISTANCE = 128
ATT_SCALE = HEAD_DIM ** -0.5
NEG_INF = -1e30
D_FF = 2816
CONV_F = 3
LN_EPS = 1e-5
ALPHA = (2 * DEPTH) ** 0.25

LANES = 128
SUBLANES = 8
HIST_A_PAD = 32
HIST_B_PAD = SUBLANES
HIST_C_PAD = 16
HIST_F_PAD = SUBLANES
FF_CHUNK = 256
N_FF_CHUNKS = D_FF // FF_CHUNK
N_PAIRS = N_KV_HEADS // 2
PAIR_ROWS = 2 * GQA
CONV_ROW_BLOCK = 32
KEY_PAD = 256
VMEM_LIMIT = 56 * 1024 * 1024


def _layer_norm(v, g, b):
    mu = jnp.mean(v, axis=-1, keepdims=True)
    d = v - mu
    var = jnp.mean(d * d, axis=-1, keepdims=True)
    return d * lax.rsqrt(var + LN_EPS) * g + b


def _dot(a, b):
    return jnp.dot(a, b, preferred_element_type=F32)


def _ffn_kernel(x_ref, hist_ref, wg_ref, wv_ref, cw_ref, cb_ref, wd_ref, lg_ref, lb_ref,
                y_ref, st_ref, xb_scr, g_scr0, g_scr1, v_scr0, v_scr1, acc_scr, *, nb, tt):
    m = nb * tt

    @pl.when(pl.program_id(1) == 0)
    def _():
        st_ref[...] = hist_ref[...]

    x = x_ref[...].reshape(m, D_MODEL)
    xb_scr[...] = x.astype(BF16)
    acc_scr[...] = jnp.zeros_like(acc_scr)

    def up(j, g_scr, v_scr):
        xb = xb_scr[...]
        g_scr[:, 0:HIST_F_PAD, :] = st_ref[j]
        g_scr[:, HIST_F_PAD:HIST_F_PAD + tt, :] = _dot(xb, wg_ref[j]).reshape(nb, tt, FF_CHUNK)
        v_scr[...] = _dot(xb, wv_ref[j]).reshape(nb, tt, FF_CHUNK)
        st_ref[j] = g_scr[:, tt:tt + HIST_F_PAD, :]

    def down(j, g_scr, v_scr):
        cw = cw_ref[j]
        conv = (cw[0:1, :][None] * g_scr[:, HIST_F_PAD - 2:HIST_F_PAD - 2 + tt, :]
                + cw[1:2, :][None] * g_scr[:, HIST_F_PAD - 1:HIST_F_PAD - 1 + tt, :]
                + cw[2:3, :][None] * g_scr[:, HIST_F_PAD:HIST_F_PAD + tt, :] + cb_ref[j][None])
        h = jax.nn.gelu(conv) * v_scr[...]
        hb = h.reshape(m, FF_CHUNK).astype(BF16)
        acc_scr[...] += _dot(hb, wd_ref[j])

    up(0, g_scr0, v_scr0)

    def body(i, carry):
        j = 2 * i
        up(j + 1, g_scr1, v_scr1)
        down(j, g_scr0, v_scr0)
        up(j + 2, g_scr0, v_scr0)
        down(j + 1, g_scr1, v_scr1)
        return carry

    lax.fori_loop(0, (N_FF_CHUNKS - 1) // 2, body, 0)
    down(N_FF_CHUNKS - 1, g_scr0, v_scr0)
    y = ALPHA * x + acc_scr[...]
    y_ref[...] = _layer_norm(y, lg_ref[...], lb_ref[...]).reshape(nb, tt, D_MODEL)


def _ffn_call(x, hist, wg, wv, cw, cb, wd, lg, lb, *, nb, tt):
    b, s, _ = x.shape
    grid = (b // nb, s // tt)
    const = lambda *shape: pl.BlockSpec(shape, lambda i, t: (0,) * len(shape))
    return pl.pallas_call(
        functools.partial(_ffn_kernel, nb=nb, tt=tt),
        grid=grid,
        in_specs=[
            pl.BlockSpec((nb, tt, D_MODEL), lambda i, t: (i, t, 0)),
            pl.BlockSpec((N_FF_CHUNKS, nb, HIST_F_PAD, FF_CHUNK), lambda i, t: (0, i, 0, 0)),
            const(N_FF_CHUNKS, D_MODEL, FF_CHUNK),
            const(N_FF_CHUNKS, D_MODEL, FF_CHUNK),
            const(N_FF_CHUNKS, CONV_F, FF_CHUNK),
            const(N_FF_CHUNKS, 1, FF_CHUNK),
            const(N_FF_CHUNKS, FF_CHUNK, D_MODEL),
            const(1, D_MODEL),
            const(1, D_MODEL),
        ],
        out_specs=[
            pl.BlockSpec((nb, tt, D_MODEL), lambda i, t: (i, t, 0)),
            pl.BlockSpec((N_FF_CHUNKS, nb, HIST_F_PAD, FF_CHUNK), lambda i, t: (0, i, 0, 0)),
        ],
        out_shape=[
            jax.ShapeDtypeStruct((b, s, D_MODEL), F32),
            jax.ShapeDtypeStruct((N_FF_CHUNKS, b, HIST_F_PAD, FF_CHUNK), F32),
        ],
        scratch_shapes=[
            pltpu.VMEM((nb * tt, D_MODEL), BF16),
            pltpu.VMEM((nb, HIST_F_PAD + tt, FF_CHUNK), F32),
            pltpu.VMEM((nb, HIST_F_PAD + tt, FF_CHUNK), F32),
            pltpu.VMEM((nb, tt, FF_CHUNK), F32),
            pltpu.VMEM((nb, tt, FF_CHUNK), F32),
            pltpu.VMEM((nb * tt, D_MODEL), F32),
        ],
        compiler_params=pltpu.CompilerParams(
            dimension_semantics=("parallel", "arbitrary"), vmem_limit_bytes=VMEM_LIMIT),
    )(x, hist, wg, wv, cw, cb, wd, lg, lb)


def _even_kernel(x_ref, ha_ref, hb_ref, win_ref, caw_ref, cab_ref, lag_ref, lab_ref, cbw_ref,
                 wout_ref, lg_ref, lb_ref, y_ref, sta_ref, stb_ref, u_scr, sh_scr, a_scr, *, nb, tt):
    m = nb * tt

    @pl.when(pl.program_id(1) == 0)
    def _():
        sta_ref[...] = ha_ref[...]
        stb_ref[...] = hb_ref[...]

    x = x_ref[...].reshape(m, D_MODEL)
    xb = x.astype(BF16)

    def proj(i):
        return _dot(xb, win_ref[:, i * D_A:(i + 1) * D_A])

    u = proj(0) * jax.nn.sigmoid(proj(1))
    u_scr[:, 0:HIST_A_PAD, :] = sta_ref[...]
    u_scr[:, HIST_A_PAD:HIST_A_PAD + tt, :] = u.reshape(nb, tt, D_A)
    sta_ref[...] = u_scr[:, tt:tt + HIST_A_PAD, :]
    n_slabs = (HIST_A_PAD + tt) // SUBLANES
    sub = lax.broadcasted_iota(jnp.int32, (SUBLANES, D_A), 0)
    for n in range(nb):
        slabs = [u_scr[n, SUBLANES * i:SUBLANES * (i + 1), :] for i in range(n_slabs)]
        for r in range(1, SUBLANES):
            rot = [pltpu.roll(s, SUBLANES - r, 0) for s in slabs]
            for i in range(n_slabs - 1):
                sh_scr[r - 1, n, SUBLANES * i:SUBLANES * (i + 1), :] = jnp.where(
                    sub < SUBLANES - r, rot[i], rot[i + 1])

    rb = min(CONV_ROW_BLOCK, tt)
    first = HIST_A_PAD - (CONV_A - 1)
    cab = cab_ref[...]
    lag = lag_ref[...]
    lab = lab_ref[...]
    for n in range(nb):
        for r0 in range(0, tt, rb):
            acc = jnp.broadcast_to(cab[None], (rb // SUBLANES, SUBLANES, D_A))
            for k in range(CONV_A):
                tiles, r = divmod(first + k, SUBLANES)
                lo = r0 + SUBLANES * tiles
                win = u_scr[n, lo:lo + rb, :] if r == 0 else sh_scr[r - 1, n, lo:lo + rb, :]
                acc = acc + caw_ref[k][None] * win.reshape(rb // SUBLANES, SUBLANES, D_A)
            a = jax.nn.silu(_layer_norm(acc.reshape(rb, D_A), lag, lab))
            a_scr[n * tt + r0:n * tt + r0 + rb, :] = a.astype(BF16)

    v = (proj(3) * proj(4)).reshape(nb, tt, D_B)
    u_scr[:, 0:HIST_B_PAD, :] = stb_ref[...]
    u_scr[:, HIST_B_PAD:HIST_B_PAD + tt, :] = v
    stb_ref[...] = u_scr[:, tt:tt + HIST_B_PAD, :]
    cbw = cbw_ref[...]
    conv_b = (cbw[0:1, :][None] * u_scr[:, HIST_B_PAD - 2:HIST_B_PAD - 2 + tt, :]
              + cbw[1:2, :][None] * u_scr[:, HIST_B_PAD - 1:HIST_B_PAD - 1 + tt, :]
              + cbw[2:3, :][None] * v)
    bo = proj(2) * conv_b.reshape(m, D_B)

    mix = _dot(a_scr[...], wout_ref[0:D_A, :]) + _dot(bo.astype(BF16), wout_ref[D_A:D_A + D_B, :])
    y = ALPHA * x + mix
    y_ref[...] = _layer_norm(y, lg_ref[...], lb_ref[...]).reshape(nb, tt, D_MODEL)


def _even_call(x, ha, hb, win, caw, cab, lag, lab, cbw, wout, lg, lb, *, nb, tt):
    b, s, _ = x.shape
    grid = (b // nb, s // tt)
    const = lambda *shape: pl.BlockSpec(shape, lambda i, t: (0,) * len(shape))
    return pl.pallas_call(
        functools.partial(_even_kernel, nb=nb, tt=tt),
        grid=grid,
        in_specs=[
            pl.BlockSpec((nb, tt, D_MODEL), lambda i, t: (i, t, 0)),
            pl.BlockSpec((nb, HIST_A_PAD, D_A), lambda i, t: (i, 0, 0)),
            pl.BlockSpec((nb, HIST_B_PAD, D_B), lambda i, t: (i, 0, 0)),
            const(D_MODEL, 2 * D_A + 3 * D_B),
            const(CONV_A, SUBLANES, D_A),
            const(1, D_A),
            const(1, D_A),
            const(1, D_A),
            const(CONV_B, D_B),
            const(D_A + D_B, D_MODEL),
            const(1, D_MODEL),
            const(1, D_MODEL),
        ],
        out_specs=[
            pl.BlockSpec((nb, tt, D_MODEL), lambda i, t: (i, t, 0)),
            pl.BlockSpec((nb, HIST_A_PAD, D_A), lambda i, t: (i, 0, 0)),
            pl.BlockSpec((nb, HIST_B_PAD, D_B), lambda i, t: (i, 0, 0)),
        ],
        out_shape=[
            jax.ShapeDtypeStruct((b, s, D_MODEL), F32),
            jax.ShapeDtypeStruct((b, HIST_A_PAD, D_A), F32),
            jax.ShapeDtypeStruct((b, HIST_B_PAD, D_B), F32),
        ],
        scratch_shapes=[
            pltpu.VMEM((nb, HIST_A_PAD + tt, D_A), F32),
            pltpu.VMEM((SUBLANES - 1, nb, HIST_A_PAD + tt, D_A), F32),
            pltpu.VMEM((nb * tt, D_A), BF16),
        ],
        compiler_params=pltpu.CompilerParams(
            dimension_semantics=("parallel", "arbitrary"), vmem_limit_bytes=VMEM_LIMIT),
    )(x, ha, hb, win, caw, cab, lag, lab, cbw, wout, lg, lb)


def _odd_kernel(x_ref, hc_ref, ck_ref, cv_ref, win_ref, wpool_ref, pscale_ref, bias_ref,
                wout_ref, lg_ref, lb_ref, y_ref, stc_ref, nk_ref, nv_ref,
                c_scr, kb_scr, vb_scr, s_scr, p_scr, att_scr, *, nb, tt, cq, pos0, masked_history):
    m = nb * tt
    kw = WINDOW + cq
    t_idx = pl.program_id(1)

    @pl.when(t_idx == 0)
    def _():
        stc_ref[...] = hc_ref[...]
        nk_ref[...] = ck_ref[...]
        nv_ref[...] = cv_ref[...]

    x = x_ref[...].reshape(m, D_MODEL)
    xb = x.astype(BF16)
    q_off = D_C
    k_off = D_C + D_ATT
    v_off = k_off + D_KV

    c = _dot(xb, win_ref[:, 0:D_C]).reshape(nb, tt, D_C)
    c_scr[:, 0:HIST_C_PAD, :] = stc_ref[...]
    c_scr[:, HIST_C_PAD:HIST_C_PAD + tt, :] = c
    stc_ref[...] = c_scr[:, tt:tt + HIST_C_PAD, :]

    def window_sum(lane0, lo, hi):
        acc = None
        for j in range(lo, hi):
            r = c_scr[:, HIST_C_PAD - j:HIST_C_PAD - j + tt, lane0:lane0 + LANES]
            acc = r if acc is None else acc + r
        return acc

    row = lax.broadcasted_iota(jnp.int32, (nb, tt, LANES), 1)
    lane = lax.broadcasted_iota(jnp.int32, (nb, tt, LANES), 2)
    pos1 = row + (pos0 + 1) + t_idx * tt
    low = lane < POOL_GROUP
    pooled = []
    for tile, (w_small, w_big) in enumerate(((2, 4), (8, 16))):
        s_small = window_sum(tile * LANES, 0, w_small)
        s_big = s_small + window_sum(tile * LANES, w_small, w_big)
        total = jnp.where(low, s_small, s_big)
        cnt = jnp.minimum(pos1, jnp.where(low, w_small, w_big)).astype(F32)
        pooled.append(total / cnt - c[:, :, tile * LANES:(tile + 1) * LANES])
    pooled = jnp.concatenate(pooled, axis=-1).reshape(m, D_C)
    pooled = _dot(pooled.astype(BF16), wpool_ref[...]) * pscale_ref[...]

    k_new = _dot(xb, win_ref[:, k_off:k_off + D_KV]).reshape(nb, tt, D_KV)
    v_new = _dot(xb, win_ref[:, v_off:v_off + D_KV]).reshape(nb, tt, D_KV)
    kb_scr[:, 0:WINDOW, :] = nk_ref[...].astype(BF16)
    vb_scr[:, 0:WINDOW, :] = nv_ref[...].astype(BF16)
    kb_scr[:, WINDOW:WINDOW + tt, :] = k_new.astype(BF16)
    vb_scr[:, WINDOW:WINDOW + tt, :] = v_new.astype(BF16)
    if tt >= WINDOW:
        nk_ref[...] = k_new[:, tt - WINDOW:, :]
        nv_ref[...] = v_new[:, tt - WINDOW:, :]
    else:
        nk_ref[...] = jnp.concatenate([nk_ref[:, tt:, :], k_new], axis=1)
        nv_ref[...] = jnp.concatenate([nv_ref[:, tt:, :], v_new], axis=1)
    q = _dot(xb, win_ref[:, q_off:q_off + D_ATT]).reshape(nb, tt, D_ATT)

    q_lane = lax.broadcasted_iota(jnp.int32, (cq, LANES), 1)
    key_row = lax.broadcasted_iota(jnp.int32, (1, KEY_PAD), 1)
    zero_keys = jnp.zeros((KEY_PAD - kw, LANES), BF16)
    n_chunks = tt // cq
    blocks = [(n, ci, p) for n in range(nb) for ci in range(n_chunks) for p in range(N_PAIRS)]

    def scores(blk):
        n, ci, p = blocks[blk]
        r0 = ci * cq
        tiles = [q[n, r0:r0 + cq, (p * GQA + g) * LANES:(p * GQA + g + 1) * LANES]
                 for g in range(GQA)]
        lhs = ([jnp.where(q_lane < HEAD_DIM, tl, 0.0) for tl in tiles]
               + [jnp.where(q_lane >= HEAD_DIM, tl, 0.0) for tl in tiles])
        lhs = jnp.concatenate(lhs, axis=0).astype(BF16)
        kt = jnp.concatenate([kb_scr[n, r0:r0 + kw, p * LANES:(p + 1) * LANES], zero_keys], axis=0)
        s = lax.dot_general(lhs, kt, (((1,), (1,)), ((), ())), preferred_element_type=F32)
        s = s + bias_ref[p]
        if masked_history and r0 < WINDOW:
            first_valid = (WINDOW - r0) - t_idx * tt
            s = s + jnp.where(key_row >= first_valid, 0.0, NEG_INF)
        s_scr[blk] = s

    def softmax(blk):
        s = s_scr[blk]
        e = jnp.exp(s - jnp.max(s, axis=-1, keepdims=True))
        inv = 1.0 / jnp.sum(e, axis=-1, keepdims=True)
        p_scr[blk] = (e * inv).astype(BF16)

    def values(blk):
        n, ci, p = blocks[blk]
        r0 = ci * cq
        vt = jnp.concatenate([vb_scr[n, r0:r0 + kw, p * LANES:(p + 1) * LANES], zero_keys], axis=0)
        pv = _dot(p_scr[blk], vt)
        for g in range(GQA):
            o = jnp.where(q_lane < HEAD_DIM, pv[g * cq:(g + 1) * cq, :],
                          pv[(GQA + g) * cq:(GQA + g + 1) * cq, :])
            col = (p * GQA + g) * LANES
            att_scr[n * tt + r0:n * tt + r0 + cq, col:col + LANES] = o.astype(BF16)

    for step in range(len(blocks) + 2):
        if step < len(blocks):
            scores(step)
        if 0 <= step - 1 < len(blocks):
            softmax(step - 1)
        if 0 <= step - 2 < len(blocks):
            values(step - 2)

    mix = (_dot(pooled.astype(BF16), wout_ref[0:D_C, :])
           + _dot(att_scr[...], wout_ref[D_C:D_C + D_ATT, :]))
    y = ALPHA * x + mix
    y_ref[...] = _layer_norm(y, lg_ref[...], lb_ref[...]).reshape(nb, tt, D_MODEL)


def _odd_call(x, hc, ck, cv, win, wpool, pscale, bias, wout, lg, lb, *,
              nb, tt, cq, pos0, masked_history):
    b, s, _ = x.shape
    grid = (b // nb, s // tt)
    rows = PAIR_ROWS * cq
    n_blocks = nb * (tt // cq) * N_PAIRS
    const = lambda *shape: pl.BlockSpec(shape, lambda i, t: (0,) * len(shape))
    return pl.pallas_call(
        functools.partial(_odd_kernel, nb=nb, tt=tt, cq=cq, pos0=pos0,
                          masked_history=masked_history),
        grid=grid,
        in_specs=[
            pl.BlockSpec((nb, tt, D_MODEL), lambda i, t: (i, t, 0)),
            pl.BlockSpec((nb, HIST_C_PAD, D_C), lambda i, t: (i, 0, 0)),
            pl.BlockSpec((nb, WINDOW, D_KV), lambda i, t: (i, 0, 0)),
            pl.BlockSpec((nb, WINDOW, D_KV), lambda i, t: (i, 0, 0)),
            const(D_MODEL, D_C + D_ATT + 2 * D_KV),
            const(D_C, D_C),
            const(1, D_C),
            const(N_PAIRS, rows, KEY_PAD),
            const(D_C + D_ATT, D_MODEL),
            const(1, D_MODEL),
            const(1, D_MODEL),
        ],
        out_specs=[
            pl.BlockSpec((nb, tt, D_MODEL), lambda i, t: (i, t, 0)),
            pl.BlockSpec((nb, HIST_C_PAD, D_C), lambda i, t: (i, 0, 0)),
            pl.BlockSpec((nb, WINDOW, D_KV), lambda i, t: (i, 0, 0)),
            pl.BlockSpec((nb, WINDOW, D_KV), lambda i, t: (i, 0, 0)),
        ],
        out_shape=[
            jax.ShapeDtypeStruct((b, s, D_MODEL), F32),
            jax.ShapeDtypeStruct((b, HIST_C_PAD, D_C), F32),
            jax.ShapeDtypeStruct((b, WINDOW, D_KV), F32),
            jax.ShapeDtypeStruct((b, WINDOW, D_KV), F32),
        ],
        scratch_shapes=[
            pltpu.VMEM((nb, HIST_C_PAD + tt, D_C), F32),
            pltpu.VMEM((nb, WINDOW + tt, D_KV), BF16),
            pltpu.VMEM((nb, WINDOW + tt, D_KV), BF16),
            pltpu.VMEM((n_blocks, rows, KEY_PAD), F32),
            pltpu.VMEM((n_blocks, rows, KEY_PAD), BF16),
            pltpu.VMEM((nb * tt, D_ATT), BF16),
        ],
        compiler_params=pltpu.CompilerParams(
            dimension_semantics=("parallel", "arbitrary"), vmem_limit_bytes=VMEM_LIMIT),
    )(x, hc, ck, cv, win, wpool, pscale, bias, wout, lg, lb)


def _bias_kernel(bucket_ref, rb_ref, out_ref):
    bucket = bucket_ref[...]
    for p in range(N_PAIRS):
        for r in range(PAIR_ROWS):
            head = (2 * p + r // GQA) * GQA + r % GQA
            acc = jnp.zeros((CHUNK, WINDOW + CHUNK), F32)
            for b in range(NUM_BUCKETS):
                acc = jnp.where(bucket == b, rb_ref[b, head], acc)
            out_ref[p, r * CHUNK:(r + 1) * CHUNK, :] = acc


def _bias_call(bucket, rel_bias):
    return pl.pallas_call(
        _bias_kernel,
        in_specs=[
            pl.BlockSpec(memory_space=pltpu.VMEM),
            pl.BlockSpec(memory_space=pltpu.SMEM),
        ],
        out_specs=pl.BlockSpec(memory_space=pltpu.VMEM),
        out_shape=jax.ShapeDtypeStruct((N_PAIRS, PAIR_ROWS * CHUNK, WINDOW + CHUNK), F32),
    )(bucket, rel_bias)


def _t5_bucket(rel):
    nb = NUM_BUCKETS // 2
    max_exact = nb // 2
    ret = jnp.where(rel > 0, nb, 0)
    n = jnp.abs(rel)
    nf = jnp.maximum(n, 1).astype(jnp.float32)
    large = max_exact + (jnp.log(nf / max_exact) / math.log(MAX_DISTANCE / max_exact)
                         * (nb - max_exact)).astype(jnp.int32)
    large = jnp.minimum(large, nb - 1)
    return ret + jnp.where(n < max_exact, n, large)


def _head_order():
    order = []
    for p in range(N_PAIRS):
        for g in range(GQA):
            order += [(2 * p) * GQA + g, (2 * p + 1) * GQA + g]
    return order


def _pad_rows(h, rows):
    pad = rows - h.shape[-2]
    cfg = [(0, 0)] * (h.ndim - 2) + [(pad, 0), (0, 0)]
    return jnp.pad(h, cfg)


def _prepare(w):
    order = _head_order()
    cols = jnp.asarray([h * HEAD_DIM + d for h in order for d in range(HEAD_DIM)], jnp.int32)
    p = {}
    p['w_in_ab'] = w['w_in_ab'].astype(BF16)
    p['w_out_ab'] = w['w_out_ab'].astype(BF16)
    win = w['w_in_cd']
    q_cols = jnp.take(win[:, :, D_C:D_C + D_ATT], cols, axis=2) * ATT_SCALE
    p['w_in_cd'] = jnp.concatenate([win[:, :, :D_C], q_cols, win[:, :, D_C + D_ATT:]], axis=2).astype(BF16)
    wout = w['w_out_cd']
    att_rows = jnp.take(wout[:, D_C:, :], cols, axis=1)
    p['w_out_cd'] = jnp.concatenate([wout[:, :D_C, :], att_rows], axis=1).astype(BF16)
    n_odd = win.shape[0]
    wp = jnp.zeros((n_odd, D_C, D_C), F32)
    for g in range(4):
        sl = slice(g * POOL_GROUP, (g + 1) * POOL_GROUP)
        wp = wp.at[:, sl, sl].set(w['w_pool'][:, g])
    p['w_pool'] = wp.astype(BF16)
    up = w['w_ffn_up'].astype(BF16)
    depth = up.shape[0]
    split = lambda a: a.reshape(depth, D_MODEL, N_FF_CHUNKS, FF_CHUNK).transpose(0, 2, 1, 3)
    p['w_up_g'] = split(up[:, :, :D_FF])
    p['w_up_v'] = split(up[:, :, D_FF:])
    p['ffn_conv_w'] = w['ffn_conv_w'].reshape(depth, CONV_F, N_FF_CHUNKS, FF_CHUNK).transpose(0, 2, 1, 3)
    p['ffn_conv_b'] = w['ffn_conv_b'].reshape(depth, N_FF_CHUNKS, 1, FF_CHUNK)
    p['w_down'] = w['w_ffn_down'].astype(BF16).reshape(depth, N_FF_CHUNKS, FF_CHUNK, D_MODEL)
    heads = jnp.asarray([[(2 * pp + r // GQA) * GQA + r % GQA for r in range(PAIR_ROWS)]
                         for pp in range(N_PAIRS)], jnp.int32)
    p['sinks'] = jnp.take(w['attn_sinks'], heads, axis=1)
    caw = w['conv_a_w']
    p['conv_a_w'] = jnp.broadcast_to(caw[:, :, None, :], caw.shape[:2] + (SUBLANES, D_A))
    for name in ('conv_a_b', 'ln_a_g', 'ln_a_b', 'conv_b_w', 'pool_scale',
                 'ln_mix_g', 'ln_mix_b', 'ln_ffn_g', 'ln_ffn_b'):
        p[name] = w[name]
    return p


def _trunk(x, pos0, hist_a, hist_b, hist_c, cache_k, cache_v, hist_f, p, bias, *,
           nb, tt, cq, masked_history):
    b = x.shape[0]
    row = lambda v: v.reshape(1, -1)
    bias_t = bias.reshape(N_PAIRS, PAIR_ROWS, CHUNK, WINDOW + CHUNK)[:, :, :cq, :WINDOW + cq]
    bias_t = bias_t.reshape(N_PAIRS, PAIR_ROWS * cq, WINDOW + cq)
    st_a, st_b, st_c, st_k, st_v, st_f = [], [], [], [], [], []
    for layer in range(DEPTH):
        i = layer // 2
        if layer % 2 == 0:
            x, sa, sb = _even_call(
                x, _pad_rows(hist_a[i], HIST_A_PAD), _pad_rows(hist_b[i], HIST_B_PAD),
                p['w_in_ab'][i], p['conv_a_w'][i], row(p['conv_a_b'][i]), row(p['ln_a_g'][i]),
                row(p['ln_a_b'][i]), p['conv_b_w'][i], p['w_out_ab'][i],
                row(p['ln_mix_g'][layer]), row(p['ln_mix_b'][layer]), nb=nb, tt=tt)
            st_a.append(sa[:, HIST_A_PAD - (CONV_A - 1):])
            st_b.append(sb[:, HIST_B_PAD - (CONV_B - 1):])
        else:
            sink = jnp.repeat(p['sinks'][i], cq, axis=1)[..., None]
            pad = jnp.full(sink.shape[:2] + (KEY_PAD - WINDOW - cq - 1,), NEG_INF, F32)
            bias_l = jnp.concatenate([bias_t, sink, pad], axis=-1)
            x, sc, sk, sv = _odd_call(
                x, _pad_rows(hist_c[i], HIST_C_PAD),
                cache_k[i].reshape(b, WINDOW, D_KV), cache_v[i].reshape(b, WINDOW, D_KV),
                p['w_in_cd'][i], p['w_pool'][i], row(p['pool_scale'][i]), bias_l,
                p['w_out_cd'][i], row(p['ln_mix_g'][layer]), row(p['ln_mix_b'][layer]),
                nb=nb, tt=tt, cq=cq, pos0=pos0, masked_history=masked_history)
            st_c.append(sc[:, HIST_C_PAD - POOL_HIST:])
            st_k.append(sk.reshape(b, WINDOW, N_KV_HEADS, HEAD_DIM))
            st_v.append(sv.reshape(b, WINDOW, N_KV_HEADS, HEAD_DIM))
        hf = _pad_rows(hist_f[layer], HIST_F_PAD)
        hf = hf.reshape(b, HIST_F_PAD, N_FF_CHUNKS, FF_CHUNK).transpose(2, 0, 1, 3)
        x, sf = _ffn_call(
            x, hf, p['w_up_g'][layer], p['w_up_v'][layer], p['ffn_conv_w'][layer],
            p['ffn_conv_b'][layer], p['w_down'][layer],
            row(p['ln_ffn_g'][layer]), row(p['ln_ffn_b'][layer]), nb=nb, tt=tt)
        sf = sf[:, :, HIST_F_PAD - (CONV_F - 1):, :].transpose(1, 2, 0, 3)
        st_f.append(sf.reshape(b, CONV_F - 1, D_FF))
    return (x, jnp.stack(st_a), jnp.stack(st_b), jnp.stack(st_c), jnp.stack(st_k),
            jnp.stack(st_v), jnp.stack(st_f))


def _tile_rows(s):
    for cand in (256, 128, 64):
        if s % cand == 0:
            return cand
    return s


def kernel(x_prompt, x_sample, state_conv_a, state_conv_b, state_pool_c, cache_k_d, cache_v_d, state_ffn_conv, w_in_ab, conv_a_w, conv_a_b, ln_a_g, ln_a_b, conv_b_w, w_out_ab, w_in_cd, w_pool, pool_scale, attn_sinks, rel_bias, w_out_cd, w_ffn_up, ffn_conv_w, ffn_conv_b, w_ffn_down, ln_mix_g, ln_mix_b, ln_ffn_g, ln_ffn_b):
    w = dict(w_in_ab=w_in_ab, conv_a_w=conv_a_w, conv_a_b=conv_a_b, ln_a_g=ln_a_g, ln_a_b=ln_a_b,
             conv_b_w=conv_b_w, w_out_ab=w_out_ab, w_in_cd=w_in_cd, w_pool=w_pool, pool_scale=pool_scale,
             attn_sinks=attn_sinks, w_out_cd=w_out_cd, w_ffn_up=w_ffn_up,
             ffn_conv_w=ffn_conv_w, ffn_conv_b=ffn_conv_b, w_ffn_down=w_ffn_down,
             ln_mix_g=ln_mix_g, ln_mix_b=ln_mix_b, ln_ffn_g=ln_ffn_g, ln_ffn_b=ln_ffn_b)
    p = _prepare(w)
    n_even, n_odd, depth = w_in_ab.shape[0], w_in_cd.shape[0], w_ffn_up.shape[0]
    rel = (jnp.arange(WINDOW + CHUNK)[None, :] - WINDOW - jnp.arange(CHUNK)[:, None]).astype(jnp.int32)
    bias = _bias_call(_t5_bucket(rel).astype(jnp.int32), rel_bias)

    b, s = x_prompt.shape[:2]
    zeros = lambda *shape: jnp.zeros(shape, F32)
    prompt = _trunk(
        x_prompt, 0, zeros(n_even, b, CONV_A - 1, D_A), zeros(n_even, b, CONV_B - 1, D_B),
        zeros(n_odd, b, POOL_HIST, D_C), zeros(n_odd, b, WINDOW, N_KV_HEADS, HEAD_DIM),
        zeros(n_odd, b, WINDOW, N_KV_HEADS, HEAD_DIM), zeros(depth, b, CONV_F - 1, D_FF),
        p, bias, nb=1, tt=_tile_rows(s), cq=CHUNK, masked_history=True)
    bs, ss = x_sample.shape[:2]
    sample = _trunk(
        x_sample, PAST_LEN, state_conv_a, state_conv_b, state_pool_c, cache_k_d, cache_v_d,
        state_ffn_conv, p, bias, nb=bs, tt=ss, cq=ss, masked_history=False)
    return (prompt[0], sample[0]) + tuple(prompt[1:]) + tuple(sample[1:])
```

```python
import functools
import math

import jax
import jax.numpy as jnp
from jax import lax
from jax.experimental import pallas as pl
from jax.experimental.pallas import tpu as pltpu

F32 = jnp.float32
BF16 = jnp.bfloat16

D_MODEL = 1024
DEPTH = 4
PAST_LEN = 4096
CHUNK = 64
D_A = D_MODEL // 2
CONV_A = 31
D_B = D_MODEL // 2
CONV_B = 3
POOL_WINDOWS = (2, 4, 8, 16)
D_C = D_MODEL // 4
POOL_GROUP = D_C // 4
POOL_HIST = 15
HEAD_DIM = 64
N_Q_HEADS = (D_MODEL - D_C) // HEAD_DIM
N_KV_HEADS = 4
GQA = N_Q_HEADS // N_KV_HEADS
D_ATT = N_Q_HEADS * HEAD_DIM
D_KV = N_KV_HEADS * HEAD_DIM
WINDOW = 128
NUM_BUCKETS = 32
MAX_DISTANCE = 128
ATT_SCALE = HEAD_DIM ** -0.5
NEG_INF = -1e30
D_FF = 2816
CONV_F = 3
LN_EPS = 1e-5
ALPHA = (2 * DEPTH) ** 0.25

LANES = 128
SUBLANES = 8
HIST_A_PAD = 32
HIST_B_PAD = SUBLANES
HIST_C_PAD = 16
HIST_F_PAD = SUBLANES
FF_CHUNK = 256
N_FF_CHUNKS = D_FF // FF_CHUNK
N_PAIRS = N_KV_HEADS // 2
PAIR_ROWS = 2 * GQA
CONV_ROW_BLOCK = 32
KEY_PAD = 256
GELU_C0 = math.sqrt(2.0 / math.pi)
GELU_C1 = GELU_C0 * 0.044715
VMEM_LIMIT = 56 * 1024 * 1024
MIXER_TILE_ROWS = 256
FFN_TILE_ROWS = 512


def _layer_norm(v, g, b):
    mu = jnp.mean(v, axis=-1, keepdims=True)
    d = v - mu
    var = jnp.mean(d * d, axis=-1, keepdims=True)
    return d * lax.rsqrt(var + LN_EPS) * g + b


def _dot(a, b):
    return jnp.dot(a, b, preferred_element_type=F32)


def _ffn_kernel(x_ref, hist_ref, wg_ref, wv_ref, cw_ref, cb_ref, wd_ref, lg_ref, lb_ref,
                y_ref, st_ref, xb_scr, g_scr0, g_scr1, v_scr0, v_scr1, acc_scr, *, nb, tt):
    m = nb * tt

    @pl.when(pl.program_id(1) == 0)
    def _():
        st_ref[...] = hist_ref[...]

    x = x_ref[...].reshape(m, D_MODEL)
    xb_scr[...] = x.astype(BF16)
    n_slabs = tt // SUBLANES
    sub = lax.broadcasted_iota(jnp.int32, (1, 1, SUBLANES, FF_CHUNK), 2)

    def up(j, g_scr, v_scr):
        xb = xb_scr[...]
        g_scr[...] = _dot(xb, wg_ref[j]).reshape(nb, tt, FF_CHUNK)
        v_scr[...] = _dot(xb, wv_ref[j]).reshape(nb, tt, FF_CHUNK)

    def down(j, g_scr, v_scr, first=False):
        g = g_scr[...].reshape(nb, n_slabs, SUBLANES, FF_CHUNK)
        hist = st_ref[j].reshape(nb, 1, SUBLANES, FF_CHUNK)
        st_ref[j] = g_scr[:, tt - HIST_F_PAD:tt, :]
        cw = cw_ref[j]
        conv = cw[CONV_F - 1:CONV_F, :] * g + cb_ref[j]
        for d in range(1, CONV_F):
            rot = pltpu.roll(g, d, 2)
            before = jnp.concatenate([pltpu.roll(hist, d, 2), rot[:, :n_slabs - 1]], axis=1)
            conv = conv + cw[CONV_F - 1 - d:CONV_F - d, :] * jnp.where(sub >= d, rot, before)
        conv = conv.reshape(nb, tt, FF_CHUNK)
        t = jnp.tanh(conv * (GELU_C0 + GELU_C1 * (conv * conv)))
        h = (conv * v_scr[...]) * (1.0 + t)
        hb = h.reshape(m, FF_CHUNK).astype(BF16)
        if first:
            acc_scr[...] = _dot(hb, wd_ref[j])
        else:
            acc_scr[...] += _dot(hb, wd_ref[j])

    up(0, g_scr0, v_scr0)
    up(1, g_scr1, v_scr1)
    down(0, g_scr0, v_scr0, first=True)

    def body(i, carry):
        j = 2 * i + 1
        up(j + 1, g_scr0, v_scr0)
        down(j, g_scr1, v_scr1)
        up(j + 2, g_scr1, v_scr1)
        down(j + 1, g_scr0, v_scr0)
        return carry

    lax.fori_loop(0, (N_FF_CHUNKS - 3) // 2, body, 0)
    up(N_FF_CHUNKS - 1, g_scr0, v_scr0)
    down(N_FF_CHUNKS - 2, g_scr1, v_scr1)
    down(N_FF_CHUNKS - 1, g_scr0, v_scr0)
    y = ALPHA * x + acc_scr[...]
    y_ref[...] = _layer_norm(y, lg_ref[...], lb_ref[...]).reshape(nb, tt, D_MODEL)


def _ffn_call(x, hist, wg, wv, cw, cb, wd, lg, lb, *, nb, tt):
    b, s, _ = x.shape
    grid = (b // nb, s // tt)
    const = lambda *shape: pl.BlockSpec(shape, lambda i, t: (0,) * len(shape),
                                        pipeline_mode=pl.Buffered(1))
    return pl.pallas_call(
        functools.partial(_ffn_kernel, nb=nb, tt=tt),
        grid=grid,
        in_specs=[
            pl.BlockSpec((nb, tt, D_MODEL), lambda i, t: (i, t, 0)),
            pl.BlockSpec((N_FF_CHUNKS, nb, HIST_F_PAD, FF_CHUNK), lambda i, t: (0, i, 0, 0)),
            const(N_FF_CHUNKS, D_MODEL, FF_CHUNK),
            const(N_FF_CHUNKS, D_MODEL, FF_CHUNK),
            const(N_FF_CHUNKS, CONV_F, FF_CHUNK),
            const(N_FF_CHUNKS, 1, FF_CHUNK),
            const(N_FF_CHUNKS, FF_CHUNK, D_MODEL),
            const(1, D_MODEL),
            const(1, D_MODEL),
        ],
        out_specs=[
            pl.BlockSpec((nb, tt, D_MODEL), lambda i, t: (i, t, 0)),
            pl.BlockSpec((N_FF_CHUNKS, nb, HIST_F_PAD, FF_CHUNK), lambda i, t: (0, i, 0, 0)),
        ],
        out_shape=[
            jax.ShapeDtypeStruct((b, s, D_MODEL), F32),
            jax.ShapeDtypeStruct((N_FF_CHUNKS, b, HIST_F_PAD, FF_CHUNK), F32),
        ],
        scratch_shapes=[
            pltpu.VMEM((nb * tt, D_MODEL), BF16),
            pltpu.VMEM((nb, tt, FF_CHUNK), F32),
            pltpu.VMEM((nb, tt, FF_CHUNK), F32),
            pltpu.VMEM((nb, tt, FF_CHUNK), F32),
            pltpu.VMEM((nb, tt, FF_CHUNK), F32),
            pltpu.VMEM((nb * tt, D_MODEL), F32),
        ],
        compiler_params=pltpu.CompilerParams(
            dimension_semantics=("parallel", "arbitrary"), vmem_limit_bytes=VMEM_LIMIT),
    )(x, hist, wg, wv, cw, cb, wd, lg, lb)


def _even_kernel(x_ref, ha_ref, hb_ref, win_ref, caw_ref, cab_ref, lag_ref, lab_ref, cbw_ref,
                 wout_ref, lg_ref, lb_ref, y_ref, sta_ref, stb_ref, u_scr, sh_scr, a_scr, *, nb, tt):
    m = nb * tt

    @pl.when(pl.program_id(1) == 0)
    def _():
        sta_ref[...] = ha_ref[...]
        stb_ref[...] = hb_ref[...]

    x = x_ref[...].reshape(m, D_MODEL)
    xb = x.astype(BF16)

    def proj(i):
        return _dot(xb, win_ref[:, i * D_A:(i + 1) * D_A])

    u = proj(0) * jax.nn.sigmoid(proj(1))
    u_scr[:, 0:HIST_A_PAD, :] = sta_ref[...]
    u_scr[:, HIST_A_PAD:HIST_A_PAD + tt, :] = u.reshape(nb, tt, D_A)
    sta_ref[...] = u_scr[:, tt:tt + HIST_A_PAD, :]
    n_slabs = (HIST_A_PAD + tt) // SUBLANES
    sub = lax.broadcasted_iota(jnp.int32, (SUBLANES, D_A), 0)
    for n in range(nb):
        slabs = [u_scr[n, SUBLANES * i:SUBLANES * (i + 1), :] for i in range(n_slabs)]
        for r in range(1, SUBLANES):
            rot = [pltpu.roll(s, SUBLANES - r, 0) for s in slabs]
            for i in range(n_slabs - 1):
                sh_scr[r - 1, n, SUBLANES * i:SUBLANES * (i + 1), :] = jnp.where(
                    sub < SUBLANES - r, rot[i], rot[i + 1])

    rb = min(CONV_ROW_BLOCK, tt)
    first = HIST_A_PAD - (CONV_A - 1)
    cab = cab_ref[...]
    lag = lag_ref[...]
    lab = lab_ref[...]
    for n in range(nb):
        for r0 in range(0, tt, rb):
            acc = jnp.broadcast_to(cab[None], (rb // SUBLANES, SUBLANES, D_A))
            for k in range(CONV_A):
                tiles, r = divmod(first + k, SUBLANES)
                lo = r0 + SUBLANES * tiles
                win = u_scr[n, lo:lo + rb, :] if r == 0 else sh_scr[r - 1, n, lo:lo + rb, :]
                acc = acc + caw_ref[k][None] * win.reshape(rb // SUBLANES, SUBLANES, D_A)
            a = jax.nn.silu(_layer_norm(acc.reshape(rb, D_A), lag, lab))
            a_scr[n * tt + r0:n * tt + r0 + rb, :] = a.astype(BF16)

    v = (proj(3) * proj(4)).reshape(nb, tt, D_B)
    u_scr[:, 0:HIST_B_PAD, :] = stb_ref[...]
    u_scr[:, HIST_B_PAD:HIST_B_PAD + tt, :] = v
    stb_ref[...] = u_scr[:, tt:tt + HIST_B_PAD, :]
    cbw = cbw_ref[...]
    conv_b = (cbw[0:1, :][None] * u_scr[:, HIST_B_PAD - 2:HIST_B_PAD - 2 + tt, :]
              + cbw[1:2, :][None] * u_scr[:, HIST_B_PAD - 1:HIST_B_PAD - 1 + tt, :]
              + cbw[2:3, :][None] * v)
    bo = proj(2) * conv_b.reshape(m, D_B)

    mix = _dot(a_scr[...], wout_ref[0:D_A, :]) + _dot(bo.astype(BF16), wout_ref[D_A:D_A + D_B, :])
    y = ALPHA * x + mix
    y_ref[...] = _layer_norm(y, lg_ref[...], lb_ref[...]).reshape(nb, tt, D_MODEL)


def _even_call(x, ha, hb, win, caw, cab, lag, lab, cbw, wout, lg, lb, *, nb, tt):
    b, s, _ = x.shape
    grid = (b // nb, s // tt)
    const = lambda *shape: pl.BlockSpec(shape, lambda i, t: (0,) * len(shape))
    return pl.pallas_call(
        functools.partial(_even_kernel, nb=nb, tt=tt),
        grid=grid,
        in_specs=[
            pl.BlockSpec((nb, tt, D_MODEL), lambda i, t: (i, t, 0)),
            pl.BlockSpec((nb, HIST_A_PAD, D_A), lambda i, t: (i, 0, 0)),
            pl.BlockSpec((nb, HIST_B_PAD, D_B), lambda i, t: (i, 0, 0)),
            const(D_MODEL, 2 * D_A + 3 * D_B),
            const(CONV_A, SUBLANES, D_A),
            const(1, D_A),
            const(1, D_A),
            const(1, D_A),
            const(CONV_B, D_B),
            const(D_A + D_B, D_MODEL),
            const(1, D_MODEL),
            const(1, D_MODEL),
        ],
        out_specs=[
            pl.BlockSpec((nb, tt, D_MODEL), lambda i, t: (i, t, 0)),
            pl.BlockSpec((nb, HIST_A_PAD, D_A), lambda i, t: (i, 0, 0)),
            pl.BlockSpec((nb, HIST_B_PAD, D_B), lambda i, t: (i, 0, 0)),
        ],
        out_shape=[
            jax.ShapeDtypeStruct((b, s, D_MODEL), F32),
            jax.ShapeDtypeStruct((b, HIST_A_PAD, D_A), F32),
            jax.ShapeDtypeStruct((b, HIST_B_PAD, D_B), F32),
        ],
        scratch_shapes=[
            pltpu.VMEM((nb, HIST_A_PAD + tt, D_A), F32),
            pltpu.VMEM((SUBLANES - 1, nb, HIST_A_PAD + tt, D_A), F32),
            pltpu.VMEM((nb * tt, D_A), BF16),
        ],
        compiler_params=pltpu.CompilerParams(
            dimension_semantics=("parallel", "arbitrary"), vmem_limit_bytes=VMEM_LIMIT),
    )(x, ha, hb, win, caw, cab, lag, lab, cbw, wout, lg, lb)


def _odd_kernel(x_ref, hc_ref, ck_ref, cv_ref, win_ref, wpool_ref, pscale_ref, bias_ref,
                wout_ref, lg_ref, lb_ref, y_ref, stc_ref, nk_ref, nv_ref,
                c_scr, kb_scr, vb_scr, s_scr, p_scr, att_scr, *, nb, tt, cq, pos0, masked_history):
    m = nb * tt
    kw = WINDOW + cq
    t_idx = pl.program_id(1)

    @pl.when(t_idx == 0)
    def _():
        stc_ref[...] = hc_ref[...]
        nk_ref[...] = ck_ref[...]
        nv_ref[...] = cv_ref[...]

    x = x_ref[...].reshape(m, D_MODEL)
    xb = x.astype(BF16)
    q_off = D_C
    k_off = D_C + D_ATT
    v_off = k_off + D_KV

    c = _dot(xb, win_ref[:, 0:D_C]).reshape(nb, tt, D_C)
    c_scr[:, 0:HIST_C_PAD, :] = stc_ref[...]
    c_scr[:, HIST_C_PAD:HIST_C_PAD + tt, :] = c
    stc_ref[...] = c_scr[:, tt:tt + HIST_C_PAD, :]

    def window_sum(lane0, lo, hi):
        acc = None
        for j in range(lo, hi):
            r = c_scr[:, HIST_C_PAD - j:HIST_C_PAD - j + tt, lane0:lane0 + LANES]
            acc = r if acc is None else acc + r
        return acc

    row = lax.broadcasted_iota(jnp.int32, (nb, tt, LANES), 1)
    lane = lax.broadcasted_iota(jnp.int32, (nb, tt, LANES), 2)
    pos1 = row + (pos0 + 1) + t_idx * tt
    low = lane < POOL_GROUP
    pooled = []
    for tile, (w_small, w_big) in enumerate(((2, 4), (8, 16))):
        s_small = window_sum(tile * LANES, 0, w_small)
        s_big = s_small + window_sum(tile * LANES, w_small, w_big)
        total = jnp.where(low, s_small, s_big)
        cnt = jnp.minimum(pos1, jnp.where(low, w_small, w_big)).astype(F32)
        pooled.append(total / cnt - c[:, :, tile * LANES:(tile + 1) * LANES])
    pooled = jnp.concatenate(pooled, axis=-1).reshape(m, D_C)
    pooled = _dot(pooled.astype(BF16), wpool_ref[...]) * pscale_ref[...]

    k_new = _dot(xb, win_ref[:, k_off:k_off + D_KV]).reshape(nb, tt, D_KV)
    v_new = _dot(xb, win_ref[:, v_off:v_off + D_KV]).reshape(nb, tt, D_KV)
    kb_scr[:, 0:WINDOW, :] = nk_ref[...].astype(BF16)
    vb_scr[:, 0:WINDOW, :] = nv_ref[...].astype(BF16)
    kb_scr[:, WINDOW:WINDOW + tt, :] = k_new.astype(BF16)
    vb_scr[:, WINDOW:WINDOW + tt, :] = v_new.astype(BF16)
    if tt >= WINDOW:
        nk_ref[...] = k_new[:, tt - WINDOW:, :]
        nv_ref[...] = v_new[:, tt - WINDOW:, :]
    else:
        nk_ref[...] = jnp.concatenate([nk_ref[:, tt:, :], k_new], axis=1)
        nv_ref[...] = jnp.concatenate([nv_ref[:, tt:, :], v_new], axis=1)
    q = _dot(xb, win_ref[:, q_off:q_off + D_ATT]).reshape(nb, tt, D_ATT)

    q_lane = lax.broadcasted_iota(jnp.int32, (cq, LANES), 1)
    key_row = lax.broadcasted_iota(jnp.int32, (1, KEY_PAD), 1)
    zero_keys = jnp.zeros((KEY_PAD - kw, LANES), BF16)
    n_chunks = tt // cq
    blocks = [(n, ci, p) for n in range(nb) for ci in range(n_chunks) for p in range(N_PAIRS)]

    def scores(blk):
        n, ci, p = blocks[blk]
        r0 = ci * cq
        tiles = [q[n, r0:r0 + cq, (p * GQA + g) * LANES:(p * GQA + g + 1) * LANES]
                 for g in range(GQA)]
        lhs = ([jnp.where(q_lane < HEAD_DIM, tl, 0.0) for tl in tiles]
               + [jnp.where(q_lane >= HEAD_DIM, tl, 0.0) for tl in tiles])
        lhs = jnp.concatenate(lhs, axis=0).astype(BF16)
        kt = jnp.concatenate([kb_scr[n, r0:r0 + kw, p * LANES:(p + 1) * LANES], zero_keys], axis=0)
        s = lax.dot_general(lhs, kt, (((1,), (1,)), ((), ())), preferred_element_type=F32)
        s = s + bias_ref[p]
        if masked_history and r0 < WINDOW:
            first_valid = (WINDOW - r0) - t_idx * tt
            s = s + jnp.where(key_row >= first_valid, 0.0, NEG_INF)
        s_scr[blk] = s

    def softmax(blk):
        s = s_scr[blk]
        e = jnp.exp(s - jnp.max(s, axis=-1, keepdims=True))
        inv = 1.0 / jnp.sum(e, axis=-1, keepdims=True)
        p_scr[blk] = (e * inv).astype(BF16)

    def values(blk):
        n, ci, p = blocks[blk]
        r0 = ci * cq
        vt = jnp.concatenate([vb_scr[n, r0:r0 + kw, p * LANES:(p + 1) * LANES], zero_keys], axis=0)
        pv = _dot(p_scr[blk], vt)
        for g in range(GQA):
            o = jnp.where(q_lane < HEAD_DIM, pv[g * cq:(g + 1) * cq, :],
                          pv[(GQA + g) * cq:(GQA + g + 1) * cq, :])
            col = (p * GQA + g) * LANES
            att_scr[n * tt + r0:n * tt + r0 + cq, col:col + LANES] = o.astype(BF16)

    for step in range(len(blocks) + 2):
        if step < len(blocks):
            scores(step)
        if 0 <= step - 1 < len(blocks):
            softmax(step - 1)
        if 0 <= step - 2 < len(blocks):
            values(step - 2)

    mix = (_dot(pooled.astype(BF16), wout_ref[0:D_C, :])
           + _dot(att_scr[...], wout_ref[D_C:D_C + D_ATT, :]))
    y = ALPHA * x + mix
    y_ref[...] = _layer_norm(y, lg_ref[...], lb_ref[...]).reshape(nb, tt, D_MODEL)


def _odd_call(x, hc, ck, cv, win, wpool, pscale, bias, wout, lg, lb, *,
              nb, tt, cq, pos0, masked_history):
    b, s, _ = x.shape
    grid = (b // nb, s // tt)
    rows = PAIR_ROWS * cq
    n_blocks = nb * (tt // cq) * N_PAIRS
    const = lambda *shape: pl.BlockSpec(shape, lambda i, t: (0,) * len(shape))
    return pl.pallas_call(
        functools.partial(_odd_kernel, nb=nb, tt=tt, cq=cq, pos0=pos0,
                          masked_history=masked_history),
        grid=grid,
        in_specs=[
            pl.BlockSpec((nb, tt, D_MODEL), lambda i, t: (i, t, 0)),
            pl.BlockSpec((nb, HIST_C_PAD, D_C), lambda i, t: (i, 0, 0)),
            pl.BlockSpec((nb, WINDOW, D_KV), lambda i, t: (i, 0, 0)),
            pl.BlockSpec((nb, WINDOW, D_KV), lambda i, t: (i, 0, 0)),
            const(D_MODEL, D_C + D_ATT + 2 * D_KV),
            const(D_C, D_C),
            const(1, D_C),
            const(N_PAIRS, rows, KEY_PAD),
            const(D_C + D_ATT, D_MODEL),
            const(1, D_MODEL),
            const(1, D_MODEL),
        ],
        out_specs=[
            pl.BlockSpec((nb, tt, D_MODEL), lambda i, t: (i, t, 0)),
            pl.BlockSpec((nb, HIST_C_PAD, D_C), lambda i, t: (i, 0, 0)),
            pl.BlockSpec((nb, WINDOW, D_KV), lambda i, t: (i, 0, 0)),
            pl.BlockSpec((nb, WINDOW, D_KV), lambda i, t: (i, 0, 0)),
        ],
        out_shape=[
            jax.ShapeDtypeStruct((b, s, D_MODEL), F32),
            jax.ShapeDtypeStruct((b, HIST_C_PAD, D_C), F32),
            jax.ShapeDtypeStruct((b, WINDOW, D_KV), F32),
            jax.ShapeDtypeStruct((b, WINDOW, D_KV), F32),
        ],
        scratch_shapes=[
            pltpu.VMEM((nb, HIST_C_PAD + tt, D_C), F32),
            pltpu.VMEM((nb, WINDOW + tt, D_KV), BF16),
            pltpu.VMEM((nb, WINDOW + tt, D_KV), BF16),
            pltpu.VMEM((n_blocks, rows, KEY_PAD), F32),
            pltpu.VMEM((n_blocks, rows, KEY_PAD), BF16),
            pltpu.VMEM((nb * tt, D_ATT), BF16),
        ],
        compiler_params=pltpu.CompilerParams(
            dimension_semantics=("parallel", "arbitrary"), vmem_limit_bytes=VMEM_LIMIT),
    )(x, hc, ck, cv, win, wpool, pscale, bias, wout, lg, lb)


def _bias_kernel(bucket_ref, rb_ref, out_ref):
    bucket = bucket_ref[...]
    for p in range(N_PAIRS):
        for r in range(PAIR_ROWS):
            head = (2 * p + r // GQA) * GQA + r % GQA
            acc = jnp.zeros((CHUNK, WINDOW + CHUNK), F32)
            for b in range(NUM_BUCKETS):
                acc = jnp.where(bucket == b, rb_ref[b, head], acc)
            out_ref[p, r * CHUNK:(r + 1) * CHUNK, :] = acc


def _bias_call(bucket, rel_bias):
    return pl.pallas_call(
        _bias_kernel,
        in_specs=[
            pl.BlockSpec(memory_space=pltpu.VMEM),
            pl.BlockSpec(memory_space=pltpu.SMEM),
        ],
        out_specs=pl.BlockSpec(memory_space=pltpu.VMEM),
        out_shape=jax.ShapeDtypeStruct((N_PAIRS, PAIR_ROWS * CHUNK, WINDOW + CHUNK), F32),
    )(bucket, rel_bias)


def _t5_bucket(rel):
    nb = NUM_BUCKETS // 2
    max_exact = nb // 2
    ret = jnp.where(rel > 0, nb, 0)
    n = jnp.abs(rel)
    nf = jnp.maximum(n, 1).astype(jnp.float32)
    large = max_exact + (jnp.log(nf / max_exact) / math.log(MAX_DISTANCE / max_exact)
                         * (nb - max_exact)).astype(jnp.int32)
    large = jnp.minimum(large, nb - 1)
    return ret + jnp.where(n < max_exact, n, large)


def _head_order():
    order = []
    for p in range(N_PAIRS):
        for g in range(GQA):
            order += [(2 * p) * GQA + g, (2 * p + 1) * GQA + g]
    return order


def _pad_rows(h, rows):
    pad = rows - h.shape[-2]
    cfg = [(0, 0)] * (h.ndim - 2) + [(pad, 0), (0, 0)]
    return jnp.pad(h, cfg)


def _prepare(w):
    order = _head_order()
    cols = jnp.asarray([h * HEAD_DIM + d for h in order for d in range(HEAD_DIM)], jnp.int32)
    p = {}
    p['w_in_ab'] = w['w_in_ab'].astype(BF16)
    p['w_out_ab'] = w['w_out_ab'].astype(BF16)
    win = w['w_in_cd']
    q_cols = jnp.take(win[:, :, D_C:D_C + D_ATT], cols, axis=2) * ATT_SCALE
    p['w_in_cd'] = jnp.concatenate([win[:, :, :D_C], q_cols, win[:, :, D_C + D_ATT:]], axis=2).astype(BF16)
    wout = w['w_out_cd']
    att_rows = jnp.take(wout[:, D_C:, :], cols, axis=1)
    p['w_out_cd'] = jnp.concatenate([wout[:, :D_C, :], att_rows], axis=1).astype(BF16)
    n_odd = win.shape[0]
    wp = jnp.zeros((n_odd, D_C, D_C), F32)
    for g in range(4):
        sl = slice(g * POOL_GROUP, (g + 1) * POOL_GROUP)
        wp = wp.at[:, sl, sl].set(w['w_pool'][:, g])
    p['w_pool'] = wp.astype(BF16)
    up = w['w_ffn_up'].astype(BF16)
    depth = up.shape[0]
    split = lambda a: a.reshape(depth, D_MODEL, N_FF_CHUNKS, FF_CHUNK).transpose(0, 2, 1, 3)
    p['w_up_g'] = split(up[:, :, :D_FF])
    p['w_up_v'] = split(up[:, :, D_FF:] * 0.5)
    p['ffn_conv_w'] = w['ffn_conv_w'].reshape(depth, CONV_F, N_FF_CHUNKS, FF_CHUNK).transpose(0, 2, 1, 3)
    p['ffn_conv_b'] = w['ffn_conv_b'].reshape(depth, N_FF_CHUNKS, 1, FF_CHUNK)
    p['w_down'] = w['w_ffn_down'].astype(BF16).reshape(depth, N_FF_CHUNKS, FF_CHUNK, D_MODEL)
    heads = jnp.asarray([[(2 * pp + r // GQA) * GQA + r % GQA for r in range(PAIR_ROWS)]
                         for pp in range(N_PAIRS)], jnp.int32)
    p['sinks'] = jnp.take(w['attn_sinks'], heads, axis=1)
    caw = w['conv_a_w']
    p['conv_a_w'] = jnp.broadcast_to(caw[:, :, None, :], caw.shape[:2] + (SUBLANES, D_A))
    for name in ('conv_a_b', 'ln_a_g', 'ln_a_b', 'conv_b_w', 'pool_scale',
                 'ln_mix_g', 'ln_mix_b', 'ln_ffn_g', 'ln_ffn_b'):
        p[name] = w[name]
    return p


def _trunk(x, pos0, hist_a, hist_b, hist_c, cache_k, cache_v, hist_f, p, bias, *,
           nb, tt, tt_ffn, cq, masked_history):
    b = x.shape[0]
    row = lambda v: v.reshape(1, -1)
    bias_t = bias.reshape(N_PAIRS, PAIR_ROWS, CHUNK, WINDOW + CHUNK)[:, :, :cq, :WINDOW + cq]
    bias_t = bias_t.reshape(N_PAIRS, PAIR_ROWS * cq, WINDOW + cq)
    st_a, st_b, st_c, st_k, st_v, st_f = [], [], [], [], [], []
    for layer in range(DEPTH):
        i = layer // 2
        if layer % 2 == 0:
            x, sa, sb = _even_call(
                x, _pad_rows(hist_a[i], HIST_A_PAD), _pad_rows(hist_b[i], HIST_B_PAD),
                p['w_in_ab'][i], p['conv_a_w'][i], row(p['conv_a_b'][i]), row(p['ln_a_g'][i]),
                row(p['ln_a_b'][i]), p['conv_b_w'][i], p['w_out_ab'][i],
                row(p['ln_mix_g'][layer]), row(p['ln_mix_b'][layer]), nb=nb, tt=tt)
            st_a.append(sa[:, HIST_A_PAD - (CONV_A - 1):])
            st_b.append(sb[:, HIST_B_PAD - (CONV_B - 1):])
        else:
            sink = jnp.repeat(p['sinks'][i], cq, axis=1)[..., None]
            pad = jnp.full(sink.shape[:2] + (KEY_PAD - WINDOW - cq - 1,), NEG_INF, F32)
            bias_l = jnp.concatenate([bias_t, sink, pad], axis=-1)
            x, sc, sk, sv = _odd_call(
                x, _pad_rows(hist_c[i], HIST_C_PAD),
                cache_k[i].reshape(b, WINDOW, D_KV), cache_v[i].reshape(b, WINDOW, D_KV),
                p['w_in_cd'][i], p['w_pool'][i], row(p['pool_scale'][i]), bias_l,
                p['w_out_cd'][i], row(p['ln_mix_g'][layer]), row(p['ln_mix_b'][layer]),
                nb=nb, tt=tt, cq=cq, pos0=pos0, masked_history=masked_history)
            st_c.append(sc[:, HIST_C_PAD - POOL_HIST:])
            st_k.append(sk.reshape(b, WINDOW, N_KV_HEADS, HEAD_DIM))
            st_v.append(sv.reshape(b, WINDOW, N_KV_HEADS, HEAD_DIM))
        hf = _pad_rows(hist_f[layer], HIST_F_PAD)
        hf = hf.reshape(b, HIST_F_PAD, N_FF_CHUNKS, FF_CHUNK).transpose(2, 0, 1, 3)
        x, sf = _ffn_call(
            x, hf, p['w_up_g'][layer], p['w_up_v'][layer], p['ffn_conv_w'][layer],
            p['ffn_conv_b'][layer], p['w_down'][layer],
            row(p['ln_ffn_g'][layer]), row(p['ln_ffn_b'][layer]), nb=nb, tt=tt_ffn)
        sf = sf[:, :, HIST_F_PAD - (CONV_F - 1):, :].transpose(1, 2, 0, 3)
        st_f.append(sf.reshape(b, CONV_F - 1, D_FF))
    return (x, jnp.stack(st_a), jnp.stack(st_b), jnp.stack(st_c), jnp.stack(st_k),
            jnp.stack(st_v), jnp.stack(st_f))


def _tile_rows(s, largest):
    for cand in (512, 256, 128, 64):
        if cand <= largest and s % cand == 0:
            return cand
    return s


def kernel(x_prompt, x_sample, state_conv_a, state_conv_b, state_pool_c, cache_k_d, cache_v_d, state_ffn_conv, w_in_ab, conv_a_w, conv_a_b, ln_a_g, ln_a_b, conv_b_w, w_out_ab, w_in_cd, w_pool, pool_scale, attn_sinks, rel_bias, w_out_cd, w_ffn_up, ffn_conv_w, ffn_conv_b, w_ffn_down, ln_mix_g, ln_mix_b, ln_ffn_g, ln_ffn_b):
    w = dict(w_in_ab=w_in_ab, conv_a_w=conv_a_w, conv_a_b=conv_a_b, ln_a_g=ln_a_g, ln_a_b=ln_a_b,
             conv_b_w=conv_b_w, w_out_ab=w_out_ab, w_in_cd=w_in_cd, w_pool=w_pool, pool_scale=pool_scale,
             attn_sinks=attn_sinks, w_out_cd=w_out_cd, w_ffn_up=w_ffn_up,
             ffn_conv_w=ffn_conv_w, ffn_conv_b=ffn_conv_b, w_ffn_down=w_ffn_down,
             ln_mix_g=ln_mix_g, ln_mix_b=ln_mix_b, ln_ffn_g=ln_ffn_g, ln_ffn_b=ln_ffn_b)
    p = _prepare(w)
    n_even, n_odd, depth = w_in_ab.shape[0], w_in_cd.shape[0], w_ffn_up.shape[0]
    rel = (jnp.arange(WINDOW + CHUNK)[None, :] - WINDOW - jnp.arange(CHUNK)[:, None]).astype(jnp.int32)
    bias = _bias_call(_t5_bucket(rel).astype(jnp.int32), rel_bias)

    b, s = x_prompt.shape[:2]
    zeros = lambda *shape: jnp.zeros(shape, F32)
    prompt = _trunk(
        x_prompt, 0, zeros(n_even, b, CONV_A - 1, D_A), zeros(n_even, b, CONV_B - 1, D_B),
        zeros(n_odd, b, POOL_HIST, D_C), zeros(n_odd, b, WINDOW, N_KV_HEADS, HEAD_DIM),
        zeros(n_odd, b, WINDOW, N_KV_HEADS, HEAD_DIM), zeros(depth, b, CONV_F - 1, D_FF),
        p, bias, nb=1, tt=_tile_rows(s, MIXER_TILE_ROWS), tt_ffn=_tile_rows(s, FFN_TILE_ROWS),
        cq=CHUNK, masked_history=True)
    bs, ss = x_sample.shape[:2]
    sample = _trunk(
        x_sample, PAST_LEN, state_conv_a, state_conv_b, state_pool_c, cache_k_d, cache_v_d,
        state_ffn_conv, p, bias, nb=bs, tt=ss, tt_ffn=ss, cq=ss, masked_history=False)
    return (prompt[0], sample[0]) + tuple(prompt[1:]) + tuple(sample[1:])
```

```python
import functools
import math

import jax
import jax.numpy as jnp
from jax import lax
from jax.experimental import pallas as pl
from jax.experimental.pallas import tpu as pltpu

F32 = jnp.float32
BF16 = jnp.bfloat16

D_MODEL = 1024
DEPTH = 4
PAST_LEN = 4096
CHUNK = 64
D_A = D_MODEL // 2
CONV_A = 31
D_B = D_MODEL // 2
CONV_B = 3
POOL_WINDOWS = (2, 4, 8, 16)
D_C = D_MODEL // 4
POOL_GROUP = D_C // 4
POOL_HIST = 15
HEAD_DIM = 64
N_Q_HEADS = (D_MODEL - D_C) // HEAD_DIM
N_KV_HEADS = 4
GQA = N_Q_HEADS // N_KV_HEADS
D_ATT = N_Q_HEADS * HEAD_DIM
D_KV = N_KV_HEADS * HEAD_DIM
WINDOW = 128
NUM_BUCKETS = 32
MAX_DISTANCE = 128
ATT_SCALE = HEAD_DIM ** -0.5
NEG_INF = -1e30
D_FF = 2816
CONV_F = 3
LN_EPS = 1e-5
ALPHA = (2 * DEPTH) ** 0.25

LANES = 128
SUBLANES = 8
HIST_A_PAD = 32
HIST_B_PAD = SUBLANES
HIST_C_PAD = 16
HIST_F_PAD = SUBLANES
FF_CHUNK = 256
N_FF_CHUNKS = D_FF // FF_CHUNK
N_PAIRS = N_KV_HEADS // 2
PAIR_ROWS = 2 * GQA
CONV_ROW_BLOCK = 32
KEY_PAD = 256
GELU_C0 = math.sqrt(2.0 / math.pi)
GELU_C1 = GELU_C0 * 0.044715
VMEM_LIMIT = 56 * 1024 * 1024
MIXER_TILE_ROWS = 256
FFN_TILE_ROWS = 512
MIXER_CHAINS = 2


def _layer_norm(v, g, b):
    mu = jnp.mean(v, axis=-1, keepdims=True)
    d = v - mu
    var = jnp.mean(d * d, axis=-1, keepdims=True)
    return d * lax.rsqrt(var + LN_EPS) * g + b


def _dot(a, b):
    return jnp.dot(a, b, preferred_element_type=F32)


def _ffn_kernel(*refs, nb, tt, has_history):
    if has_history:
        x_ref, hist_ref, wup_ref, cw_ref, cb_ref, wd_ref, lg_ref, lb_ref = refs[:8]
    else:
        x_ref, wup_ref, cw_ref, cb_ref, wd_ref, lg_ref, lb_ref = refs[:7]
    y_ref, st_ref, xb_scr, g_scr0, g_scr1, v_scr0, v_scr1, acc_scr = refs[-8:]
    m = nb * tt

    @pl.when(pl.program_id(1) == 0)
    def _():
        if has_history:
            st_ref[:, 0:HIST_F_PAD - (CONV_F - 1), :] = jnp.zeros(
                (nb, HIST_F_PAD - (CONV_F - 1), D_FF), F32)
            st_ref[:, HIST_F_PAD - (CONV_F - 1):, :] = hist_ref[0]
        else:
            st_ref[...] = jnp.zeros_like(st_ref)

    x = x_ref[...].reshape(m, D_MODEL)
    xb_scr[...] = x.astype(BF16)
    n_slabs = tt // SUBLANES
    sub = lax.broadcasted_iota(jnp.int32, (1, 1, SUBLANES, FF_CHUNK), 2)

    def chunk(j, base=0):
        off = base + j * FF_CHUNK
        return pl.ds(off if isinstance(off, int) else pl.multiple_of(off, FF_CHUNK), FF_CHUNK)

    def up(j, g_scr, v_scr):
        xb = xb_scr[...]
        g_scr[...] = _dot(xb, wup_ref[0, :, chunk(j)]).reshape(nb, tt, FF_CHUNK)
        v_scr[...] = _dot(xb, wup_ref[0, :, chunk(j, D_FF)]).reshape(nb, tt, FF_CHUNK)

    def down(j, g_scr, v_scr, first=False):
        g = g_scr[...].reshape(nb, n_slabs, SUBLANES, FF_CHUNK)
        hist = st_ref[:, :, chunk(j)].reshape(nb, 1, SUBLANES, FF_CHUNK)
        st_ref[:, :, chunk(j)] = g_scr[:, tt - HIST_F_PAD:tt, :]
        cw = cw_ref[0, :, chunk(j)]
        conv = cw[CONV_F - 1:CONV_F, :] * g + cb_ref[0, :, chunk(j)]
        for d in range(1, CONV_F):
            rot = pltpu.roll(g, d, 2)
            before = jnp.concatenate([pltpu.roll(hist, d, 2), rot[:, :n_slabs - 1]], axis=1)
            conv = conv + cw[CONV_F - 1 - d:CONV_F - d, :] * jnp.where(sub >= d, rot, before)
        conv = conv.reshape(nb, tt, FF_CHUNK)
        t = jnp.tanh(conv * (GELU_C0 + GELU_C1 * (conv * conv)))
        h = (conv * v_scr[...]) * (1.0 + t)
        hb = h.reshape(m, FF_CHUNK).astype(BF16)
        wd = wd_ref[0, chunk(j), :]
        if first:
            acc_scr[...] = _dot(hb, wd)
        else:
            acc_scr[...] += _dot(hb, wd)

    up(0, g_scr0, v_scr0)
    up(1, g_scr1, v_scr1)
    down(0, g_scr0, v_scr0, first=True)

    def body(i, carry):
        j = 2 * i + 1
        up(j + 1, g_scr0, v_scr0)
        down(j, g_scr1, v_scr1)
        up(j + 2, g_scr1, v_scr1)
        down(j + 1, g_scr0, v_scr0)
        return carry

    lax.fori_loop(0, (N_FF_CHUNKS - 3) // 2, body, 0)
    up(N_FF_CHUNKS - 1, g_scr0, v_scr0)
    down(N_FF_CHUNKS - 2, g_scr1, v_scr1)
    down(N_FF_CHUNKS - 1, g_scr0, v_scr0)
    y = ALPHA * x + acc_scr[...]
    y_ref[...] = _layer_norm(y, lg_ref[0], lb_ref[0]).reshape(nb, tt, D_MODEL)


def _layer_spec(layer, *shape):
    return pl.BlockSpec((1,) + shape, lambda i, t: (layer,) + (0,) * len(shape),
                        pipeline_mode=pl.Buffered(1))


def _ffn_call(x, hist, wup, cw, cb, wd, lg, lb, *, layer, nb, tt):
    b, s, _ = x.shape
    grid = (b // nb, s // tt)
    hist_specs, hist_args = [], []
    if hist is not None:
        hist_specs = [pl.BlockSpec((1, nb, CONV_F - 1, D_FF), lambda i, t: (layer, i, 0, 0))]
        hist_args = [hist]
    return pl.pallas_call(
        functools.partial(_ffn_kernel, nb=nb, tt=tt, has_history=hist is not None),
        grid=grid,
        in_specs=[pl.BlockSpec((nb, tt, D_MODEL), lambda i, t: (i, t, 0))] + hist_specs + [
            _layer_spec(layer, D_MODEL, 2 * D_FF),
            _layer_spec(layer, CONV_F, D_FF),
            _layer_spec(layer, 1, D_FF),
            _layer_spec(layer, D_FF, D_MODEL),
            _layer_spec(layer, 1, D_MODEL),
            _layer_spec(layer, 1, D_MODEL),
        ],
        out_specs=[
            pl.BlockSpec((nb, tt, D_MODEL), lambda i, t: (i, t, 0)),
            pl.BlockSpec((nb, HIST_F_PAD, D_FF), lambda i, t: (i, 0, 0)),
        ],
        out_shape=[
            jax.ShapeDtypeStruct((b, s, D_MODEL), F32),
            jax.ShapeDtypeStruct((b, HIST_F_PAD, D_FF), F32),
        ],
        scratch_shapes=[
            pltpu.VMEM((nb * tt, D_MODEL), BF16),
            pltpu.VMEM((nb, tt, FF_CHUNK), F32),
            pltpu.VMEM((nb, tt, FF_CHUNK), F32),
            pltpu.VMEM((nb, tt, FF_CHUNK), F32),
            pltpu.VMEM((nb, tt, FF_CHUNK), F32),
            pltpu.VMEM((nb * tt, D_MODEL), F32),
        ],
        compiler_params=pltpu.CompilerParams(
            dimension_semantics=("parallel", "arbitrary"), vmem_limit_bytes=VMEM_LIMIT),
    )(x, *hist_args, wup, cw, cb, wd, lg, lb)


def _init_history(st_ref, hist_ref):
    if hist_ref is None:
        st_ref[...] = jnp.zeros_like(st_ref)
    else:
        pad = st_ref.shape[1] - hist_ref.shape[2]
        if pad:
            st_ref[:, 0:pad, :] = jnp.zeros((st_ref.shape[0], pad, st_ref.shape[2]), F32)
        st_ref[:, pad:, :] = hist_ref[0]


def _even_kernel(*refs, nb, tt, chains, has_history):
    x_ref = refs[0]
    ha_ref, hb_ref = refs[1:3] if has_history else (None, None)
    (win_ref, caw_ref, cab_ref, lag_ref, lab_ref, cbw_ref, wout_ref, lg_ref, lb_ref,
     y_ref, sta_ref, stb_ref, u_scr, sh_scr, a_scr) = refs[-15:]

    @pl.when(pl.program_id(1) == 0)
    def _():
        _init_history(sta_ref, ha_ref)
        _init_history(stb_ref, hb_ref)

    per = nb // chains
    for c in range(chains):
        rows = slice(c * per, (c + 1) * per)
        _even_chain(x_ref.at[rows], win_ref, caw_ref, cab_ref, lag_ref, lab_ref, cbw_ref, wout_ref,
                    lg_ref, lb_ref, y_ref.at[rows], sta_ref.at[rows], stb_ref.at[rows],
                    u_scr.at[rows], sh_scr.at[:, rows], a_scr.at[c * per * tt:(c + 1) * per * tt],
                    nb=per, tt=tt)


def _even_chain(x_ref, win_ref, caw_ref, cab_ref, lag_ref, lab_ref, cbw_ref, wout_ref, lg_ref, lb_ref,
                y_ref, sta_ref, stb_ref, u_scr, sh_scr, a_scr, *, nb, tt):
    m = nb * tt
    x = x_ref[...].reshape(m, D_MODEL)
    xb = x.astype(BF16)

    def proj(i):
        return _dot(xb, win_ref[0, :, i * D_A:(i + 1) * D_A])

    u = proj(0) * jax.nn.sigmoid(proj(1))
    u_scr[:, 0:HIST_A_PAD, :] = sta_ref[...]
    u_scr[:, HIST_A_PAD:HIST_A_PAD + tt, :] = u.reshape(nb, tt, D_A)
    sta_ref[...] = u_scr[:, tt:tt + HIST_A_PAD, :]
    n_slabs = (HIST_A_PAD + tt) // SUBLANES
    sub = lax.broadcasted_iota(jnp.int32, (SUBLANES, D_A), 0)
    for n in range(nb):
        slabs = [u_scr[n, SUBLANES * i:SUBLANES * (i + 1), :] for i in range(n_slabs)]
        for r in range(1, SUBLANES):
            rot = [pltpu.roll(s, SUBLANES - r, 0) for s in slabs]
            for i in range(n_slabs - 1):
                sh_scr[r - 1, n, SUBLANES * i:SUBLANES * (i + 1), :] = jnp.where(
                    sub < SUBLANES - r, rot[i], rot[i + 1])

    rb = min(CONV_ROW_BLOCK, tt)
    first = HIST_A_PAD - (CONV_A - 1)
    cab = cab_ref[0]
    lag = lag_ref[0]
    lab = lab_ref[0]
    for n in range(nb):
        for r0 in range(0, tt, rb):
            acc = jnp.broadcast_to(cab[None], (rb // SUBLANES, SUBLANES, D_A))
            for k in range(CONV_A):
                tiles, r = divmod(first + k, SUBLANES)
                lo = r0 + SUBLANES * tiles
                win = u_scr[n, lo:lo + rb, :] if r == 0 else sh_scr[r - 1, n, lo:lo + rb, :]
                acc = acc + caw_ref[0, k][None] * win.reshape(rb // SUBLANES, SUBLANES, D_A)
            a = jax.nn.silu(_layer_norm(acc.reshape(rb, D_A), lag, lab))
            a_scr[n * tt + r0:n * tt + r0 + rb, :] = a.astype(BF16)

    v = (proj(3) * proj(4)).reshape(nb, tt, D_B)
    u_scr[:, 0:HIST_B_PAD, :] = stb_ref[...]
    u_scr[:, HIST_B_PAD:HIST_B_PAD + tt, :] = v
    stb_ref[...] = u_scr[:, tt:tt + HIST_B_PAD, :]
    cbw = cbw_ref[0]
    conv_b = (cbw[0:1, :][None] * u_scr[:, HIST_B_PAD - 2:HIST_B_PAD - 2 + tt, :]
              + cbw[1:2, :][None] * u_scr[:, HIST_B_PAD - 1:HIST_B_PAD - 1 + tt, :]
              + cbw[2:3, :][None] * v)
    bo = proj(2) * conv_b.reshape(m, D_B)

    mix = (_dot(a_scr[...], wout_ref[0, 0:D_A, :])
           + _dot(bo.astype(BF16), wout_ref[0, D_A:D_A + D_B, :]))
    y = ALPHA * x + mix
    y_ref[...] = _layer_norm(y, lg_ref[0], lb_ref[0]).reshape(nb, tt, D_MODEL)


def _even_call(x, ha, hb, win, caw, cab, lag, lab, cbw, wout, lg, lb, *,
               idx, layer, nb, tt, chains):
    b, s, _ = x.shape
    grid = (b // nb, s // tt)
    hist_specs, hist_args = [], []
    if ha is not None:
        hist_specs = [pl.BlockSpec((1, nb, CONV_A - 1, D_A), lambda i, t: (idx, i, 0, 0)),
                      pl.BlockSpec((1, nb, CONV_B - 1, D_B), lambda i, t: (idx, i, 0, 0))]
        hist_args = [ha, hb]
    return pl.pallas_call(
        functools.partial(_even_kernel, nb=nb, tt=tt, chains=chains, has_history=ha is not None),
        grid=grid,
        in_specs=[pl.BlockSpec((nb, tt, D_MODEL), lambda i, t: (i, t, 0))] + hist_specs + [
            _layer_spec(idx, D_MODEL, 2 * D_A + 3 * D_B),
            _layer_spec(idx, CONV_A, SUBLANES, D_A),
            _layer_spec(idx, 1, D_A),
            _layer_spec(idx, 1, D_A),
            _layer_spec(idx, 1, D_A),
            _layer_spec(idx, CONV_B, D_B),
            _layer_spec(idx, D_A + D_B, D_MODEL),
            _layer_spec(layer, 1, D_MODEL),
            _layer_spec(layer, 1, D_MODEL),
        ],
        out_specs=[
            pl.BlockSpec((nb, tt, D_MODEL), lambda i, t: (i, t, 0)),
            pl.BlockSpec((nb, HIST_A_PAD, D_A), lambda i, t: (i, 0, 0)),
            pl.BlockSpec((nb, HIST_B_PAD, D_B), lambda i, t: (i, 0, 0)),
        ],
        out_shape=[
            jax.ShapeDtypeStruct((b, s, D_MODEL), F32),
            jax.ShapeDtypeStruct((b, HIST_A_PAD, D_A), F32),
            jax.ShapeDtypeStruct((b, HIST_B_PAD, D_B), F32),
        ],
        scratch_shapes=[
            pltpu.VMEM((nb, HIST_A_PAD + tt, D_A), F32),
            pltpu.VMEM((SUBLANES - 1, nb, HIST_A_PAD + tt, D_A), F32),
            pltpu.VMEM((nb * tt, D_A), BF16),
        ],
        compiler_params=pltpu.CompilerParams(
            dimension_semantics=("parallel", "arbitrary"), vmem_limit_bytes=VMEM_LIMIT),
    )(x, *hist_args, win, caw, cab, lag, lab, cbw, wout, lg, lb)


def _odd_kernel(*refs, nb, tt, cq, pos0, masked_history, chains):
    x_ref = refs[0]
    hc_ref, ck_ref, cv_ref = (None, None, None) if masked_history else refs[1:4]
    (win_ref, wpool_ref, pscale_ref, bias_ref, wout_ref, lg_ref, lb_ref,
     y_ref, stc_ref, nk_ref, nv_ref, c_scr, kb_scr, vb_scr, s_scr, p_scr, att_scr) = refs[-17:]

    @pl.when(pl.program_id(1) == 0)
    def _():
        _init_history(stc_ref, hc_ref)
        _init_history(nk_ref, ck_ref)
        _init_history(nv_ref, cv_ref)

    per = nb // chains
    blocks = per * (tt // cq) * N_PAIRS
    for c in range(chains):
        rows = slice(c * per, (c + 1) * per)
        _odd_chain(x_ref.at[rows], win_ref, wpool_ref, pscale_ref, bias_ref, wout_ref, lg_ref, lb_ref,
                   y_ref.at[rows], stc_ref.at[rows], nk_ref.at[rows], nv_ref.at[rows],
                   c_scr.at[rows], kb_scr.at[rows], vb_scr.at[rows],
                   s_scr.at[c * blocks:(c + 1) * blocks], p_scr.at[c * blocks:(c + 1) * blocks],
                   att_scr.at[c * per * tt:(c + 1) * per * tt],
                   nb=per, tt=tt, cq=cq, pos0=pos0, masked_history=masked_history)


def _odd_chain(x_ref, win_ref, wpool_ref, pscale_ref, bias_ref, wout_ref, lg_ref, lb_ref,
               y_ref, stc_ref, nk_ref, nv_ref, c_scr, kb_scr, vb_scr, s_scr, p_scr, att_scr, *,
               nb, tt, cq, pos0, masked_history):
    m = nb * tt
    kw = WINDOW + cq
    t_idx = pl.program_id(1)
    x = x_ref[...].reshape(m, D_MODEL)
    xb = x.astype(BF16)
    q_off = D_C
    k_off = D_C + D_ATT
    v_off = k_off + D_KV

    c = _dot(xb, win_ref[0, :, 0:D_C]).reshape(nb, tt, D_C)
    c_scr[:, 0:HIST_C_PAD, :] = stc_ref[...]
    c_scr[:, HIST_C_PAD:HIST_C_PAD + tt, :] = c
    stc_ref[...] = c_scr[:, tt:tt + HIST_C_PAD, :]

    def window_sum(lane0, lo, hi):
        acc = None
        for j in range(lo, hi):
            r = c_scr[:, HIST_C_PAD - j:HIST_C_PAD - j + tt, lane0:lane0 + LANES]
            acc = r if acc is None else acc + r
        return acc

    row = lax.broadcasted_iota(jnp.int32, (nb, tt, LANES), 1)
    lane = lax.broadcasted_iota(jnp.int32, (nb, tt, LANES), 2)
    pos1 = row + (pos0 + 1) + t_idx * tt
    low = lane < POOL_GROUP
    pooled = []
    for tile, (w_small, w_big) in enumerate(((2, 4), (8, 16))):
        s_small = window_sum(tile * LANES, 0, w_small)
        s_big = s_small + window_sum(tile * LANES, w_small, w_big)
        total = jnp.where(low, s_small, s_big)
        cnt = jnp.minimum(pos1, jnp.where(low, w_small, w_big)).astype(F32)
        pooled.append(total / cnt - c[:, :, tile * LANES:(tile + 1) * LANES])
    pooled = jnp.concatenate(pooled, axis=-1).reshape(m, D_C)
    pooled = _dot(pooled.astype(BF16), wpool_ref[0]) * pscale_ref[0]

    k_new = _dot(xb, win_ref[0, :, k_off:k_off + D_KV]).reshape(nb, tt, D_KV)
    v_new = _dot(xb, win_ref[0, :, v_off:v_off + D_KV]).reshape(nb, tt, D_KV)
    kb_scr[:, 0:WINDOW, :] = nk_ref[...].astype(BF16)
    vb_scr[:, 0:WINDOW, :] = nv_ref[...].astype(BF16)
    kb_scr[:, WINDOW:WINDOW + tt, :] = k_new.astype(BF16)
    vb_scr[:, WINDOW:WINDOW + tt, :] = v_new.astype(BF16)
    if tt >= WINDOW:
        nk_ref[...] = k_new[:, tt - WINDOW:, :]
        nv_ref[...] = v_new[:, tt - WINDOW:, :]
    else:
        nk_ref[...] = jnp.concatenate([nk_ref[:, tt:, :], k_new], axis=1)
        nv_ref[...] = jnp.concatenate([nv_ref[:, tt:, :], v_new], axis=1)
    q = _dot(xb, win_ref[0, :, q_off:q_off + D_ATT]).reshape(nb, tt, D_ATT)

    q_lane = lax.broadcasted_iota(jnp.int32, (cq, LANES), 1)
    key_row = lax.broadcasted_iota(jnp.int32, (1, KEY_PAD), 1)
    zero_keys = jnp.zeros((KEY_PAD - kw, LANES), BF16)
    n_chunks = tt // cq
    blocks = [(n, ci, p) for n in range(nb) for ci in range(n_chunks) for p in range(N_PAIRS)]

    def scores(blk):
        n, ci, p = blocks[blk]
        r0 = ci * cq
        tiles = [q[n, r0:r0 + cq, (p * GQA + g) * LANES:(p * GQA + g + 1) * LANES]
                 for g in range(GQA)]
        lhs = ([jnp.where(q_lane < HEAD_DIM, tl, 0.0) for tl in tiles]
               + [jnp.where(q_lane >= HEAD_DIM, tl, 0.0) for tl in tiles])
        lhs = jnp.concatenate(lhs, axis=0).astype(BF16)
        kt = jnp.concatenate([kb_scr[n, r0:r0 + kw, p * LANES:(p + 1) * LANES], zero_keys], axis=0)
        s = lax.dot_general(lhs, kt, (((1,), (1,)), ((), ())), preferred_element_type=F32)
        s = s + bias_ref[p]
        if masked_history and r0 < WINDOW:
            first_valid = (WINDOW - r0) - t_idx * tt
            s = s + jnp.where(key_row >= first_valid, 0.0, NEG_INF)
        s_scr[blk] = s

    def softmax(blk):
        s = s_scr[blk]
        e = jnp.exp(s - jnp.max(s, axis=-1, keepdims=True))
        inv = 1.0 / jnp.sum(e, axis=-1, keepdims=True)
        p_scr[blk] = (e * inv).astype(BF16)

    def values(blk):
        n, ci, p = blocks[blk]
        r0 = ci * cq
        vt = jnp.concatenate([vb_scr[n, r0:r0 + kw, p * LANES:(p + 1) * LANES], zero_keys], axis=0)
        pv = _dot(p_scr[blk], vt)
        for g in range(GQA):
            o = jnp.where(q_lane < HEAD_DIM, pv[g * cq:(g + 1) * cq, :],
                          pv[(GQA + g) * cq:(GQA + g + 1) * cq, :])
            col = (p * GQA + g) * LANES
            att_scr[n * tt + r0:n * tt + r0 + cq, col:col + LANES] = o.astype(BF16)

    for step in range(len(blocks) + 2):
        if step < len(blocks):
            scores(step)
        if 0 <= step - 1 < len(blocks):
            softmax(step - 1)
        if 0 <= step - 2 < len(blocks):
            values(step - 2)

    mix = (_dot(pooled.astype(BF16), wout_ref[0, 0:D_C, :])
           + _dot(att_scr[...], wout_ref[0, D_C:D_C + D_ATT, :]))
    y = ALPHA * x + mix
    y_ref[...] = _layer_norm(y, lg_ref[0], lb_ref[0]).reshape(nb, tt, D_MODEL)


def _odd_call(x, hc, ck, cv, win, wpool, pscale, bias, wout, lg, lb, *,
              idx, layer, nb, tt, cq, pos0, chains):
    b, s, _ = x.shape
    grid = (b // nb, s // tt)
    rows = PAIR_ROWS * cq
    n_blocks = nb * (tt // cq) * N_PAIRS
    hist_specs, hist_args = [], []
    if hc is not None:
        hist_specs = [pl.BlockSpec((1, nb, POOL_HIST, D_C), lambda i, t: (idx, i, 0, 0)),
                      pl.BlockSpec((1, nb, WINDOW, D_KV), lambda i, t: (idx, i, 0, 0)),
                      pl.BlockSpec((1, nb, WINDOW, D_KV), lambda i, t: (idx, i, 0, 0))]
        hist_args = [hc, ck, cv]
    return pl.pallas_call(
        functools.partial(_odd_kernel, nb=nb, tt=tt, cq=cq, pos0=pos0,
                          masked_history=hc is None, chains=chains),
        grid=grid,
        in_specs=[pl.BlockSpec((nb, tt, D_MODEL), lambda i, t: (i, t, 0))] + hist_specs + [
            _layer_spec(idx, D_MODEL, D_C + D_ATT + 2 * D_KV),
            _layer_spec(idx, D_C, D_C),
            _layer_spec(idx, 1, D_C),
            pl.BlockSpec((N_PAIRS, rows, KEY_PAD), lambda i, t: (0, 0, 0)),
            _layer_spec(idx, D_C + D_ATT, D_MODEL),
            _layer_spec(layer, 1, D_MODEL),
            _layer_spec(layer, 1, D_MODEL),
        ],
        out_specs=[
            pl.BlockSpec((nb, tt, D_MODEL), lambda i, t: (i, t, 0)),
            pl.BlockSpec((nb, HIST_C_PAD, D_C), lambda i, t: (i, 0, 0)),
            pl.BlockSpec((nb, WINDOW, D_KV), lambda i, t: (i, 0, 0)),
            pl.BlockSpec((nb, WINDOW, D_KV), lambda i, t: (i, 0, 0)),
        ],
        out_shape=[
            jax.ShapeDtypeStruct((b, s, D_MODEL), F32),
            jax.ShapeDtypeStruct((b, HIST_C_PAD, D_C), F32),
            jax.ShapeDtypeStruct((b, WINDOW, D_KV), F32),
            jax.ShapeDtypeStruct((b, WINDOW, D_KV), F32),
        ],
        scratch_shapes=[
            pltpu.VMEM((nb, HIST_C_PAD + tt, D_C), F32),
            pltpu.VMEM((nb, WINDOW + tt, D_KV), BF16),
            pltpu.VMEM((nb, WINDOW + tt, D_KV), BF16),
            pltpu.VMEM((n_blocks, rows, KEY_PAD), F32),
            pltpu.VMEM((n_blocks, rows, KEY_PAD), BF16),
            pltpu.VMEM((nb * tt, D_ATT), BF16),
        ],
        compiler_params=pltpu.CompilerParams(
            dimension_semantics=("parallel", "arbitrary"), vmem_limit_bytes=VMEM_LIMIT),
    )(x, *hist_args, win, wpool, pscale, bias, wout, lg, lb)


def _bias_kernel(bucket_ref, rb_ref, out_ref):
    bucket = bucket_ref[...]
    for p in range(N_PAIRS):
        for r in range(PAIR_ROWS):
            head = (2 * p + r // GQA) * GQA + r % GQA
            acc = jnp.zeros((CHUNK, WINDOW + CHUNK), F32)
            for b in range(NUM_BUCKETS):
                acc = jnp.where(bucket == b, rb_ref[b, head], acc)
            out_ref[p, r * CHUNK:(r + 1) * CHUNK, :] = acc


def _bias_call(bucket, rel_bias):
    return pl.pallas_call(
        _bias_kernel,
        in_specs=[
            pl.BlockSpec(memory_space=pltpu.VMEM),
            pl.BlockSpec(memory_space=pltpu.SMEM),
        ],
        out_specs=pl.BlockSpec(memory_space=pltpu.VMEM),
        out_shape=jax.ShapeDtypeStruct((N_PAIRS, PAIR_ROWS * CHUNK, WINDOW + CHUNK), F32),
    )(bucket, rel_bias)


def _t5_bucket(rel):
    nb = NUM_BUCKETS // 2
    max_exact = nb // 2
    ret = jnp.where(rel > 0, nb, 0)
    n = jnp.abs(rel)
    nf = jnp.maximum(n, 1).astype(jnp.float32)
    large = max_exact + (jnp.log(nf / max_exact) / math.log(MAX_DISTANCE / max_exact)
                         * (nb - max_exact)).astype(jnp.int32)
    large = jnp.minimum(large, nb - 1)
    return ret + jnp.where(n < max_exact, n, large)


def _prepare(w):
    p = {}
    p['w_in_ab'] = w['w_in_ab'].astype(BF16)
    p['w_out_ab'] = w['w_out_ab'].astype(BF16)
    win = w['w_in_cd']
    n_odd = win.shape[0]
    q_cols = win[:, :, D_C:D_C + D_ATT].reshape(n_odd, D_MODEL, N_PAIRS, 2, GQA, HEAD_DIM)
    q_cols = q_cols.transpose(0, 1, 2, 4, 3, 5).reshape(n_odd, D_MODEL, D_ATT) * ATT_SCALE
    p['w_in_cd'] = jnp.concatenate([win[:, :, :D_C], q_cols, win[:, :, D_C + D_ATT:]], axis=2).astype(BF16)
    wout = w['w_out_cd']
    att_rows = wout[:, D_C:, :].reshape(n_odd, N_PAIRS, 2, GQA, HEAD_DIM, D_MODEL)
    att_rows = att_rows.transpose(0, 1, 3, 2, 4, 5).reshape(n_odd, D_ATT, D_MODEL)
    p['w_out_cd'] = jnp.concatenate([wout[:, :D_C, :], att_rows], axis=1).astype(BF16)
    wp = jnp.zeros((n_odd, D_C, D_C), F32)
    for g in range(4):
        sl = slice(g * POOL_GROUP, (g + 1) * POOL_GROUP)
        wp = wp.at[:, sl, sl].set(w['w_pool'][:, g])
    p['w_pool'] = wp.astype(BF16)
    half = jnp.concatenate([jnp.ones((D_FF,), F32), jnp.full((D_FF,), 0.5, F32)])
    p['w_ffn_up'] = (w['w_ffn_up'] * half).astype(BF16)
    p['w_ffn_down'] = w['w_ffn_down'].astype(BF16)
    p['ffn_conv_w'] = w['ffn_conv_w']
    p['sinks'] = w['attn_sinks'].reshape(n_odd, N_PAIRS, PAIR_ROWS)
    caw = w['conv_a_w']
    p['conv_a_w'] = jnp.broadcast_to(caw[:, :, None, :], caw.shape[:2] + (SUBLANES, D_A))
    p['conv_b_w'] = w['conv_b_w']
    for name in ('conv_a_b', 'ln_a_g', 'ln_a_b', 'pool_scale', 'ffn_conv_b',
                 'ln_mix_g', 'ln_mix_b', 'ln_ffn_g', 'ln_ffn_b'):
        p[name] = w[name][:, None, :]
    return p


def _trunk(x, pos0, states, p, bias, *, nb, nb_ffn, tt, tt_ffn, cq, chains):
    b = x.shape[0]
    if states is None:
        hist_a = hist_b = hist_c = cache_k = cache_v = hist_f = None
    else:
        hist_a, hist_b, hist_c, cache_k, cache_v, hist_f = states
        cache_k = cache_k.reshape(cache_k.shape[0], b, WINDOW, D_KV)
        cache_v = cache_v.reshape(cache_v.shape[0], b, WINDOW, D_KV)
    bias_t = bias.reshape(N_PAIRS, PAIR_ROWS, CHUNK, WINDOW + CHUNK)[:, :, :cq, :WINDOW + cq]
    bias_t = bias_t.reshape(N_PAIRS, PAIR_ROWS * cq, WINDOW + cq)
    pad = jnp.full((N_PAIRS, PAIR_ROWS * cq, KEY_PAD - WINDOW - cq - 1), NEG_INF, F32)
    st_a, st_b, st_c, st_k, st_v, st_f = [], [], [], [], [], []
    for layer in range(DEPTH):
        i = layer // 2
        if layer % 2 == 0:
            x, sa, sb = _even_call(
                x, hist_a, hist_b, p['w_in_ab'], p['conv_a_w'], p['conv_a_b'], p['ln_a_g'],
                p['ln_a_b'], p['conv_b_w'], p['w_out_ab'], p['ln_mix_g'], p['ln_mix_b'],
                idx=i, layer=layer, nb=nb, tt=tt, chains=chains)
            st_a.append(sa[:, HIST_A_PAD - (CONV_A - 1):])
            st_b.append(sb[:, HIST_B_PAD - (CONV_B - 1):])
        else:
            sink = jnp.repeat(p['sinks'][i], cq, axis=1)[..., None]
            bias_l = jnp.concatenate([bias_t, sink, pad], axis=-1)
            x, sc, sk, sv = _odd_call(
                x, hist_c, cache_k, cache_v, p['w_in_cd'], p['w_pool'], p['pool_scale'], bias_l,
                p['w_out_cd'], p['ln_mix_g'], p['ln_mix_b'],
                idx=i, layer=layer, nb=nb, tt=tt, cq=cq, pos0=pos0, chains=chains)
            st_c.append(sc[:, HIST_C_PAD - POOL_HIST:])
            st_k.append(sk.reshape(b, WINDOW, N_KV_HEADS, HEAD_DIM))
            st_v.append(sv.reshape(b, WINDOW, N_KV_HEADS, HEAD_DIM))
        x, sf = _ffn_call(
            x, hist_f, p['w_ffn_up'], p['ffn_conv_w'], p['ffn_conv_b'], p['w_ffn_down'],
            p['ln_ffn_g'], p['ln_ffn_b'], layer=layer, nb=nb_ffn, tt=tt_ffn)
        st_f.append(sf[:, HIST_F_PAD - (CONV_F - 1):])
    return (x, jnp.stack(st_a), jnp.stack(st_b), jnp.stack(st_c), jnp.stack(st_k),
            jnp.stack(st_v), jnp.stack(st_f))


def _tile_rows(s, largest):
    for cand in (512, 256, 128, 64):
        if cand <= largest and s % cand == 0:
            return cand
    return s


def kernel(x_prompt, x_sample, state_conv_a, state_conv_b, state_pool_c, cache_k_d, cache_v_d, state_ffn_conv, w_in_ab, conv_a_w, conv_a_b, ln_a_g, ln_a_b, conv_b_w, w_out_ab, w_in_cd, w_pool, pool_scale, attn_sinks, rel_bias, w_out_cd, w_ffn_up, ffn_conv_w, ffn_conv_b, w_ffn_down, ln_mix_g, ln_mix_b, ln_ffn_g, ln_ffn_b):
    w = dict(w_in_ab=w_in_ab, conv_a_w=conv_a_w, conv_a_b=conv_a_b, ln_a_g=ln_a_g, ln_a_b=ln_a_b,
             conv_b_w=conv_b_w, w_out_ab=w_out_ab, w_in_cd=w_in_cd, w_pool=w_pool, pool_scale=pool_scale,
             attn_sinks=attn_sinks, w_out_cd=w_out_cd, w_ffn_up=w_ffn_up,
             ffn_conv_w=ffn_conv_w, ffn_conv_b=ffn_conv_b, w_ffn_down=w_ffn_down,
             ln_mix_g=ln_mix_g, ln_mix_b=ln_mix_b, ln_ffn_g=ln_ffn_g, ln_ffn_b=ln_ffn_b)
    p = _prepare(w)
    rel = (jnp.arange(WINDOW + CHUNK)[None, :] - WINDOW - jnp.arange(CHUNK)[:, None]).astype(jnp.int32)
    bias = _bias_call(_t5_bucket(rel).astype(jnp.int32), rel_bias)

    b, s = x_prompt.shape[:2]
    chains = MIXER_CHAINS if b % MIXER_CHAINS == 0 else 1
    prompt = _trunk(
        x_prompt, 0, None, p, bias, nb=chains, nb_ffn=1, tt=_tile_rows(s, MIXER_TILE_ROWS),
        tt_ffn=_tile_rows(s, FFN_TILE_ROWS), cq=CHUNK, chains=chains)
    bs, ss = x_sample.shape[:2]
    states = (state_conv_a, state_conv_b, state_pool_c, cache_k_d, cache_v_d, state_ffn_conv)
    sample = _trunk(
        x_sample, PAST_LEN, states, p, bias, nb=bs, nb_ffn=bs, tt=ss, tt_ffn=ss, cq=ss, chains=1)
    return (prompt[0], sample[0]) + tuple(prompt[1:]) + tuple(sample[1:])
```

```python
import functools
import math

import jax
import jax.numpy as jnp
from jax import lax
from jax.experimental import pallas as pl
from jax.experimental.pallas import tpu as pltpu

F32 = jnp.float32
BF16 = jnp.bfloat16

D_MODEL = 1024
DEPTH = 4
PAST_LEN = 4096
CHUNK = 64
D_A = D_MODEL // 2
CONV_A = 31
D_B = D_MODEL // 2
CONV_B = 3
POOL_WINDOWS = (2, 4, 8, 16)
D_C = D_MODEL // 4
POOL_GROUP = D_C // 4
POOL_HIST = 15
HEAD_DIM = 64
N_Q_HEADS = (D_MODEL - D_C) // HEAD_DIM
N_KV_HEADS = 4
GQA = N_Q_HEADS // N_KV_HEADS
D_ATT = N_Q_HEADS * HEAD_DIM
D_KV = N_KV_HEADS * HEAD_DIM
WINDOW = 128
NUM_BUCKETS = 32
MAX_DISTANCE = 128
ATT_SCALE = HEAD_DIM ** -0.5
NEG_INF = -1e30
D_FF = 2816
CONV_F = 3
LN_EPS = 1e-5
ALPHA = (2 * DEPTH) ** 0.25

LANES = 128
SUBLANES = 8
HIST_A_PAD = 32
HIST_B_PAD = SUBLANES
HIST_C_PAD = 16
HIST_F_PAD = SUBLANES
FF_CHUNK = 256
N_FF_CHUNKS = D_FF // FF_CHUNK
N_PAIRS = N_KV_HEADS // 2
PAIR_ROWS = 2 * GQA
CONV_ROW_BLOCK = 32
ROW_BLOCK = 64
KEY_PAD = 256
GELU_C0 = math.sqrt(2.0 / math.pi)
GELU_C1 = GELU_C0 * 0.044715
VMEM_LIMIT = 56 * 1024 * 1024
MIXER_TILE_ROWS = 256
FFN_TILE_ROWS = 1024
MIXER_CHAINS = 2


def _layer_norm(v, g, b):
    mu = jnp.mean(v, axis=-1, keepdims=True)
    d = v - mu
    var = jnp.mean(d * d, axis=-1, keepdims=True)
    return d * lax.rsqrt(var + LN_EPS) * g + b


def _dot(a, b):
    return jnp.dot(a, b, preferred_element_type=F32)


def _init_history(st_ref, hist_ref):
    if hist_ref is None:
        st_ref[...] = jnp.zeros_like(st_ref)
    else:
        pad = st_ref.shape[1] - hist_ref.shape[2]
        if pad:
            st_ref[:, 0:pad, :] = jnp.zeros((st_ref.shape[0], pad, st_ref.shape[2]), F32)
        st_ref[:, pad:, :] = hist_ref[0]


def _residual_norm(x_ref, branch, g, b, y_ref, *, nb, tt):
    rb = min(ROW_BLOCK, tt)
    for n in range(nb):
        for r0 in range(0, tt, rb):
            y = ALPHA * x_ref[n, r0:r0 + rb, :] + branch[n * tt + r0:n * tt + r0 + rb, :]
            y_ref[n, r0:r0 + rb, :] = _layer_norm(y, g, b)


def _layer_spec(layer, *shape):
    return pl.BlockSpec((1,) + shape, lambda i, t: (layer,) + (0,) * len(shape),
                        pipeline_mode=pl.Buffered(1))


def _x_spec(nb, tt):
    return pl.BlockSpec((nb, tt, D_MODEL), lambda i, t: (i, t, 0))


def _state_spec(nb, rows, cols):
    return pl.BlockSpec((nb, rows, cols), lambda i, t: (i, 0, 0))


def _compiler_params():
    return pltpu.CompilerParams(dimension_semantics=("parallel", "arbitrary"),
                                vmem_limit_bytes=VMEM_LIMIT)


def _ffn_kernel(*refs, nb, tt, has_history):
    x_ref = refs[0]
    hist_ref = refs[1] if has_history else None
    (wup_ref, cw_ref, cb_ref, wd_ref, lg_ref, lb_ref, y_ref, st_ref,
     xb_scr, g_scr0, g_scr1, v_scr0, v_scr1, acc_scr) = refs[-14:]
    m = nb * tt

    @pl.when(pl.program_id(1) == 0)
    def _():
        _init_history(st_ref, hist_ref)

    xb_scr[...] = x_ref[...].reshape(m, D_MODEL).astype(BF16)
    n_slabs = tt // SUBLANES
    sub = lax.broadcasted_iota(jnp.int32, (1, 1, SUBLANES, FF_CHUNK), 2)

    def chunk(j, base=0):
        off = base + j * FF_CHUNK
        return pl.ds(off if isinstance(off, int) else pl.multiple_of(off, FF_CHUNK), FF_CHUNK)

    def up(j, g_scr, v_scr):
        xb = xb_scr[...]
        g_scr[...] = _dot(xb, wup_ref[0, :, chunk(j)]).reshape(nb, tt, FF_CHUNK)
        v_scr[...] = _dot(xb, wup_ref[0, :, chunk(j, D_FF)]).reshape(nb, tt, FF_CHUNK)

    def down(j, g_scr, v_scr, first=False):
        g = g_scr[...].reshape(nb, n_slabs, SUBLANES, FF_CHUNK)
        hist = st_ref[:, :, chunk(j)].reshape(nb, 1, SUBLANES, FF_CHUNK)
        st_ref[:, :, chunk(j)] = g_scr[:, tt - HIST_F_PAD:tt, :]
        cw = cw_ref[0, :, chunk(j)]
        conv = cw[CONV_F - 1:CONV_F, :] * g + cb_ref[0, :, chunk(j)]
        for d in range(1, CONV_F):
            rot = pltpu.roll(g, d, 2)
            before = jnp.concatenate([pltpu.roll(hist, d, 2), rot[:, :n_slabs - 1]], axis=1)
            conv = conv + cw[CONV_F - 1 - d:CONV_F - d, :] * jnp.where(sub >= d, rot, before)
        conv = conv.reshape(nb, tt, FF_CHUNK)
        t = jnp.tanh(conv * (GELU_C0 + GELU_C1 * (conv * conv)))
        h = (conv * v_scr[...]) * (1.0 + t)
        hb = h.reshape(m, FF_CHUNK).astype(BF16)
        wd = wd_ref[0, chunk(j), :]
        if first:
            acc_scr[...] = _dot(hb, wd)
        else:
            acc_scr[...] += _dot(hb, wd)

    up(0, g_scr0, v_scr0)
    up(1, g_scr1, v_scr1)
    down(0, g_scr0, v_scr0, first=True)

    def body(i, carry):
        j = 2 * i + 1
        up(j + 1, g_scr0, v_scr0)
        down(j, g_scr1, v_scr1)
        up(j + 2, g_scr1, v_scr1)
        down(j + 1, g_scr0, v_scr0)
        return carry

    lax.fori_loop(0, (N_FF_CHUNKS - 3) // 2, body, 0)
    up(N_FF_CHUNKS - 1, g_scr0, v_scr0)
    down(N_FF_CHUNKS - 2, g_scr1, v_scr1)
    down(N_FF_CHUNKS - 1, g_scr0, v_scr0)
    _residual_norm(x_ref, acc_scr, lg_ref[0], lb_ref[0], y_ref, nb=nb, tt=tt)


def _ffn_call(x, hist, p, *, layer, nb, tt):
    b, s, _ = x.shape
    hist_specs, hist_args = [], []
    if hist is not None:
        hist_specs = [pl.BlockSpec((1, nb, CONV_F - 1, D_FF), lambda i, t: (layer, i, 0, 0))]
        hist_args = [hist]
    return pl.pallas_call(
        functools.partial(_ffn_kernel, nb=nb, tt=tt, has_history=hist is not None),
        grid=(b // nb, s // tt),
        in_specs=[_x_spec(nb, tt)] + hist_specs + [
            _layer_spec(layer, D_MODEL, 2 * D_FF),
            _layer_spec(layer, CONV_F, D_FF),
            _layer_spec(layer, 1, D_FF),
            _layer_spec(layer, D_FF, D_MODEL),
            _layer_spec(layer, 1, D_MODEL),
            _layer_spec(layer, 1, D_MODEL),
        ],
        out_specs=[_x_spec(nb, tt), _state_spec(nb, HIST_F_PAD, D_FF)],
        out_shape=[
            jax.ShapeDtypeStruct((b, s, D_MODEL), F32),
            jax.ShapeDtypeStruct((b, HIST_F_PAD, D_FF), F32),
        ],
        scratch_shapes=[
            pltpu.VMEM((nb * tt, D_MODEL), BF16),
            pltpu.VMEM((nb, tt, FF_CHUNK), F32),
            pltpu.VMEM((nb, tt, FF_CHUNK), F32),
            pltpu.VMEM((nb, tt, FF_CHUNK), F32),
            pltpu.VMEM((nb, tt, FF_CHUNK), F32),
            pltpu.VMEM((nb * tt, D_MODEL), F32),
        ],
        compiler_params=_compiler_params(),
    )(x, *hist_args, p['w_ffn_up'], p['ffn_conv_w'], p['ffn_conv_b'], p['w_ffn_down'],
      p['ln_ffn_g'], p['ln_ffn_b'])


def _even_kernel(*refs, nb, tt, chains, has_history):
    x_ref = refs[0]
    ha_ref, hb_ref = refs[1:3] if has_history else (None, None)
    params = refs[-18:-9]
    y_ref, sta_ref, stb_ref, u_scr, sh_scr, v_scr, g_scr, a_scr, b_scr = refs[-9:]

    @pl.when(pl.program_id(1) == 0)
    def _():
        _init_history(sta_ref, ha_ref)
        _init_history(stb_ref, hb_ref)

    per = nb // chains
    for c in range(chains):
        rows = slice(c * per, (c + 1) * per)
        flat = slice(c * per * tt, (c + 1) * per * tt)
        _even_chain(x_ref.at[rows], *params, y_ref.at[rows], sta_ref.at[rows], stb_ref.at[rows],
                    u_scr.at[rows], sh_scr.at[:, rows], v_scr.at[rows], g_scr.at[flat],
                    a_scr.at[flat], b_scr.at[flat], nb=per, tt=tt)


def _even_chain(x_ref, win_ref, caw_ref, cab_ref, lag_ref, lab_ref, cbw_ref, wout_ref, lg_ref, lb_ref,
                y_ref, sta_ref, stb_ref, u_scr, sh_scr, v_scr, g_scr, a_scr, b_scr, *, nb, tt):
    m = nb * tt
    xb = x_ref[...].reshape(m, D_MODEL).astype(BF16)

    def proj(i):
        return _dot(xb, win_ref[0, :, i * D_A:(i + 1) * D_A])

    blk = min(ROW_BLOCK, tt)
    val, gate = proj(0), proj(1)
    u_scr[:, 0:HIST_A_PAD, :] = sta_ref[...]
    for n in range(nb):
        for r0 in range(0, tt, blk):
            rows = slice(n * tt + r0, n * tt + r0 + blk)
            u_scr[n, HIST_A_PAD + r0:HIST_A_PAD + r0 + blk, :] = val[rows] * jax.nn.sigmoid(gate[rows])
    sta_ref[...] = u_scr[:, tt:tt + HIST_A_PAD, :]
    c_gate, b_val = proj(3), proj(4)
    v_scr[:, 0:HIST_B_PAD, :] = stb_ref[...]
    for n in range(nb):
        for r0 in range(0, tt, blk):
            rows = slice(n * tt + r0, n * tt + r0 + blk)
            v_scr[n, HIST_B_PAD + r0:HIST_B_PAD + r0 + blk, :] = c_gate[rows] * b_val[rows]
    stb_ref[...] = v_scr[:, tt:tt + HIST_B_PAD, :]
    g_scr[...] = proj(2)

    n_slabs = (HIST_A_PAD + tt) // SUBLANES
    sub = lax.broadcasted_iota(jnp.int32, (SUBLANES, D_A), 0)
    for n in range(nb):
        for r in range(1, SUBLANES):
            rot = pltpu.roll(u_scr[n, 0:SUBLANES, :], SUBLANES - r, 0)
            for i in range(n_slabs - 1):
                nxt = pltpu.roll(u_scr[n, SUBLANES * (i + 1):SUBLANES * (i + 2), :], SUBLANES - r, 0)
                sh_scr[r - 1, n, SUBLANES * i:SUBLANES * (i + 1), :] = jnp.where(
                    sub < SUBLANES - r, rot, nxt)
                rot = nxt

    rb = min(CONV_ROW_BLOCK, tt)
    first = HIST_A_PAD - (CONV_A - 1)
    cab = cab_ref[0]
    lag = lag_ref[0]
    lab = lab_ref[0]
    for n in range(nb):
        for r0 in range(0, tt, rb):
            acc = jnp.broadcast_to(cab[None], (rb // SUBLANES, SUBLANES, D_A))
            for k in range(CONV_A):
                tiles, r = divmod(first + k, SUBLANES)
                lo = r0 + SUBLANES * tiles
                win = u_scr[n, lo:lo + rb, :] if r == 0 else sh_scr[r - 1, n, lo:lo + rb, :]
                acc = acc + caw_ref[0, k][None] * win.reshape(rb // SUBLANES, SUBLANES, D_A)
            a = jax.nn.silu(_layer_norm(acc.reshape(rb, D_A), lag, lab))
            a_scr[n * tt + r0:n * tt + r0 + rb, :] = a.astype(BF16)

    cbw = cbw_ref[0]
    conv_b = (cbw[0:1, :][None] * v_scr[:, HIST_B_PAD - 2:HIST_B_PAD - 2 + tt, :]
              + cbw[1:2, :][None] * v_scr[:, HIST_B_PAD - 1:HIST_B_PAD - 1 + tt, :]
              + cbw[2:3, :][None] * v_scr[:, HIST_B_PAD:HIST_B_PAD + tt, :])
    b_scr[...] = (g_scr[...] * conv_b.reshape(m, D_B)).astype(BF16)

    mix = _dot(a_scr[...], wout_ref[0, 0:D_A, :]) + _dot(b_scr[...], wout_ref[0, D_A:D_A + D_B, :])
    _residual_norm(x_ref, mix, lg_ref[0], lb_ref[0], y_ref, nb=nb, tt=tt)


def _even_call(x, ha, hb, p, *, idx, layer, nb, tt, chains):
    b, s, _ = x.shape
    hist_specs, hist_args = [], []
    if ha is not None:
        hist_specs = [pl.BlockSpec((1, nb, CONV_A - 1, D_A), lambda i, t: (idx, i, 0, 0)),
                      pl.BlockSpec((1, nb, CONV_B - 1, D_B), lambda i, t: (idx, i, 0, 0))]
        hist_args = [ha, hb]
    return pl.pallas_call(
        functools.partial(_even_kernel, nb=nb, tt=tt, chains=chains, has_history=ha is not None),
        grid=(b // nb, s // tt),
        in_specs=[_x_spec(nb, tt)] + hist_specs + [
            _layer_spec(idx, D_MODEL, 2 * D_A + 3 * D_B),
            _layer_spec(idx, CONV_A, SUBLANES, D_A),
            _layer_spec(idx, 1, D_A),
            _layer_spec(idx, 1, D_A),
            _layer_spec(idx, 1, D_A),
            _layer_spec(idx, CONV_B, D_B),
            _layer_spec(idx, D_A + D_B, D_MODEL),
            _layer_spec(layer, 1, D_MODEL),
            _layer_spec(layer, 1, D_MODEL),
        ],
        out_specs=[_x_spec(nb, tt), _state_spec(nb, HIST_A_PAD, D_A),
                   _state_spec(nb, HIST_B_PAD, D_B)],
        out_shape=[
            jax.ShapeDtypeStruct((b, s, D_MODEL), F32),
            jax.ShapeDtypeStruct((b, HIST_A_PAD, D_A), F32),
            jax.ShapeDtypeStruct((b, HIST_B_PAD, D_B), F32),
        ],
        scratch_shapes=[
            pltpu.VMEM((nb, HIST_A_PAD + tt, D_A), F32),
            pltpu.VMEM((SUBLANES - 1, nb, HIST_A_PAD + tt, D_A), F32),
            pltpu.VMEM((nb, HIST_B_PAD + tt, D_B), F32),
            pltpu.VMEM((nb * tt, D_B), F32),
            pltpu.VMEM((nb * tt, D_A), BF16),
            pltpu.VMEM((nb * tt, D_B), BF16),
        ],
        compiler_params=_compiler_params(),
    )(x, *hist_args, p['w_in_ab'], p['conv_a_w'], p['conv_a_b'], p['ln_a_g'], p['ln_a_b'],
      p['conv_b_w'], p['w_out_ab'], p['ln_mix_g'], p['ln_mix_b'])


def _odd_kernel(*refs, nb, tt, cq, pos0, masked_history, chains):
    x_ref = refs[0]
    hc_ref, ck_ref, cv_ref = (None, None, None) if masked_history else refs[1:4]
    params = refs[-19:-12]
    (y_ref, stc_ref, nk_ref, nv_ref, c_scr, kb_scr, vb_scr, q_scr, pool_scr, s_scr, p_scr,
     att_scr) = refs[-12:]

    @pl.when(pl.program_id(1) == 0)
    def _():
        _init_history(stc_ref, hc_ref)
        _init_history(nk_ref, ck_ref)
        _init_history(nv_ref, cv_ref)

    per = nb // chains
    blocks = per * (tt // cq) * N_PAIRS
    for c in range(chains):
        rows = slice(c * per, (c + 1) * per)
        flat = slice(c * per * tt, (c + 1) * per * tt)
        blk = slice(c * blocks, (c + 1) * blocks)
        _odd_chain(x_ref.at[rows], *params, y_ref.at[rows], stc_ref.at[rows], nk_ref.at[rows],
                   nv_ref.at[rows], c_scr.at[rows], kb_scr.at[rows], vb_scr.at[rows], q_scr.at[rows],
                   pool_scr.at[flat], s_scr.at[blk], p_scr.at[blk], att_scr.at[flat],
                   nb=per, tt=tt, cq=cq, pos0=pos0, masked_history=masked_history)


def _odd_chain(x_ref, win_ref, wpool_ref, pscale_ref, bias_ref, wout_ref, lg_ref, lb_ref,
               y_ref, stc_ref, nk_ref, nv_ref, c_scr, kb_scr, vb_scr, q_scr, pool_scr, s_scr, p_scr,
               att_scr, *, nb, tt, cq, pos0, masked_history):
    m = nb * tt
    kw = WINDOW + cq
    t_idx = pl.program_id(1)
    xb = x_ref[...].reshape(m, D_MODEL).astype(BF16)
    q_off = D_C
    k_off = D_C + D_ATT
    v_off = k_off + D_KV

    c = _dot(xb, win_ref[0, :, 0:D_C]).reshape(nb, tt, D_C)
    c_scr[:, 0:HIST_C_PAD, :] = stc_ref[...]
    c_scr[:, HIST_C_PAD:HIST_C_PAD + tt, :] = c
    stc_ref[...] = c_scr[:, tt:tt + HIST_C_PAD, :]

    def window_sum(lane0, lo, hi):
        acc = None
        for j in range(lo, hi):
            r = c_scr[:, HIST_C_PAD - j:HIST_C_PAD - j + tt, lane0:lane0 + LANES]
            acc = r if acc is None else acc + r
        return acc

    row = lax.broadcasted_iota(jnp.int32, (nb, tt, LANES), 1)
    lane = lax.broadcasted_iota(jnp.int32, (nb, tt, LANES), 2)
    pos1 = row + (pos0 + 1) + t_idx * tt
    low = lane < POOL_GROUP
    pooled = []
    for tile, (w_small, w_big) in enumerate(((2, 4), (8, 16))):
        s_small = window_sum(tile * LANES, 0, w_small)
        s_big = s_small + window_sum(tile * LANES, w_small, w_big)
        total = jnp.where(low, s_small, s_big)
        cnt = jnp.minimum(pos1, jnp.where(low, w_small, w_big)).astype(F32)
        pooled.append(total / cnt - c[:, :, tile * LANES:(tile + 1) * LANES])
    pooled = jnp.concatenate(pooled, axis=-1).reshape(m, D_C)
    pool_scr[...] = (_dot(pooled.astype(BF16), wpool_ref[0]) * pscale_ref[0]).astype(BF16)

    k_new = _dot(xb, win_ref[0, :, k_off:k_off + D_KV]).reshape(nb, tt, D_KV)
    v_new = _dot(xb, win_ref[0, :, v_off:v_off + D_KV]).reshape(nb, tt, D_KV)
    kb_scr[:, 0:WINDOW, :] = nk_ref[...].astype(BF16)
    vb_scr[:, 0:WINDOW, :] = nv_ref[...].astype(BF16)
    kb_scr[:, WINDOW:WINDOW + tt, :] = k_new.astype(BF16)
    vb_scr[:, WINDOW:WINDOW + tt, :] = v_new.astype(BF16)
    if tt >= WINDOW:
        nk_ref[...] = k_new[:, tt - WINDOW:, :]
        nv_ref[...] = v_new[:, tt - WINDOW:, :]
    else:
        nk_ref[...] = jnp.concatenate([nk_ref[:, tt:, :], k_new], axis=1)
        nv_ref[...] = jnp.concatenate([nv_ref[:, tt:, :], v_new], axis=1)
    q_scr[...] = _dot(xb, win_ref[0, :, q_off:q_off + D_ATT]).reshape(nb, tt, D_ATT)

    q_lane = lax.broadcasted_iota(jnp.int32, (cq, LANES), 1)
    key_row = lax.broadcasted_iota(jnp.int32, (1, KEY_PAD), 1)
    zero_keys = jnp.zeros((KEY_PAD - kw, LANES), BF16)
    n_chunks = tt // cq
    blocks = [(n, ci, p) for n in range(nb) for ci in range(n_chunks) for p in range(N_PAIRS)]

    def scores(blk):
        n, ci, p = blocks[blk]
        r0 = ci * cq
        tiles = [q_scr[n, r0:r0 + cq, (p * GQA + g) * LANES:(p * GQA + g + 1) * LANES]
                 for g in range(GQA)]
        lhs = ([jnp.where(q_lane < HEAD_DIM, tl, 0.0) for tl in tiles]
               + [jnp.where(q_lane >= HEAD_DIM, tl, 0.0) for tl in tiles])
        lhs = jnp.concatenate(lhs, axis=0).astype(BF16)
        kt = jnp.concatenate([kb_scr[n, r0:r0 + kw, p * LANES:(p + 1) * LANES], zero_keys], axis=0)
        s = lax.dot_general(lhs, kt, (((1,), (1,)), ((), ())), preferred_element_type=F32)
        s = s + bias_ref[p]
        if masked_history and r0 < WINDOW:
            first_valid = (WINDOW - r0) - t_idx * tt
            s = s + jnp.where(key_row >= first_valid, 0.0, NEG_INF)
        s_scr[blk] = s

    def softmax(blk):
        s = s_scr[blk]
        e = jnp.exp(s - jnp.max(s, axis=-1, keepdims=True))
        inv = 1.0 / jnp.sum(e, axis=-1, keepdims=True)
        p_scr[blk] = (e * inv).astype(BF16)

    def values(blk):
        n, ci, p = blocks[blk]
        r0 = ci * cq
        vt = jnp.concatenate([vb_scr[n, r0:r0 + kw, p * LANES:(p + 1) * LANES], zero_keys], axis=0)
        pv = _dot(p_scr[blk], vt)
        for g in range(GQA):
            o = jnp.where(q_lane < HEAD_DIM, pv[g * cq:(g + 1) * cq, :],
                          pv[(GQA + g) * cq:(GQA + g + 1) * cq, :])
            col = (p * GQA + g) * LANES
            att_scr[n * tt + r0:n * tt + r0 + cq, col:col + LANES] = o.astype(BF16)

    for step in range(len(blocks) + 2):
        if step < len(blocks):
            scores(step)
        if 0 <= step - 1 < len(blocks):
            softmax(step - 1)
        if 0 <= step - 2 < len(blocks):
            values(step - 2)

    mix = (_dot(pool_scr[...], wout_ref[0, 0:D_C, :])
           + _dot(att_scr[...], wout_ref[0, D_C:D_C + D_ATT, :]))
    _residual_norm(x_ref, mix, lg_ref[0], lb_ref[0], y_ref, nb=nb, tt=tt)


def _odd_call(x, hc, ck, cv, p, bias_l, *, idx, layer, nb, tt, cq, pos0, chains):
    b, s, _ = x.shape
    rows = PAIR_ROWS * cq
    n_blocks = nb * (tt // cq) * N_PAIRS
    hist_specs, hist_args = [], []
    if hc is not None:
        hist_specs = [pl.BlockSpec((1, nb, POOL_HIST, D_C), lambda i, t: (idx, i, 0, 0)),
                      pl.BlockSpec((1, nb, WINDOW, D_KV), lambda i, t: (idx, i, 0, 0)),
                      pl.BlockSpec((1, nb, WINDOW, D_KV), lambda i, t: (idx, i, 0, 0))]
        hist_args = [hc, ck, cv]
    return pl.pallas_call(
        functools.partial(_odd_kernel, nb=nb, tt=tt, cq=cq, pos0=pos0,
                          masked_history=hc is None, chains=chains),
        grid=(b // nb, s // tt),
        in_specs=[_x_spec(nb, tt)] + hist_specs + [
            _layer_spec(idx, D_MODEL, D_C + D_ATT + 2 * D_KV),
            _layer_spec(idx, D_C, D_C),
            _layer_spec(idx, 1, D_C),
            pl.BlockSpec((N_PAIRS, rows, KEY_PAD), lambda i, t: (0, 0, 0)),
            _layer_spec(idx, D_C + D_ATT, D_MODEL),
            _layer_spec(layer, 1, D_MODEL),
            _layer_spec(layer, 1, D_MODEL),
        ],
        out_specs=[_x_spec(nb, tt), _state_spec(nb, HIST_C_PAD, D_C),
                   _state_spec(nb, WINDOW, D_KV), _state_spec(nb, WINDOW, D_KV)],
        out_shape=[
            jax.ShapeDtypeStruct((b, s, D_MODEL), F32),
            jax.ShapeDtypeStruct((b, HIST_C_PAD, D_C), F32),
            jax.ShapeDtypeStruct((b, WINDOW, D_KV), F32),
            jax.ShapeDtypeStruct((b, WINDOW, D_KV), F32),
        ],
        scratch_shapes=[
            pltpu.VMEM((nb, HIST_C_PAD + tt, D_C), F32),
            pltpu.VMEM((nb, WINDOW + tt, D_KV), BF16),
            pltpu.VMEM((nb, WINDOW + tt, D_KV), BF16),
            pltpu.VMEM((nb, tt, D_ATT), F32),
            pltpu.VMEM((nb * tt, D_C), BF16),
            pltpu.VMEM((n_blocks, rows, KEY_PAD), F32),
            pltpu.VMEM((n_blocks, rows, KEY_PAD), BF16),
            pltpu.VMEM((nb * tt, D_ATT), BF16),
        ],
        compiler_params=_compiler_params(),
    )(x, *hist_args, p['w_in_cd'], p['w_pool'], p['pool_scale'], bias_l, p['w_out_cd'],
      p['ln_mix_g'], p['ln_mix_b'])


def _bias_kernel(bucket_ref, rb_ref, out_ref):
    bucket = bucket_ref[...]
    for p in range(N_PAIRS):
        for r in range(PAIR_ROWS):
            head = (2 * p + r // GQA) * GQA + r % GQA
            acc = jnp.zeros((CHUNK, WINDOW + CHUNK), F32)
            for b in range(NUM_BUCKETS):
                acc = jnp.where(bucket == b, rb_ref[b, head], acc)
            out_ref[p, r * CHUNK:(r + 1) * CHUNK, :] = acc


def _bias_call(bucket, rel_bias):
    return pl.pallas_call(
        _bias_kernel,
        in_specs=[
            pl.BlockSpec(memory_space=pltpu.VMEM),
            pl.BlockSpec(memory_space=pltpu.SMEM),
        ],
        out_specs=pl.BlockSpec(memory_space=pltpu.VMEM),
        out_shape=jax.ShapeDtypeStruct((N_PAIRS, PAIR_ROWS * CHUNK, WINDOW + CHUNK), F32),
    )(bucket, rel_bias)


def _t5_bucket(rel):
    nb = NUM_BUCKETS // 2
    max_exact = nb // 2
    ret = jnp.where(rel > 0, nb, 0)
    n = jnp.abs(rel)
    nf = jnp.maximum(n, 1).astype(jnp.float32)
    large = max_exact + (jnp.log(nf / max_exact) / math.log(MAX_DISTANCE / max_exact)
                         * (nb - max_exact)).astype(jnp.int32)
    large = jnp.minimum(large, nb - 1)
    return ret + jnp.where(n < max_exact, n, large)


def _prepare(w):
    p = {}
    p['w_in_ab'] = w['w_in_ab'].astype(BF16)
    p['w_out_ab'] = w['w_out_ab'].astype(BF16)
    win = w['w_in_cd']
    n_odd = win.shape[0]
    q_cols = win[:, :, D_C:D_C + D_ATT].reshape(n_odd, D_MODEL, N_PAIRS, 2, GQA, HEAD_DIM)
    q_cols = q_cols.transpose(0, 1, 2, 4, 3, 5).reshape(n_odd, D_MODEL, D_ATT) * ATT_SCALE
    p['w_in_cd'] = jnp.concatenate([win[:, :, :D_C], q_cols, win[:, :, D_C + D_ATT:]], axis=2).astype(BF16)
    wout = w['w_out_cd']
    att_rows = wout[:, D_C:, :].reshape(n_odd, N_PAIRS, 2, GQA, HEAD_DIM, D_MODEL)
    att_rows = att_rows.transpose(0, 1, 3, 2, 4, 5).reshape(n_odd, D_ATT, D_MODEL)
    p['w_out_cd'] = jnp.concatenate([wout[:, :D_C, :], att_rows], axis=1).astype(BF16)
    wp = jnp.zeros((n_odd, D_C, D_C), F32)
    for g in range(4):
        sl = slice(g * POOL_GROUP, (g + 1) * POOL_GROUP)
        wp = wp.at[:, sl, sl].set(w['w_pool'][:, g])
    p['w_pool'] = wp.astype(BF16)
    half = jnp.concatenate([jnp.ones((D_FF,), F32), jnp.full((D_FF,), 0.5, F32)])
    p['w_ffn_up'] = (w['w_ffn_up'] * half).astype(BF16)
    p['w_ffn_down'] = w['w_ffn_down'].astype(BF16)
    p['ffn_conv_w'] = w['ffn_conv_w']
    p['sinks'] = w['attn_sinks'].reshape(n_odd, N_PAIRS, PAIR_ROWS)
    caw = w['conv_a_w']
    p['conv_a_w'] = jnp.broadcast_to(caw[:, :, None, :], caw.shape[:2] + (SUBLANES, D_A))
    p['conv_b_w'] = w['conv_b_w']
    for name in ('conv_a_b', 'ln_a_g', 'ln_a_b', 'pool_scale', 'ffn_conv_b',
                 'ln_mix_g', 'ln_mix_b', 'ln_ffn_g', 'ln_ffn_b'):
        p[name] = w[name][:, None, :]
    return p


def _trunk(x, pos0, states, p, bias, *, nb, nb_ffn, tt, tt_ffn, cq, chains):
    b = x.shape[0]
    if states is None:
        hist_a = hist_b = hist_c = cache_k = cache_v = hist_f = None
    else:
        hist_a, hist_b, hist_c, cache_k, cache_v, hist_f = states
        cache_k = cache_k.reshape(cache_k.shape[0], b, WINDOW, D_KV)
        cache_v = cache_v.reshape(cache_v.shape[0], b, WINDOW, D_KV)
    bias_t = bias.reshape(N_PAIRS, PAIR_ROWS, CHUNK, WINDOW + CHUNK)[:, :, :cq, :WINDOW + cq]
    bias_t = bias_t.reshape(N_PAIRS, PAIR_ROWS * cq, WINDOW + cq)
    pad = jnp.full((N_PAIRS, PAIR_ROWS * cq, KEY_PAD - WINDOW - cq - 1), NEG_INF, F32)
    st_a, st_b, st_c, st_k, st_v, st_f = [], [], [], [], [], []
    for layer in range(DEPTH):
        i = layer // 2
        if layer % 2 == 0:
            x, sa, sb = _even_call(x, hist_a, hist_b, p, idx=i, layer=layer, nb=nb, tt=tt,
                                   chains=chains)
            st_a.append(sa[:, HIST_A_PAD - (CONV_A - 1):])
            st_b.append(sb[:, HIST_B_PAD - (CONV_B - 1):])
        else:
            sink = jnp.repeat(p['sinks'][i], cq, axis=1)[..., None]
            bias_l = jnp.concatenate([bias_t, sink, pad], axis=-1)
            x, sc, sk, sv = _odd_call(x, hist_c, cache_k, cache_v, p, bias_l, idx=i, layer=layer,
                                      nb=nb, tt=tt, cq=cq, pos0=pos0, chains=chains)
            st_c.append(sc[:, HIST_C_PAD - POOL_HIST:])
            st_k.append(sk.reshape(b, WINDOW, N_KV_HEADS, HEAD_DIM))
            st_v.append(sv.reshape(b, WINDOW, N_KV_HEADS, HEAD_DIM))
        x, sf = _ffn_call(x, hist_f, p, layer=layer, nb=nb_ffn, tt=tt_ffn)
        st_f.append(sf[:, HIST_F_PAD - (CONV_F - 1):])
    return (x, jnp.stack(st_a), jnp.stack(st_b), jnp.stack(st_c), jnp.stack(st_k),
            jnp.stack(st_v), jnp.stack(st_f))


def _tile_rows(s, largest):
    for cand in (1024, 512, 256, 128, 64):
        if cand <= largest and s % cand == 0:
            return cand
    return s


def kernel(x_prompt, x_sample, state_conv_a, state_conv_b, state_pool_c, cache_k_d, cache_v_d, state_ffn_conv, w_in_ab, conv_a_w, conv_a_b, ln_a_g, ln_a_b, conv_b_w, w_out_ab, w_in_cd, w_pool, pool_scale, attn_sinks, rel_bias, w_out_cd, w_ffn_up, ffn_conv_w, ffn_conv_b, w_ffn_down, ln_mix_g, ln_mix_b, ln_ffn_g, ln_ffn_b):
    w = dict(w_in_ab=w_in_ab, conv_a_w=conv_a_w, conv_a_b=conv_a_b, ln_a_g=ln_a_g, ln_a_b=ln_a_b,
             conv_b_w=conv_b_w, w_out_ab=w_out_ab, w_in_cd=w_in_cd, w_pool=w_pool, pool_scale=pool_scale,
             attn_sinks=attn_sinks, w_out_cd=w_out_cd, w_ffn_up=w_ffn_up,
             ffn_conv_w=ffn_conv_w, ffn_conv_b=ffn_conv_b, w_ffn_down=w_ffn_down,
             ln_mix_g=ln_mix_g, ln_mix_b=ln_mix_b, ln_ffn_g=ln_ffn_g, ln_ffn_b=ln_ffn_b)
    p = _prepare(w)
    rel = (jnp.arange(WINDOW + CHUNK)[None, :] - WINDOW - jnp.arange(CHUNK)[:, None]).astype(jnp.int32)
    bias = _bias_call(_t5_bucket(rel).astype(jnp.int32), rel_bias)

    b, s = x_prompt.shape[:2]
    chains = MIXER_CHAINS if b % MIXER_CHAINS == 0 else 1
    prompt = _trunk(
        x_prompt, 0, None, p, bias, nb=chains, nb_ffn=1, tt=_tile_rows(s, MIXER_TILE_ROWS),
        tt_ffn=_tile_rows(s, FFN_TILE_ROWS), cq=CHUNK, chains=chains)
    bs, ss = x_sample.shape[:2]
    states = (state_conv_a, state_conv_b, state_pool_c, cache_k_d, cache_v_d, state_ffn_conv)
    sample = _trunk(
        x_sample, PAST_LEN, states, p, bias, nb=bs, nb_ffn=bs, tt=ss, tt_ffn=ss, cq=ss, chains=1)
    return (prompt[0], sample[0]) + tuple(prompt[1:]) + tuple(sample[1:])
```

```python
import functools
import math

import jax
import jax.numpy as jnp
from jax import lax
from jax.experimental import pallas as pl
from jax.experimental.pallas import tpu as pltpu

F32 = jnp.float32
BF16 = jnp.bfloat16

D_MODEL = 1024
DEPTH = 4
PAST_LEN = 4096
CHUNK = 64
D_A = D_MODEL // 2
CONV_A = 31
D_B = D_MODEL // 2
CONV_B = 3
POOL_WINDOWS = (2, 4, 8, 16)
D_C = D_MODEL // 4
POOL_GROUP = D_C // 4
POOL_HIST = 15
HEAD_DIM = 64
N_Q_HEADS = (D_MODEL - D_C) // HEAD_DIM
N_KV_HEADS = 4
GQA = N_Q_HEADS // N_KV_HEADS
D_ATT = N_Q_HEADS * HEAD_DIM
D_KV = N_KV_HEADS * HEAD_DIM
WINDOW = 128
NUM_BUCKETS = 32
MAX_DISTANCE = 128
ATT_SCALE = HEAD_DIM ** -0.5
NEG_INF = -1e30
D_FF = 2816
CONV_F = 3
LN_EPS = 1e-5
ALPHA = (2 * DEPTH) ** 0.25

LANES = 128
SUBLANES = 8
HIST_A_PAD = 32
HIST_B_PAD = SUBLANES
HIST_C_PAD = 16
HIST_F_PAD = SUBLANES
FF_CHUNK = 256
N_FF_CHUNKS = D_FF // FF_CHUNK
N_PAIRS = N_KV_HEADS // 2
PAIR_ROWS = 2 * GQA
CONV_ROW_BLOCK = 32
ROW_BLOCK = 64
KEY_PAD = 256
GELU_C0 = math.sqrt(2.0 / math.pi)
GELU_C1 = GELU_C0 * 0.044715
VMEM_LIMIT = 56 * 1024 * 1024
MIXER_TILE_ROWS = 256
FFN_TILE_ROWS = 1024
MIXER_BATCH_ROWS = 4


def _layer_norm(v, g, b):
    mu = jnp.mean(v, axis=-1, keepdims=True)
    d = v - mu
    var = jnp.mean(d * d, axis=-1, keepdims=True)
    return d * lax.rsqrt(var + LN_EPS) * g + b


def _dot(a, b):
    return jnp.dot(a, b, preferred_element_type=F32)


def _init_history(st_ref, hist_ref):
    if hist_ref is None:
        st_ref[...] = jnp.zeros_like(st_ref)
    else:
        pad = st_ref.shape[1] - hist_ref.shape[2]
        if pad:
            st_ref[:, 0:pad, :] = jnp.zeros((st_ref.shape[0], pad, st_ref.shape[2]), F32)
        st_ref[:, pad:, :] = hist_ref[0]


def _residual_norm(x_ref, branch, g, b, y_ref, *, nb, tt):
    rb = min(ROW_BLOCK, tt)
    for n in range(nb):
        for r0 in range(0, tt, rb):
            y = ALPHA * x_ref[n, r0:r0 + rb, :] + branch[n * tt + r0:n * tt + r0 + rb, :]
            y_ref[n, r0:r0 + rb, :] = _layer_norm(y, g, b)


def _layer_spec(layer, *shape):
    return pl.BlockSpec((1,) + shape, lambda i, t: (layer,) + (0,) * len(shape),
                        pipeline_mode=pl.Buffered(1))


def _x_spec(nb, tt):
    return pl.BlockSpec((nb, tt, D_MODEL), lambda i, t: (i, t, 0))


def _state_spec(nb, rows, cols):
    return pl.BlockSpec((nb, rows, cols), lambda i, t: (i, 0, 0))


def _compiler_params():
    return pltpu.CompilerParams(dimension_semantics=("parallel", "arbitrary"),
                                vmem_limit_bytes=VMEM_LIMIT)


def _ffn_kernel(*refs, nb, tt, has_history):
    x_ref = refs[0]
    hist_ref = refs[1] if has_history else None
    (wup_ref, cw_ref, cb_ref, wd_ref, lg_ref, lb_ref, y_ref, st_ref,
     xb_scr, g_scr0, g_scr1, v_scr0, v_scr1, acc_scr) = refs[-14:]
    m = nb * tt

    @pl.when(pl.program_id(1) == 0)
    def _():
        _init_history(st_ref, hist_ref)

    xb_scr[...] = x_ref[...].reshape(m, D_MODEL).astype(BF16)
    n_slabs = tt // SUBLANES
    sub = lax.broadcasted_iota(jnp.int32, (1, 1, SUBLANES, FF_CHUNK), 2)

    def chunk(j, base=0):
        off = base + j * FF_CHUNK
        return pl.ds(off if isinstance(off, int) else pl.multiple_of(off, FF_CHUNK), FF_CHUNK)

    def up(j, g_scr, v_scr):
        xb = xb_scr[...]
        g_scr[...] = _dot(xb, wup_ref[0, :, chunk(j)]).reshape(nb, tt, FF_CHUNK)
        v_scr[...] = _dot(xb, wup_ref[0, :, chunk(j, D_FF)]).reshape(nb, tt, FF_CHUNK)

    def down(j, g_scr, v_scr, first=False):
        g = g_scr[...].reshape(nb, n_slabs, SUBLANES, FF_CHUNK)
        hist = st_ref[:, :, chunk(j)].reshape(nb, 1, SUBLANES, FF_CHUNK)
        st_ref[:, :, chunk(j)] = g_scr[:, tt - HIST_F_PAD:tt, :]
        cw = cw_ref[0, :, chunk(j)]
        conv = cw[CONV_F - 1:CONV_F, :] * g + cb_ref[0, :, chunk(j)]
        for d in range(1, CONV_F):
            rot = pltpu.roll(g, d, 2)
            before = jnp.concatenate([pltpu.roll(hist, d, 2), rot[:, :n_slabs - 1]], axis=1)
            conv = conv + cw[CONV_F - 1 - d:CONV_F - d, :] * jnp.where(sub >= d, rot, before)
        conv = conv.reshape(nb, tt, FF_CHUNK)
        t = jnp.tanh(conv * (GELU_C0 + GELU_C1 * (conv * conv)))
        h = (conv * v_scr[...]) * (1.0 + t)
        hb = h.reshape(m, FF_CHUNK).astype(BF16)
        wd = wd_ref[0, chunk(j), :]
        if first:
            acc_scr[...] = _dot(hb, wd)
        else:
            acc_scr[...] += _dot(hb, wd)

    up(0, g_scr0, v_scr0)
    up(1, g_scr1, v_scr1)
    down(0, g_scr0, v_scr0, first=True)

    def body(i, carry):
        j = 2 * i + 1
        up(j + 1, g_scr0, v_scr0)
        down(j, g_scr1, v_scr1)
        up(j + 2, g_scr1, v_scr1)
        down(j + 1, g_scr0, v_scr0)
        return carry

    lax.fori_loop(0, (N_FF_CHUNKS - 3) // 2, body, 0)
    up(N_FF_CHUNKS - 1, g_scr0, v_scr0)
    down(N_FF_CHUNKS - 2, g_scr1, v_scr1)
    down(N_FF_CHUNKS - 1, g_scr0, v_scr0)
    _residual_norm(x_ref, acc_scr, lg_ref[0], lb_ref[0], y_ref, nb=nb, tt=tt)


def _ffn_call(x, hist, p, *, layer, nb, tt):
    b, s, _ = x.shape
    hist_specs, hist_args = [], []
    if hist is not None:
        hist_specs = [pl.BlockSpec((1, nb, CONV_F - 1, D_FF), lambda i, t: (layer, i, 0, 0))]
        hist_args = [hist]
    return pl.pallas_call(
        functools.partial(_ffn_kernel, nb=nb, tt=tt, has_history=hist is not None),
        grid=(b // nb, s // tt),
        in_specs=[_x_spec(nb, tt)] + hist_specs + [
            _layer_spec(layer, D_MODEL, 2 * D_FF),
            _layer_spec(layer, CONV_F, D_FF),
            _layer_spec(layer, 1, D_FF),
            _layer_spec(layer, D_FF, D_MODEL),
            _layer_spec(layer, 1, D_MODEL),
            _layer_spec(layer, 1, D_MODEL),
        ],
        out_specs=[_x_spec(nb, tt), _state_spec(nb, HIST_F_PAD, D_FF)],
        out_shape=[
            jax.ShapeDtypeStruct((b, s, D_MODEL), F32),
            jax.ShapeDtypeStruct((b, HIST_F_PAD, D_FF), F32),
        ],
        scratch_shapes=[
            pltpu.VMEM((nb * tt, D_MODEL), BF16),
            pltpu.VMEM((nb, tt, FF_CHUNK), F32),
            pltpu.VMEM((nb, tt, FF_CHUNK), F32),
            pltpu.VMEM((nb, tt, FF_CHUNK), F32),
            pltpu.VMEM((nb, tt, FF_CHUNK), F32),
            pltpu.VMEM((nb * tt, D_MODEL), F32),
        ],
        compiler_params=_compiler_params(),
    )(x, *hist_args, p['w_ffn_up'], p['ffn_conv_w'], p['ffn_conv_b'], p['w_ffn_down'],
      p['ln_ffn_g'], p['ln_ffn_b'])


def _even_kernel(*refs, nb, tt, chains, has_history):
    x_ref = refs[0]
    ha_ref, hb_ref = refs[1:3] if has_history else (None, None)
    params = refs[-18:-9]
    y_ref, sta_ref, stb_ref, u_scr, sh_scr, v_scr, g_scr, a_scr, b_scr = refs[-9:]

    @pl.when(pl.program_id(1) == 0)
    def _():
        _init_history(sta_ref, ha_ref)
        _init_history(stb_ref, hb_ref)

    per = nb // chains
    for c in range(chains):
        rows = slice(c * per, (c + 1) * per)
        flat = slice(c * per * tt, (c + 1) * per * tt)
        _even_chain(x_ref.at[rows], *params, y_ref.at[rows], sta_ref.at[rows], stb_ref.at[rows],
                    u_scr.at[rows], sh_scr.at[:, rows], v_scr.at[rows], g_scr.at[flat],
                    a_scr.at[flat], b_scr.at[flat], nb=per, tt=tt)


def _even_chain(x_ref, win_ref, caw_ref, cab_ref, lag_ref, lab_ref, cbw_ref, wout_ref, lg_ref, lb_ref,
                y_ref, sta_ref, stb_ref, u_scr, sh_scr, v_scr, g_scr, a_scr, b_scr, *, nb, tt):
    m = nb * tt
    xb = x_ref[...].reshape(m, D_MODEL).astype(BF16)

    def proj(i):
        return _dot(xb, win_ref[0, :, i * D_A:(i + 1) * D_A])

    blk = min(ROW_BLOCK, tt)
    val, gate = proj(0), proj(1)
    u_scr[:, 0:HIST_A_PAD, :] = sta_ref[...]
    for n in range(nb):
        for r0 in range(0, tt, blk):
            rows = slice(n * tt + r0, n * tt + r0 + blk)
            u_scr[n, HIST_A_PAD + r0:HIST_A_PAD + r0 + blk, :] = val[rows] * jax.nn.sigmoid(gate[rows])
    sta_ref[...] = u_scr[:, tt:tt + HIST_A_PAD, :]
    c_gate, b_val = proj(3), proj(4)
    v_scr[:, 0:HIST_B_PAD, :] = stb_ref[...]
    for n in range(nb):
        for r0 in range(0, tt, blk):
            rows = slice(n * tt + r0, n * tt + r0 + blk)
            v_scr[n, HIST_B_PAD + r0:HIST_B_PAD + r0 + blk, :] = c_gate[rows] * b_val[rows]
    stb_ref[...] = v_scr[:, tt:tt + HIST_B_PAD, :]
    g_scr[...] = proj(2)

    n_slabs = (HIST_A_PAD + tt) // SUBLANES
    sub = lax.broadcasted_iota(jnp.int32, (SUBLANES, D_A), 0)
    for n in range(nb):
        for r in range(1, SUBLANES):
            rot = pltpu.roll(u_scr[n, 0:SUBLANES, :], SUBLANES - r, 0)
            for i in range(n_slabs - 1):
                nxt = pltpu.roll(u_scr[n, SUBLANES * (i + 1):SUBLANES * (i + 2), :], SUBLANES - r, 0)
                sh_scr[r - 1, n, SUBLANES * i:SUBLANES * (i + 1), :] = jnp.where(
                    sub < SUBLANES - r, rot, nxt)
                rot = nxt

    rb = min(CONV_ROW_BLOCK, tt)
    first = HIST_A_PAD - (CONV_A - 1)
    cab = cab_ref[0]
    lag = lag_ref[0]
    lab = lab_ref[0]
    for n in range(nb):
        for r0 in range(0, tt, rb):
            acc = jnp.broadcast_to(cab[None], (rb // SUBLANES, SUBLANES, D_A))
            for k in range(CONV_A):
                tiles, r = divmod(first + k, SUBLANES)
                lo = r0 + SUBLANES * tiles
                win = u_scr[n, lo:lo + rb, :] if r == 0 else sh_scr[r - 1, n, lo:lo + rb, :]
                acc = acc + caw_ref[0, k][None] * win.reshape(rb // SUBLANES, SUBLANES, D_A)
            a = jax.nn.silu(_layer_norm(acc.reshape(rb, D_A), lag, lab))
            a_scr[n * tt + r0:n * tt + r0 + rb, :] = a.astype(BF16)

    cbw = cbw_ref[0]
    conv_b = (cbw[0:1, :][None] * v_scr[:, HIST_B_PAD - 2:HIST_B_PAD - 2 + tt, :]
              + cbw[1:2, :][None] * v_scr[:, HIST_B_PAD - 1:HIST_B_PAD - 1 + tt, :]
              + cbw[2:3, :][None] * v_scr[:, HIST_B_PAD:HIST_B_PAD + tt, :])
    b_scr[...] = (g_scr[...] * conv_b.reshape(m, D_B)).astype(BF16)

    mix = _dot(a_scr[...], wout_ref[0, 0:D_A, :]) + _dot(b_scr[...], wout_ref[0, D_A:D_A + D_B, :])
    _residual_norm(x_ref, mix, lg_ref[0], lb_ref[0], y_ref, nb=nb, tt=tt)


def _even_call(x, ha, hb, p, *, idx, layer, nb, tt, chains):
    b, s, _ = x.shape
    hist_specs, hist_args = [], []
    if ha is not None:
        hist_specs = [pl.BlockSpec((1, nb, CONV_A - 1, D_A), lambda i, t: (idx, i, 0, 0)),
                      pl.BlockSpec((1, nb, CONV_B - 1, D_B), lambda i, t: (idx, i, 0, 0))]
        hist_args = [ha, hb]
    return pl.pallas_call(
        functools.partial(_even_kernel, nb=nb, tt=tt, chains=chains, has_history=ha is not None),
        grid=(b // nb, s // tt),
        in_specs=[_x_spec(nb, tt)] + hist_specs + [
            _layer_spec(idx, D_MODEL, 2 * D_A + 3 * D_B),
            _layer_spec(idx, CONV_A, SUBLANES, D_A),
            _layer_spec(idx, 1, D_A),
            _layer_spec(idx, 1, D_A),
            _layer_spec(idx, 1, D_A),
            _layer_spec(idx, CONV_B, D_B),
            _layer_spec(idx, D_A + D_B, D_MODEL),
            _layer_spec(layer, 1, D_MODEL),
            _layer_spec(layer, 1, D_MODEL),
        ],
        out_specs=[_x_spec(nb, tt), _state_spec(nb, HIST_A_PAD, D_A),
                   _state_spec(nb, HIST_B_PAD, D_B)],
        out_shape=[
            jax.ShapeDtypeStruct((b, s, D_MODEL), F32),
            jax.ShapeDtypeStruct((b, HIST_A_PAD, D_A), F32),
            jax.ShapeDtypeStruct((b, HIST_B_PAD, D_B), F32),
        ],
        scratch_shapes=[
            pltpu.VMEM((nb, HIST_A_PAD + tt, D_A), F32),
            pltpu.VMEM((SUBLANES - 1, nb, HIST_A_PAD + tt, D_A), F32),
            pltpu.VMEM((nb, HIST_B_PAD + tt, D_B), F32),
            pltpu.VMEM((nb * tt, D_B), F32),
            pltpu.VMEM((nb * tt, D_A), BF16),
            pltpu.VMEM((nb * tt, D_B), BF16),
        ],
        compiler_params=_compiler_params(),
    )(x, *hist_args, p['w_in_ab'], p['conv_a_w'], p['conv_a_b'], p['ln_a_g'], p['ln_a_b'],
      p['conv_b_w'], p['w_out_ab'], p['ln_mix_g'], p['ln_mix_b'])


def _odd_kernel(*refs, nb, tt, cq, pos0, masked_history, chains):
    x_ref = refs[0]
    hc_ref, ck_ref, cv_ref = (None, None, None) if masked_history else refs[1:4]
    params = refs[-19:-12]
    (y_ref, stc_ref, nk_ref, nv_ref, c_scr, kb_scr, vb_scr, q_scr, pool_scr, s_scr, p_scr,
     att_scr) = refs[-12:]

    @pl.when(pl.program_id(1) == 0)
    def _():
        _init_history(stc_ref, hc_ref)
        _init_history(nk_ref, ck_ref)
        _init_history(nv_ref, cv_ref)

    per = nb // chains
    blocks = per * (tt // cq) * N_PAIRS
    for c in range(chains):
        rows = slice(c * per, (c + 1) * per)
        flat = slice(c * per * tt, (c + 1) * per * tt)
        blk = slice(c * blocks, (c + 1) * blocks)
        _odd_chain(x_ref.at[rows], *params, y_ref.at[rows], stc_ref.at[rows], nk_ref.at[rows],
                   nv_ref.at[rows], c_scr.at[rows], kb_scr.at[rows], vb_scr.at[rows], q_scr.at[rows],
                   pool_scr.at[flat], s_scr.at[blk], p_scr.at[blk], att_scr.at[flat],
                   nb=per, tt=tt, cq=cq, pos0=pos0, masked_history=masked_history)


def _odd_chain(x_ref, win_ref, wpool_ref, pscale_ref, bias_ref, wout_ref, lg_ref, lb_ref,
               y_ref, stc_ref, nk_ref, nv_ref, c_scr, kb_scr, vb_scr, q_scr, pool_scr, s_scr, p_scr,
               att_scr, *, nb, tt, cq, pos0, masked_history):
    m = nb * tt
    kw = WINDOW + cq
    t_idx = pl.program_id(1)
    xb = x_ref[...].reshape(m, D_MODEL).astype(BF16)
    q_off = D_C
    k_off = D_C + D_ATT
    v_off = k_off + D_KV

    c = _dot(xb, win_ref[0, :, 0:D_C]).reshape(nb, tt, D_C)
    c_scr[:, 0:HIST_C_PAD, :] = stc_ref[...]
    c_scr[:, HIST_C_PAD:HIST_C_PAD + tt, :] = c
    stc_ref[...] = c_scr[:, tt:tt + HIST_C_PAD, :]

    def window_sum(lane0, lo, hi):
        acc = None
        for j in range(lo, hi):
            r = c_scr[:, HIST_C_PAD - j:HIST_C_PAD - j + tt, lane0:lane0 + LANES]
            acc = r if acc is None else acc + r
        return acc

    row = lax.broadcasted_iota(jnp.int32, (nb, tt, LANES), 1)
    lane = lax.broadcasted_iota(jnp.int32, (nb, tt, LANES), 2)
    pos1 = row + (pos0 + 1) + t_idx * tt
    low = lane < POOL_GROUP
    pooled = []
    for tile, (w_small, w_big) in enumerate(((2, 4), (8, 16))):
        s_small = window_sum(tile * LANES, 0, w_small)
        s_big = s_small + window_sum(tile * LANES, w_small, w_big)
        total = jnp.where(low, s_small, s_big)
        cnt = jnp.minimum(pos1, jnp.where(low, w_small, w_big)).astype(F32)
        pooled.append(total / cnt - c[:, :, tile * LANES:(tile + 1) * LANES])
    pooled = jnp.concatenate(pooled, axis=-1).reshape(m, D_C)
    pool_scr[...] = (_dot(pooled.astype(BF16), wpool_ref[0]) * pscale_ref[0]).astype(BF16)

    k_new = _dot(xb, win_ref[0, :, k_off:k_off + D_KV]).reshape(nb, tt, D_KV)
    v_new = _dot(xb, win_ref[0, :, v_off:v_off + D_KV]).reshape(nb, tt, D_KV)
    kb_scr[:, 0:WINDOW, :] = nk_ref[...].astype(BF16)
    vb_scr[:, 0:WINDOW, :] = nv_ref[...].astype(BF16)
    kb_scr[:, WINDOW:WINDOW + tt, :] = k_new.astype(BF16)
    vb_scr[:, WINDOW:WINDOW + tt, :] = v_new.astype(BF16)
    if tt >= WINDOW:
        nk_ref[...] = k_new[:, tt - WINDOW:, :]
        nv_ref[...] = v_new[:, tt - WINDOW:, :]
    else:
        nk_ref[...] = jnp.concatenate([nk_ref[:, tt:, :], k_new], axis=1)
        nv_ref[...] = jnp.concatenate([nv_ref[:, tt:, :], v_new], axis=1)
    q_scr[...] = _dot(xb, win_ref[0, :, q_off:q_off + D_ATT]).reshape(nb, tt, D_ATT)

    q_lane = lax.broadcasted_iota(jnp.int32, (cq, LANES), 1)
    key_row = lax.broadcasted_iota(jnp.int32, (1, KEY_PAD), 1)
    zero_keys = jnp.zeros((KEY_PAD - kw, LANES), BF16)
    n_chunks = tt // cq
    blocks = [(n, ci, p) for n in range(nb) for ci in range(n_chunks) for p in range(N_PAIRS)]

    def scores(blk):
        n, ci, p = blocks[blk]
        r0 = ci * cq
        tiles = [q_scr[n, r0:r0 + cq, (p * GQA + g) * LANES:(p * GQA + g + 1) * LANES]
                 for g in range(GQA)]
        lhs = ([jnp.where(q_lane < HEAD_DIM, tl, 0.0) for tl in tiles]
               + [jnp.where(q_lane >= HEAD_DIM, tl, 0.0) for tl in tiles])
        lhs = jnp.concatenate(lhs, axis=0).astype(BF16)
        kt = jnp.concatenate([kb_scr[n, r0:r0 + kw, p * LANES:(p + 1) * LANES], zero_keys], axis=0)
        s = lax.dot_general(lhs, kt, (((1,), (1,)), ((), ())), preferred_element_type=F32)
        s = s + bias_ref[p]
        if masked_history and r0 < WINDOW:
            first_valid = (WINDOW - r0) - t_idx * tt
            s = s + jnp.where(key_row >= first_valid, 0.0, NEG_INF)
        s_scr[blk] = s

    def softmax(blk):
        s = s_scr[blk]
        e = jnp.exp(s - jnp.max(s, axis=-1, keepdims=True))
        inv = 1.0 / jnp.sum(e, axis=-1, keepdims=True)
        p_scr[blk] = (e * inv).astype(BF16)

    def values(blk):
        n, ci, p = blocks[blk]
        r0 = ci * cq
        vt = jnp.concatenate([vb_scr[n, r0:r0 + kw, p * LANES:(p + 1) * LANES], zero_keys], axis=0)
        pv = _dot(p_scr[blk], vt)
        for g in range(GQA):
            o = jnp.where(q_lane < HEAD_DIM, pv[g * cq:(g + 1) * cq, :],
                          pv[(GQA + g) * cq:(GQA + g + 1) * cq, :])
            col = (p * GQA + g) * LANES
            att_scr[n * tt + r0:n * tt + r0 + cq, col:col + LANES] = o.astype(BF16)

    for step in range(len(blocks) + 2):
        if step < len(blocks):
            scores(step)
        if 0 <= step - 1 < len(blocks):
            softmax(step - 1)
        if 0 <= step - 2 < len(blocks):
            values(step - 2)

    mix = (_dot(pool_scr[...], wout_ref[0, 0:D_C, :])
           + _dot(att_scr[...], wout_ref[0, D_C:D_C + D_ATT, :]))
    _residual_norm(x_ref, mix, lg_ref[0], lb_ref[0], y_ref, nb=nb, tt=tt)


def _odd_call(x, hc, ck, cv, p, bias_l, *, idx, layer, nb, tt, cq, pos0, chains):
    b, s, _ = x.shape
    rows = PAIR_ROWS * cq
    n_blocks = nb * (tt // cq) * N_PAIRS
    hist_specs, hist_args = [], []
    if hc is not None:
        hist_specs = [pl.BlockSpec((1, nb, POOL_HIST, D_C), lambda i, t: (idx, i, 0, 0)),
                      pl.BlockSpec((1, nb, WINDOW, D_KV), lambda i, t: (idx, i, 0, 0)),
                      pl.BlockSpec((1, nb, WINDOW, D_KV), lambda i, t: (idx, i, 0, 0))]
        hist_args = [hc, ck, cv]
    return pl.pallas_call(
        functools.partial(_odd_kernel, nb=nb, tt=tt, cq=cq, pos0=pos0,
                          masked_history=hc is None, chains=chains),
        grid=(b // nb, s // tt),
        in_specs=[_x_spec(nb, tt)] + hist_specs + [
            _layer_spec(idx, D_MODEL, D_C + D_ATT + 2 * D_KV),
            _layer_spec(idx, D_C, D_C),
            _layer_spec(idx, 1, D_C),
            pl.BlockSpec((N_PAIRS, rows, KEY_PAD), lambda i, t: (0, 0, 0)),
            _layer_spec(idx, D_C + D_ATT, D_MODEL),
            _layer_spec(layer, 1, D_MODEL),
            _layer_spec(layer, 1, D_MODEL),
        ],
        out_specs=[_x_spec(nb, tt), _state_spec(nb, HIST_C_PAD, D_C),
                   _state_spec(nb, WINDOW, D_KV), _state_spec(nb, WINDOW, D_KV)],
        out_shape=[
            jax.ShapeDtypeStruct((b, s, D_MODEL), F32),
            jax.ShapeDtypeStruct((b, HIST_C_PAD, D_C), F32),
            jax.ShapeDtypeStruct((b, WINDOW, D_KV), F32),
            jax.ShapeDtypeStruct((b, WINDOW, D_KV), F32),
        ],
        scratch_shapes=[
            pltpu.VMEM((nb, HIST_C_PAD + tt, D_C), F32),
            pltpu.VMEM((nb, WINDOW + tt, D_KV), BF16),
            pltpu.VMEM((nb, WINDOW + tt, D_KV), BF16),
            pltpu.VMEM((nb, tt, D_ATT), F32),
            pltpu.VMEM((nb * tt, D_C), BF16),
            pltpu.VMEM((n_blocks, rows, KEY_PAD), F32),
            pltpu.VMEM((n_blocks, rows, KEY_PAD), BF16),
            pltpu.VMEM((nb * tt, D_ATT), BF16),
        ],
        compiler_params=_compiler_params(),
    )(x, *hist_args, p['w_in_cd'], p['w_pool'], p['pool_scale'], bias_l, p['w_out_cd'],
      p['ln_mix_g'], p['ln_mix_b'])


def _bias_kernel(bucket_ref, rb_ref, out_ref):
    bucket = bucket_ref[...]
    for p in range(N_PAIRS):
        for r in range(PAIR_ROWS):
            head = (2 * p + r // GQA) * GQA + r % GQA
            acc = jnp.zeros((CHUNK, WINDOW + CHUNK), F32)
            for b in range(NUM_BUCKETS):
                acc = jnp.where(bucket == b, rb_ref[b, head], acc)
            out_ref[p, r * CHUNK:(r + 1) * CHUNK, :] = acc


def _bias_call(bucket, rel_bias):
    return pl.pallas_call(
        _bias_kernel,
        in_specs=[
            pl.BlockSpec(memory_space=pltpu.VMEM),
            pl.BlockSpec(memory_space=pltpu.SMEM),
        ],
        out_specs=pl.BlockSpec(memory_space=pltpu.VMEM),
        out_shape=jax.ShapeDtypeStruct((N_PAIRS, PAIR_ROWS * CHUNK, WINDOW + CHUNK), F32),
    )(bucket, rel_bias)


def _t5_bucket(rel):
    nb = NUM_BUCKETS // 2
    max_exact = nb // 2
    ret = jnp.where(rel > 0, nb, 0)
    n = jnp.abs(rel)
    nf = jnp.maximum(n, 1).astype(jnp.float32)
    large = max_exact + (jnp.log(nf / max_exact) / math.log(MAX_DISTANCE / max_exact)
                         * (nb - max_exact)).astype(jnp.int32)
    large = jnp.minimum(large, nb - 1)
    return ret + jnp.where(n < max_exact, n, large)


def _prepare(w):
    p = {}
    p['w_in_ab'] = w['w_in_ab'].astype(BF16)
    p['w_out_ab'] = w['w_out_ab'].astype(BF16)
    win = w['w_in_cd']
    n_odd = win.shape[0]
    q_cols = win[:, :, D_C:D_C + D_ATT].reshape(n_odd, D_MODEL, N_PAIRS, 2, GQA, HEAD_DIM)
    q_cols = q_cols.transpose(0, 1, 2, 4, 3, 5).reshape(n_odd, D_MODEL, D_ATT) * ATT_SCALE
    p['w_in_cd'] = jnp.concatenate([win[:, :, :D_C], q_cols, win[:, :, D_C + D_ATT:]], axis=2).astype(BF16)
    wout = w['w_out_cd']
    att_rows = wout[:, D_C:, :].reshape(n_odd, N_PAIRS, 2, GQA, HEAD_DIM, D_MODEL)
    att_rows = att_rows.transpose(0, 1, 3, 2, 4, 5).reshape(n_odd, D_ATT, D_MODEL)
    p['w_out_cd'] = jnp.concatenate([wout[:, :D_C, :], att_rows], axis=1).astype(BF16)
    wp = jnp.zeros((n_odd, D_C, D_C), F32)
    for g in range(4):
        sl = slice(g * POOL_GROUP, (g + 1) * POOL_GROUP)
        wp = wp.at[:, sl, sl].set(w['w_pool'][:, g])
    p['w_pool'] = wp.astype(BF16)
    half = jnp.concatenate([jnp.ones((D_FF,), F32), jnp.full((D_FF,), 0.5, F32)])
    p['w_ffn_up'] = (w['w_ffn_up'] * half).astype(BF16)
    p['w_ffn_down'] = w['w_ffn_down'].astype(BF16)
    p['ffn_conv_w'] = w['ffn_conv_w']
    p['sinks'] = w['attn_sinks'].reshape(n_odd, N_PAIRS, PAIR_ROWS)
    caw = w['conv_a_w']
    p['conv_a_w'] = jnp.broadcast_to(caw[:, :, None, :], caw.shape[:2] + (SUBLANES, D_A))
    p['conv_b_w'] = w['conv_b_w']
    for name in ('conv_a_b', 'ln_a_g', 'ln_a_b', 'pool_scale', 'ffn_conv_b',
                 'ln_mix_g', 'ln_mix_b', 'ln_ffn_g', 'ln_ffn_b'):
        p[name] = w[name][:, None, :]
    return p


def _trunk(x, pos0, states, p, bias, *, nb, nb_ffn, tt, tt_ffn, cq, chains):
    b = x.shape[0]
    if states is None:
        hist_a = hist_b = hist_c = cache_k = cache_v = hist_f = None
    else:
        hist_a, hist_b, hist_c, cache_k, cache_v, hist_f = states
        cache_k = cache_k.reshape(cache_k.shape[0], b, WINDOW, D_KV)
        cache_v = cache_v.reshape(cache_v.shape[0], b, WINDOW, D_KV)
    bias_t = bias.reshape(N_PAIRS, PAIR_ROWS, CHUNK, WINDOW + CHUNK)[:, :, :cq, :WINDOW + cq]
    bias_t = bias_t.reshape(N_PAIRS, PAIR_ROWS * cq, WINDOW + cq)
    pad = jnp.full((N_PAIRS, PAIR_ROWS * cq, KEY_PAD - WINDOW - cq - 1), NEG_INF, F32)
    st_a, st_b, st_c, st_k, st_v, st_f = [], [], [], [], [], []
    for layer in range(DEPTH):
        i = layer // 2
        if layer % 2 == 0:
            x, sa, sb = _even_call(x, hist_a, hist_b, p, idx=i, layer=layer, nb=nb, tt=tt,
                                   chains=chains)
            st_a.append(sa[:, HIST_A_PAD - (CONV_A - 1):])
            st_b.append(sb[:, HIST_B_PAD - (CONV_B - 1):])
        else:
            sink = jnp.repeat(p['sinks'][i], cq, axis=1)[..., None]
            bias_l = jnp.concatenate([bias_t, sink, pad], axis=-1)
            x, sc, sk, sv = _odd_call(x, hist_c, cache_k, cache_v, p, bias_l, idx=i, layer=layer,
                                      nb=nb, tt=tt, cq=cq, pos0=pos0, chains=chains)
            st_c.append(sc[:, HIST_C_PAD - POOL_HIST:])
            st_k.append(sk.reshape(b, WINDOW, N_KV_HEADS, HEAD_DIM))
            st_v.append(sv.reshape(b, WINDOW, N_KV_HEADS, HEAD_DIM))
        x, sf = _ffn_call(x, hist_f, p, layer=layer, nb=nb_ffn, tt=tt_ffn)
        st_f.append(sf[:, HIST_F_PAD - (CONV_F - 1):])
    return (x, jnp.stack(st_a), jnp.stack(st_b), jnp.stack(st_c), jnp.stack(st_k),
            jnp.stack(st_v), jnp.stack(st_f))


def _tile_rows(s, largest):
    for cand in (1024, 512, 256, 128, 64):
        if cand <= largest and s % cand == 0:
            return cand
    return s


def kernel(x_prompt, x_sample, state_conv_a, state_conv_b, state_pool_c, cache_k_d, cache_v_d, state_ffn_conv, w_in_ab, conv_a_w, conv_a_b, ln_a_g, ln_a_b, conv_b_w, w_out_ab, w_in_cd, w_pool, pool_scale, attn_sinks, rel_bias, w_out_cd, w_ffn_up, ffn_conv_w, ffn_conv_b, w_ffn_down, ln_mix_g, ln_mix_b, ln_ffn_g, ln_ffn_b):
    w = dict(w_in_ab=w_in_ab, conv_a_w=conv_a_w, conv_a_b=conv_a_b, ln_a_g=ln_a_g, ln_a_b=ln_a_b,
             conv_b_w=conv_b_w, w_out_ab=w_out_ab, w_in_cd=w_in_cd, w_pool=w_pool, pool_scale=pool_scale,
             attn_sinks=attn_sinks, w_out_cd=w_out_cd, w_ffn_up=w_ffn_up,
             ffn_conv_w=ffn_conv_w, ffn_conv_b=ffn_conv_b, w_ffn_down=w_ffn_down,
             ln_mix_g=ln_mix_g, ln_mix_b=ln_mix_b, ln_ffn_g=ln_ffn_g, ln_ffn_b=ln_ffn_b)
    p = _prepare(w)
    rel = (jnp.arange(WINDOW + CHUNK)[None, :] - WINDOW - jnp.arange(CHUNK)[:, None]).astype(jnp.int32)
    bias = _bias_call(_t5_bucket(rel).astype(jnp.int32), rel_bias)

    b, s = x_prompt.shape[:2]
    nb = MIXER_BATCH_ROWS if b % MIXER_BATCH_ROWS == 0 else 1
    prompt = _trunk(
        x_prompt, 0, None, p, bias, nb=nb, nb_ffn=1, tt=_tile_rows(s, MIXER_TILE_ROWS),
        tt_ffn=_tile_rows(s, FFN_TILE_ROWS), cq=CHUNK, chains=1)
    bs, ss = x_sample.shape[:2]
    states = (state_conv_a, state_conv_b, state_pool_c, cache_k_d, cache_v_d, state_ffn_conv)
    sample = _trunk(
        x_sample, PAST_LEN, states, p, bias, nb=bs, nb_ffn=bs, tt=ss, tt_ffn=ss, cq=ss, chains=1)
    return (prompt[0], sample[0]) + tuple(prompt[1:]) + tuple(sample[1:])
```

```python
import functools
import math

import jax
import jax.numpy as jnp
from jax import lax
from jax.experimental import pallas as pl
from jax.experimental.pallas import tpu as pltpu

F32 = jnp.float32
BF16 = jnp.bfloat16

D_MODEL = 1024
DEPTH = 4
PAST_LEN = 4096
CHUNK = 64
D_A = D_MODEL // 2
CONV_A = 31
D_B = D_MODEL // 2
CONV_B = 3
POOL_WINDOWS = (2, 4, 8, 16)
D_C = D_MODEL // 4
POOL_GROUP = D_C // 4
POOL_HIST = 15
HEAD_DIM = 64
N_Q_HEADS = (D_MODEL - D_C) // HEAD_DIM
N_KV_HEADS = 4
GQA = N_Q_HEADS // N_KV_HEADS
D_ATT = N_Q_HEADS * HEAD_DIM
D_KV = N_KV_HEADS * HEAD_DIM
WINDOW = 128
NUM_BUCKETS = 32
MAX_DISTANCE = 128
ATT_SCALE = HEAD_DIM ** -0.5
NEG_INF = -1e30
D_FF = 2816
CONV_F = 3
LN_EPS = 1e-5
ALPHA = (2 * DEPTH) ** 0.25

LANES = 128
SUBLANES = 8
HIST_A_PAD = 32
HIST_B_PAD = SUBLANES
HIST_C_PAD = 16
HIST_F_PAD = SUBLANES
FF_CHUNK = 256
N_FF_CHUNKS = D_FF // FF_CHUNK
N_PAIRS = N_KV_HEADS // 2
PAIR_ROWS = 2 * GQA
CONV_ROW_BLOCK = 32
CONV_STREAMS = 3
ROW_BLOCK = 64
KEY_PAD = 256
GELU_C0 = math.sqrt(2.0 / math.pi)
GELU_C1 = GELU_C0 * 0.044715
VMEM_LIMIT = 56 * 1024 * 1024
MIXER_TILE_ROWS = 256
FFN_TILE_ROWS = 1024
MIXER_BATCH_ROWS = 4


def _layer_norm(v, g, b):
    mu = jnp.mean(v, axis=-1, keepdims=True)
    d = v - mu
    var = jnp.mean(d * d, axis=-1, keepdims=True)
    return d * lax.rsqrt(var + LN_EPS) * g + b


def _dot(a, b):
    return jnp.dot(a, b, preferred_element_type=F32)


def _init_history(st_ref, hist_ref):
    if hist_ref is None:
        st_ref[...] = jnp.zeros_like(st_ref)
    else:
        pad = st_ref.shape[1] - hist_ref.shape[2]
        if pad:
            st_ref[:, 0:pad, :] = jnp.zeros((st_ref.shape[0], pad, st_ref.shape[2]), F32)
        st_ref[:, pad:, :] = hist_ref[0]


def _exact_zero(v):
    bits = pltpu.bitcast(v, jnp.uint32)
    bits = lax.shift_right_logical(lax.shift_right_logical(bits, jnp.uint32(16)), jnp.uint32(16))
    return pltpu.bitcast(bits, F32)


def _residual_norm(x_ref, branch, g, b, y_ref, *, nb, tt):
    rb = min(ROW_BLOCK, tt)
    for n in range(nb):
        for r0 in range(0, tt, rb):
            y = ALPHA * x_ref[n, r0:r0 + rb, :] + branch[n * tt + r0:n * tt + r0 + rb, :]
            y_ref[n, r0:r0 + rb, :] = _layer_norm(y, g, b)


def _layer_spec(layer, *shape):
    return pl.BlockSpec((1,) + shape, lambda i, t: (layer,) + (0,) * len(shape),
                        pipeline_mode=pl.Buffered(1))


def _x_spec(nb, tt):
    return pl.BlockSpec((nb, tt, D_MODEL), lambda i, t: (i, t, 0))


def _state_spec(nb, rows, cols):
    return pl.BlockSpec((nb, rows, cols), lambda i, t: (i, 0, 0))


def _compiler_params():
    return pltpu.CompilerParams(dimension_semantics=("parallel", "arbitrary"),
                                vmem_limit_bytes=VMEM_LIMIT)


def _ffn_kernel(*refs, nb, tt, has_history):
    x_ref = refs[0]
    hist_ref = refs[1] if has_history else None
    (wup_ref, cw_ref, cb_ref, wd_ref, lg_ref, lb_ref, y_ref, st_ref,
     xb_scr, g_scr0, g_scr1, v_scr0, v_scr1, acc_scr) = refs[-14:]
    m = nb * tt

    @pl.when(pl.program_id(1) == 0)
    def _():
        _init_history(st_ref, hist_ref)

    xb_scr[...] = x_ref[...].reshape(m, D_MODEL).astype(BF16)

    n_slabs = tt // SUBLANES
    sub = lax.broadcasted_iota(jnp.int32, (1, 1, SUBLANES, FF_CHUNK), 2)

    def chunk(j, base=0):
        off = base + j * FF_CHUNK
        return pl.ds(off if isinstance(off, int) else pl.multiple_of(off, FF_CHUNK), FF_CHUNK)

    def up(j, g_scr, v_scr):
        xb = xb_scr[...]
        g_scr[...] = _dot(xb, wup_ref[0, :, chunk(j)]).reshape(nb, tt, FF_CHUNK)
        v_scr[...] = _dot(xb, wup_ref[0, :, chunk(j, D_FF)]).reshape(nb, tt, FF_CHUNK)

    def down(j, g_scr, v_scr, first=False):
        g = g_scr[...].reshape(nb, n_slabs, SUBLANES, FF_CHUNK)
        hist = st_ref[:, :, chunk(j)].reshape(nb, 1, SUBLANES, FF_CHUNK)
        st_ref[:, :, chunk(j)] = g_scr[:, tt - HIST_F_PAD:tt, :]
        cw = cw_ref[0, :, chunk(j)]
        conv = cw[CONV_F - 1:CONV_F, :] * g + cb_ref[0, :, chunk(j)]
        for d in range(1, CONV_F):
            rot = pltpu.roll(g, d, 2)
            before = jnp.concatenate([pltpu.roll(hist, d, 2), rot[:, :n_slabs - 1]], axis=1)
            conv = conv + cw[CONV_F - 1 - d:CONV_F - d, :] * jnp.where(sub >= d, rot, before)
        conv = conv.reshape(nb, tt, FF_CHUNK)
        t = jnp.tanh(conv * (GELU_C0 + GELU_C1 * (conv * conv)))
        h = (conv * v_scr[...]) * (1.0 + t)
        hb = h.reshape(m, FF_CHUNK).astype(BF16)
        wd = wd_ref[0, chunk(j), :]
        if first:
            acc_scr[...] = _dot(hb, wd)
        else:
            acc_scr[...] += _dot(hb, wd)

    up(0, g_scr0, v_scr0)
    up(1, g_scr1, v_scr1)
    down(0, g_scr0, v_scr0, first=True)

    def body(i, carry):
        j = 2 * i + 1
        up(j + 1, g_scr0, v_scr0)
        down(j, g_scr1, v_scr1)
        up(j + 2, g_scr1, v_scr1)
        down(j + 1, g_scr0, v_scr0)
        return carry

    lax.fori_loop(0, (N_FF_CHUNKS - 3) // 2, body, 0)
    up(N_FF_CHUNKS - 1, g_scr0, v_scr0)
    down(N_FF_CHUNKS - 2, g_scr1, v_scr1)
    down(N_FF_CHUNKS - 1, g_scr0, v_scr0)
    _residual_norm(x_ref, acc_scr, lg_ref[0], lb_ref[0], y_ref, nb=nb, tt=tt)


def _ffn_call(x, hist, p, *, layer, nb, tt):
    b, s, _ = x.shape
    hist_specs, hist_args = [], []
    if hist is not None:
        hist_specs = [pl.BlockSpec((1, nb, CONV_F - 1, D_FF), lambda i, t: (layer, i, 0, 0))]
        hist_args = [hist]
    return pl.pallas_call(
        functools.partial(_ffn_kernel, nb=nb, tt=tt, has_history=hist is not None),
        grid=(b // nb, s // tt),
        in_specs=[_x_spec(nb, tt)] + hist_specs + [
            _layer_spec(layer, D_MODEL, 2 * D_FF),
            _layer_spec(layer, CONV_F, D_FF),
            _layer_spec(layer, 1, D_FF),
            _layer_spec(layer, D_FF, D_MODEL),
            _layer_spec(layer, 1, D_MODEL),
            _layer_spec(layer, 1, D_MODEL),
        ],
        out_specs=[_x_spec(nb, tt), _state_spec(nb, HIST_F_PAD, D_FF)],
        out_shape=[
            jax.ShapeDtypeStruct((b, s, D_MODEL), F32),
            jax.ShapeDtypeStruct((b, HIST_F_PAD, D_FF), F32),
        ],
        scratch_shapes=[
            pltpu.VMEM((nb * tt, D_MODEL), BF16),
            pltpu.VMEM((nb, tt, FF_CHUNK), F32),
            pltpu.VMEM((nb, tt, FF_CHUNK), F32),
            pltpu.VMEM((nb, tt, FF_CHUNK), F32),
            pltpu.VMEM((nb, tt, FF_CHUNK), F32),
            pltpu.VMEM((nb * tt, D_MODEL), F32),
        ],
        compiler_params=_compiler_params(),
    )(x, *hist_args, p['w_ffn_up'], p['ffn_conv_w'], p['ffn_conv_b'], p['w_ffn_down'],
      p['ln_ffn_g'], p['ln_ffn_b'])


def _even_kernel(*refs, nb, tt, has_history):
    x_ref = refs[0]
    ha_ref, hb_ref = refs[1:3] if has_history else (None, None)
    (win_ref, caw_ref, cab_ref, lag_ref, lab_ref, cbw_ref, wout_ref, lg_ref, lb_ref,
     y_ref, sta_ref, stb_ref, u_scr, sh_scr, v_scr, g_scr, a_scr, b_scr) = refs[-18:]

    @pl.when(pl.program_id(1) == 0)
    def _():
        _init_history(sta_ref, ha_ref)
        _init_history(stb_ref, hb_ref)

    m = nb * tt
    xb = x_ref[...].reshape(m, D_MODEL).astype(BF16)

    def proj(i):
        return _dot(xb, win_ref[0, :, i * D_A:(i + 1) * D_A])

    blk = min(ROW_BLOCK, tt)
    val, gate = proj(0), proj(1)
    u_scr[:, 0:HIST_A_PAD, :] = sta_ref[...]
    for n in range(nb):
        for r0 in range(0, tt, blk):
            rows = slice(n * tt + r0, n * tt + r0 + blk)
            u_scr[n, HIST_A_PAD + r0:HIST_A_PAD + r0 + blk, :] = val[rows] * jax.nn.sigmoid(gate[rows])
    sta_ref[...] = u_scr[:, tt:tt + HIST_A_PAD, :]
    c_gate, b_val = proj(3), proj(4)
    v_scr[:, 0:HIST_B_PAD, :] = stb_ref[...]
    for n in range(nb):
        for r0 in range(0, tt, blk):
            rows = slice(n * tt + r0, n * tt + r0 + blk)
            v_scr[n, HIST_B_PAD + r0:HIST_B_PAD + r0 + blk, :] = c_gate[rows] * b_val[rows]
    stb_ref[...] = v_scr[:, tt:tt + HIST_B_PAD, :]
    g_scr[...] = proj(2)

    n_slabs = (HIST_A_PAD + tt) // SUBLANES
    sub = lax.broadcasted_iota(jnp.int32, (SUBLANES, D_A), 0)
    for n in range(nb):
        for r in range(1, SUBLANES):
            rot = pltpu.roll(u_scr[n, 0:SUBLANES, :], SUBLANES - r, 0)
            for i in range(n_slabs - 1):
                nxt = pltpu.roll(u_scr[n, SUBLANES * (i + 1):SUBLANES * (i + 2), :], SUBLANES - r, 0)
                sh_scr[r - 1, n, SUBLANES * i:SUBLANES * (i + 1), :] = jnp.where(
                    sub < SUBLANES - r, rot, nxt)
                rot = nxt

    rb = min(CONV_ROW_BLOCK, tt)
    first = HIST_A_PAD - (CONV_A - 1)
    cab = cab_ref[0]
    lag = lag_ref[0]
    lab = lab_ref[0]
    recent = []
    for n in range(nb):
        for r0 in range(0, tt, rb):
            start = cab
            if len(recent) >= CONV_STREAMS:
                start = cab + _exact_zero(recent[-CONV_STREAMS])
            acc = jnp.broadcast_to(start[None], (rb // SUBLANES, SUBLANES, D_A))
            for k in range(CONV_A):
                tiles, r = divmod(first + k, SUBLANES)
                lo = r0 + SUBLANES * tiles
                win = u_scr[n, lo:lo + rb, :] if r == 0 else sh_scr[r - 1, n, lo:lo + rb, :]
                acc = acc + caw_ref[0, k][None] * win.reshape(rb // SUBLANES, SUBLANES, D_A)
            a = jax.nn.silu(_layer_norm(acc.reshape(rb, D_A), lag, lab))
            recent.append(a[0:SUBLANES, :])
            a_scr[n * tt + r0:n * tt + r0 + rb, :] = a.astype(BF16)

    cbw = cbw_ref[0]
    conv_b = (cbw[0:1, :][None] * v_scr[:, HIST_B_PAD - 2:HIST_B_PAD - 2 + tt, :]
              + cbw[1:2, :][None] * v_scr[:, HIST_B_PAD - 1:HIST_B_PAD - 1 + tt, :]
              + cbw[2:3, :][None] * v_scr[:, HIST_B_PAD:HIST_B_PAD + tt, :])
    b_scr[...] = (g_scr[...] * conv_b.reshape(m, D_B)).astype(BF16)

    mix = _dot(a_scr[...], wout_ref[0, 0:D_A, :]) + _dot(b_scr[...], wout_ref[0, D_A:D_A + D_B, :])
    _residual_norm(x_ref, mix, lg_ref[0], lb_ref[0], y_ref, nb=nb, tt=tt)


def _even_call(x, ha, hb, p, *, idx, layer, nb, tt):
    b, s, _ = x.shape
    hist_specs, hist_args = [], []
    if ha is not None:
        hist_specs = [pl.BlockSpec((1, nb, CONV_A - 1, D_A), lambda i, t: (idx, i, 0, 0)),
                      pl.BlockSpec((1, nb, CONV_B - 1, D_B), lambda i, t: (idx, i, 0, 0))]
        hist_args = [ha, hb]
    return pl.pallas_call(
        functools.partial(_even_kernel, nb=nb, tt=tt, has_history=ha is not None),
        grid=(b // nb, s // tt),
        in_specs=[_x_spec(nb, tt)] + hist_specs + [
            _layer_spec(idx, D_MODEL, 2 * D_A + 3 * D_B),
            _layer_spec(idx, CONV_A, SUBLANES, D_A),
            _layer_spec(idx, 1, D_A),
            _layer_spec(idx, 1, D_A),
            _layer_spec(idx, 1, D_A),
            _layer_spec(idx, CONV_B, D_B),
            _layer_spec(idx, D_A + D_B, D_MODEL),
            _layer_spec(layer, 1, D_MODEL),
            _layer_spec(layer, 1, D_MODEL),
        ],
        out_specs=[_x_spec(nb, tt), _state_spec(nb, HIST_A_PAD, D_A),
                   _state_spec(nb, HIST_B_PAD, D_B)],
        out_shape=[
            jax.ShapeDtypeStruct((b, s, D_MODEL), F32),
            jax.ShapeDtypeStruct((b, HIST_A_PAD, D_A), F32),
            jax.ShapeDtypeStruct((b, HIST_B_PAD, D_B), F32),
        ],
        scratch_shapes=[
            pltpu.VMEM((nb, HIST_A_PAD + tt, D_A), F32),
            pltpu.VMEM((SUBLANES - 1, nb, HIST_A_PAD + tt, D_A), F32),
            pltpu.VMEM((nb, HIST_B_PAD + tt, D_B), F32),
            pltpu.VMEM((nb * tt, D_B), F32),
            pltpu.VMEM((nb * tt, D_A), BF16),
            pltpu.VMEM((nb * tt, D_B), BF16),
        ],
        compiler_params=_compiler_params(),
    )(x, *hist_args, p['w_in_ab'], p['conv_a_w'], p['conv_a_b'], p['ln_a_g'], p['ln_a_b'],
      p['conv_b_w'], p['w_out_ab'], p['ln_mix_g'], p['ln_mix_b'])


def _odd_kernel(*refs, nb, tt, cq, pos0, masked_history):
    x_ref = refs[0]
    hc_ref, ck_ref, cv_ref = (None, None, None) if masked_history else refs[1:4]
    (win_ref, wpool_ref, pscale_ref, bias_ref, wout_ref, lg_ref, lb_ref,
     y_ref, stc_ref, nk_ref, nv_ref, c_scr, kb_scr, vb_scr, q_scr, pool_scr, s_scr, p_scr,
     att_scr) = refs[-19:]

    @pl.when(pl.program_id(1) == 0)
    def _():
        _init_history(stc_ref, hc_ref)
        _init_history(nk_ref, ck_ref)
        _init_history(nv_ref, cv_ref)

    m = nb * tt
    kw = WINDOW + cq
    t_idx = pl.program_id(1)
    xb = x_ref[...].reshape(m, D_MODEL).astype(BF16)
    q_off = D_C
    k_off = D_C + D_ATT
    v_off = k_off + D_KV

    c = _dot(xb, win_ref[0, :, 0:D_C]).reshape(nb, tt, D_C)
    c_scr[:, 0:HIST_C_PAD, :] = stc_ref[...]
    c_scr[:, HIST_C_PAD:HIST_C_PAD + tt, :] = c
    stc_ref[...] = c_scr[:, tt:tt + HIST_C_PAD, :]

    def window_sum(lane0, lo, hi):
        acc = None
        for j in range(lo, hi):
            r = c_scr[:, HIST_C_PAD - j:HIST_C_PAD - j + tt, lane0:lane0 + LANES]
            acc = r if acc is None else acc + r
        return acc

    row = lax.broadcasted_iota(jnp.int32, (nb, tt, LANES), 1)
    lane = lax.broadcasted_iota(jnp.int32, (nb, tt, LANES), 2)
    pos1 = row + (pos0 + 1) + t_idx * tt
    low = lane < POOL_GROUP
    pooled = []
    for tile, (w_small, w_big) in enumerate(((2, 4), (8, 16))):
        s_small = window_sum(tile * LANES, 0, w_small)
        s_big = s_small + window_sum(tile * LANES, w_small, w_big)
        total = jnp.where(low, s_small, s_big)
        cnt = jnp.minimum(pos1, jnp.where(low, w_small, w_big)).astype(F32)
        pooled.append(total / cnt - c[:, :, tile * LANES:(tile + 1) * LANES])
    pooled = jnp.concatenate(pooled, axis=-1).reshape(m, D_C)
    pool_scr[...] = (_dot(pooled.astype(BF16), wpool_ref[0]) * pscale_ref[0]).astype(BF16)

    k_new = _dot(xb, win_ref[0, :, k_off:k_off + D_KV]).reshape(nb, tt, D_KV)
    v_new = _dot(xb, win_ref[0, :, v_off:v_off + D_KV]).reshape(nb, tt, D_KV)
    kb_scr[:, 0:WINDOW, :] = nk_ref[...].astype(BF16)
    vb_scr[:, 0:WINDOW, :] = nv_ref[...].astype(BF16)
    kb_scr[:, WINDOW:WINDOW + tt, :] = k_new.astype(BF16)
    vb_scr[:, WINDOW:WINDOW + tt, :] = v_new.astype(BF16)
    if tt >= WINDOW:
        nk_ref[...] = k_new[:, tt - WINDOW:, :]
        nv_ref[...] = v_new[:, tt - WINDOW:, :]
    else:
        nk_ref[...] = jnp.concatenate([nk_ref[:, tt:, :], k_new], axis=1)
        nv_ref[...] = jnp.concatenate([nv_ref[:, tt:, :], v_new], axis=1)
    q_scr[...] = _dot(xb, win_ref[0, :, q_off:q_off + D_ATT]).reshape(nb, tt, D_ATT)

    q_lane = lax.broadcasted_iota(jnp.int32, (cq, LANES), 1)
    key_row = lax.broadcasted_iota(jnp.int32, (1, KEY_PAD), 1)
    zero_keys = jnp.zeros((KEY_PAD - kw, LANES), BF16)
    n_chunks = tt // cq
    blocks = [(n, ci, p) for n in range(nb) for ci in range(n_chunks) for p in range(N_PAIRS)]

    def scores(blk):
        n, ci, p = blocks[blk]
        r0 = ci * cq
        tiles = [q_scr[n, r0:r0 + cq, (p * GQA + g) * LANES:(p * GQA + g + 1) * LANES]
                 for g in range(GQA)]
        lhs = ([jnp.where(q_lane < HEAD_DIM, tl, 0.0) for tl in tiles]
               + [jnp.where(q_lane >= HEAD_DIM, tl, 0.0) for tl in tiles])
        lhs = jnp.concatenate(lhs, axis=0).astype(BF16)
        kt = jnp.concatenate([kb_scr[n, r0:r0 + kw, p * LANES:(p + 1) * LANES], zero_keys], axis=0)
        s = lax.dot_general(lhs, kt, (((1,), (1,)), ((), ())), preferred_element_type=F32)
        s = s + bias_ref[p]
        if masked_history and r0 < WINDOW:
            first_valid = (WINDOW - r0) - t_idx * tt
            s = s + jnp.where(key_row >= first_valid, 0.0, NEG_INF)
        s_scr[blk] = s

    def softmax(blk):
        s = s_scr[blk]
        e = jnp.exp(s - jnp.max(s, axis=-1, keepdims=True))
        inv = 1.0 / jnp.sum(e, axis=-1, keepdims=True)
        p_scr[blk] = (e * inv).astype(BF16)

    def values(blk):
        n, ci, p = blocks[blk]
        r0 = ci * cq
        vt = jnp.concatenate([vb_scr[n, r0:r0 + kw, p * LANES:(p + 1) * LANES], zero_keys], axis=0)
        pv = _dot(p_scr[blk], vt)
        for g in range(GQA):
            o = jnp.where(q_lane < HEAD_DIM, pv[g * cq:(g + 1) * cq, :],
                          pv[(GQA + g) * cq:(GQA + g + 1) * cq, :])
            col = (p * GQA + g) * LANES
            att_scr[n * tt + r0:n * tt + r0 + cq, col:col + LANES] = o.astype(BF16)

    for step in range(len(blocks) + 2):
        if step < len(blocks):
            scores(step)
        if 0 <= step - 1 < len(blocks):
            softmax(step - 1)
        if 0 <= step - 2 < len(blocks):
            values(step - 2)

    mix = (_dot(pool_scr[...], wout_ref[0, 0:D_C, :])
           + _dot(att_scr[...], wout_ref[0, D_C:D_C + D_ATT, :]))
    _residual_norm(x_ref, mix, lg_ref[0], lb_ref[0], y_ref, nb=nb, tt=tt)


def _odd_call(x, hc, ck, cv, p, bias_l, *, idx, layer, nb, tt, cq, pos0):
    b, s, _ = x.shape
    rows = PAIR_ROWS * cq
    n_blocks = nb * (tt // cq) * N_PAIRS
    hist_specs, hist_args = [], []
    if hc is not None:
        hist_specs = [pl.BlockSpec((1, nb, POOL_HIST, D_C), lambda i, t: (idx, i, 0, 0)),
                      pl.BlockSpec((1, nb, WINDOW, D_KV), lambda i, t: (idx, i, 0, 0)),
                      pl.BlockSpec((1, nb, WINDOW, D_KV), lambda i, t: (idx, i, 0, 0))]
        hist_args = [hc, ck, cv]
    return pl.pallas_call(
        functools.partial(_odd_kernel, nb=nb, tt=tt, cq=cq, pos0=pos0,
                          masked_history=hc is None),
        grid=(b // nb, s // tt),
        in_specs=[_x_spec(nb, tt)] + hist_specs + [
            _layer_spec(idx, D_MODEL, D_C + D_ATT + 2 * D_KV),
            _layer_spec(idx, D_C, D_C),
            _layer_spec(idx, 1, D_C),
            pl.BlockSpec((N_PAIRS, rows, KEY_PAD), lambda i, t: (0, 0, 0)),
            _layer_spec(idx, D_C + D_ATT, D_MODEL),
            _layer_spec(layer, 1, D_MODEL),
            _layer_spec(layer, 1, D_MODEL),
        ],
        out_specs=[_x_spec(nb, tt), _state_spec(nb, HIST_C_PAD, D_C),
                   _state_spec(nb, WINDOW, D_KV), _state_spec(nb, WINDOW, D_KV)],
        out_shape=[
            jax.ShapeDtypeStruct((b, s, D_MODEL), F32),
            jax.ShapeDtypeStruct((b, HIST_C_PAD, D_C), F32),
            jax.ShapeDtypeStruct((b, WINDOW, D_KV), F32),
            jax.ShapeDtypeStruct((b, WINDOW, D_KV), F32),
        ],
        scratch_shapes=[
            pltpu.VMEM((nb, HIST_C_PAD + tt, D_C), F32),
            pltpu.VMEM((nb, WINDOW + tt, D_KV), BF16),
            pltpu.VMEM((nb, WINDOW + tt, D_KV), BF16),
            pltpu.VMEM((nb, tt, D_ATT), F32),
            pltpu.VMEM((nb * tt, D_C), BF16),
            pltpu.VMEM((n_blocks, rows, KEY_PAD), F32),
            pltpu.VMEM((n_blocks, rows, KEY_PAD), BF16),
            pltpu.VMEM((nb * tt, D_ATT), BF16),
        ],
        compiler_params=_compiler_params(),
    )(x, *hist_args, p['w_in_cd'], p['w_pool'], p['pool_scale'], bias_l, p['w_out_cd'],
      p['ln_mix_g'], p['ln_mix_b'])


def _bias_kernel(bucket_ref, rb_ref, out_ref):
    bucket = bucket_ref[...]
    for p in range(N_PAIRS):
        for r in range(PAIR_ROWS):
            head = (2 * p + r // GQA) * GQA + r % GQA
            acc = jnp.zeros((CHUNK, WINDOW + CHUNK), F32)
            for b in range(NUM_BUCKETS):
                acc = jnp.where(bucket == b, rb_ref[b, head], acc)
            out_ref[p, r * CHUNK:(r + 1) * CHUNK, :] = acc


def _bias_call(bucket, rel_bias):
    return pl.pallas_call(
        _bias_kernel,
        in_specs=[
            pl.BlockSpec(memory_space=pltpu.VMEM),
            pl.BlockSpec(memory_space=pltpu.SMEM),
        ],
        out_specs=pl.BlockSpec(memory_space=pltpu.VMEM),
        out_shape=jax.ShapeDtypeStruct((N_PAIRS, PAIR_ROWS * CHUNK, WINDOW + CHUNK), F32),
    )(bucket, rel_bias)


def _t5_bucket(rel):
    nb = NUM_BUCKETS // 2
    max_exact = nb // 2
    ret = jnp.where(rel > 0, nb, 0)
    n = jnp.abs(rel)
    nf = jnp.maximum(n, 1).astype(jnp.float32)
    large = max_exact + (jnp.log(nf / max_exact) / math.log(MAX_DISTANCE / max_exact)
                         * (nb - max_exact)).astype(jnp.int32)
    large = jnp.minimum(large, nb - 1)
    return ret + jnp.where(n < max_exact, n, large)


def _prepare(w):
    p = {}
    p['w_in_ab'] = w['w_in_ab'].astype(BF16)
    p['w_out_ab'] = w['w_out_ab'].astype(BF16)
    win = w['w_in_cd']
    n_odd = win.shape[0]
    q_cols = win[:, :, D_C:D_C + D_ATT].reshape(n_odd, D_MODEL, N_PAIRS, 2, GQA, HEAD_DIM)
    q_cols = q_cols.transpose(0, 1, 2, 4, 3, 5).reshape(n_odd, D_MODEL, D_ATT) * ATT_SCALE
    p['w_in_cd'] = jnp.concatenate([win[:, :, :D_C], q_cols, win[:, :, D_C + D_ATT:]], axis=2).astype(BF16)
    wout = w['w_out_cd']
    att_rows = wout[:, D_C:, :].reshape(n_odd, N_PAIRS, 2, GQA, HEAD_DIM, D_MODEL)
    att_rows = att_rows.transpose(0, 1, 3, 2, 4, 5).reshape(n_odd, D_ATT, D_MODEL)
    p['w_out_cd'] = jnp.concatenate([wout[:, :D_C, :], att_rows], axis=1).astype(BF16)
    groups = D_C // POOL_GROUP
    wp = w['w_pool'][:, :, :, None, :] * jnp.eye(groups, dtype=F32)[None, :, None, :, None]
    p['w_pool'] = wp.reshape(n_odd, D_C, D_C).astype(BF16)
    half = jnp.concatenate([jnp.ones((D_FF,), F32), jnp.full((D_FF,), 0.5, F32)])
    p['w_ffn_up'] = (w['w_ffn_up'] * half).astype(BF16)
    p['w_ffn_down'] = w['w_ffn_down'].astype(BF16)
    p['ffn_conv_w'] = w['ffn_conv_w']
    p['sinks'] = w['attn_sinks'].reshape(n_odd, N_PAIRS, PAIR_ROWS)
    caw = w['conv_a_w']
    p['conv_a_w'] = jnp.broadcast_to(caw[:, :, None, :], caw.shape[:2] + (SUBLANES, D_A))
    p['conv_b_w'] = w['conv_b_w']
    for name in ('conv_a_b', 'ln_a_g', 'ln_a_b', 'pool_scale', 'ffn_conv_b',
                 'ln_mix_g', 'ln_mix_b', 'ln_ffn_g', 'ln_ffn_b'):
        p[name] = w[name][:, None, :]
    return p


def _trunk(x, pos0, states, p, bias, *, nb, nb_ffn, tt, tt_ffn, cq):
    b = x.shape[0]
    if states is None:
        hist_a = hist_b = hist_c = cache_k = cache_v = hist_f = None
    else:
        hist_a, hist_b, hist_c, cache_k, cache_v, hist_f = states
        cache_k = cache_k.reshape(cache_k.shape[0], b, WINDOW, D_KV)
        cache_v = cache_v.reshape(cache_v.shape[0], b, WINDOW, D_KV)
    bias_t = bias.reshape(N_PAIRS, PAIR_ROWS, CHUNK, WINDOW + CHUNK)[:, :, :cq, :WINDOW + cq]
    bias_t = bias_t.reshape(N_PAIRS, PAIR_ROWS * cq, WINDOW + cq)
    pad = jnp.full((N_PAIRS, PAIR_ROWS * cq, KEY_PAD - WINDOW - cq - 1), NEG_INF, F32)
    st_a, st_b, st_c, st_k, st_v, st_f = [], [], [], [], [], []
    for layer in range(DEPTH):
        i = layer // 2
        if layer % 2 == 0:
            x, sa, sb = _even_call(x, hist_a, hist_b, p, idx=i, layer=layer, nb=nb, tt=tt)
            st_a.append(sa[:, HIST_A_PAD - (CONV_A - 1):])
            st_b.append(sb[:, HIST_B_PAD - (CONV_B - 1):])
        else:
            sink = jnp.repeat(p['sinks'][i], cq, axis=1)[..., None]
            bias_l = jnp.concatenate([bias_t, sink, pad], axis=-1)
            x, sc, sk, sv = _odd_call(x, hist_c, cache_k, cache_v, p, bias_l, idx=i, layer=layer,
                                      nb=nb, tt=tt, cq=cq, pos0=pos0)
            st_c.append(sc[:, HIST_C_PAD - POOL_HIST:])
            st_k.append(sk.reshape(b, WINDOW, N_KV_HEADS, HEAD_DIM))
            st_v.append(sv.reshape(b, WINDOW, N_KV_HEADS, HEAD_DIM))
        x, sf = _ffn_call(x, hist_f, p, layer=layer, nb=nb_ffn, tt=tt_ffn)
        st_f.append(sf[:, HIST_F_PAD - (CONV_F - 1):])
    return (x, jnp.stack(st_a), jnp.stack(st_b), jnp.stack(st_c), jnp.stack(st_k),
            jnp.stack(st_v), jnp.stack(st_f))


def _tile_rows(s, largest):
    for cand in (1024, 512, 256, 128, 64):
        if cand <= largest and s % cand == 0:
            return cand
    return s


def kernel(x_prompt, x_sample, state_conv_a, state_conv_b, state_pool_c, cache_k_d, cache_v_d, state_ffn_conv, w_in_ab, conv_a_w, conv_a_b, ln_a_g, ln_a_b, conv_b_w, w_out_ab, w_in_cd, w_pool, pool_scale, attn_sinks, rel_bias, w_out_cd, w_ffn_up, ffn_conv_w, ffn_conv_b, w_ffn_down, ln_mix_g, ln_mix_b, ln_ffn_g, ln_ffn_b):
    w = dict(w_in_ab=w_in_ab, conv_a_w=conv_a_w, conv_a_b=conv_a_b, ln_a_g=ln_a_g, ln_a_b=ln_a_b,
             conv_b_w=conv_b_w, w_out_ab=w_out_ab, w_in_cd=w_in_cd, w_pool=w_pool, pool_scale=pool_scale,
             attn_sinks=attn_sinks, w_out_cd=w_out_cd, w_ffn_up=w_ffn_up,
             ffn_conv_w=ffn_conv_w, ffn_conv_b=ffn_conv_b, w_ffn_down=w_ffn_down,
             ln_mix_g=ln_mix_g, ln_mix_b=ln_mix_b, ln_ffn_g=ln_ffn_g, ln_ffn_b=ln_ffn_b)
    p = _prepare(w)
    rel = (jnp.arange(WINDOW + CHUNK)[None, :] - WINDOW - jnp.arange(CHUNK)[:, None]).astype(jnp.int32)
    bias = _bias_call(_t5_bucket(rel).astype(jnp.int32), rel_bias)

    b, s = x_prompt.shape[:2]
    nb = MIXER_BATCH_ROWS if b % MIXER_BATCH_ROWS == 0 else 1
    prompt = _trunk(
        x_prompt, 0, None, p, bias, nb=nb, nb_ffn=1, tt=_tile_rows(s, MIXER_TILE_ROWS),
        tt_ffn=_tile_rows(s, FFN_TILE_ROWS), cq=CHUNK)
    bs, ss = x_sample.shape[:2]
    states = (state_conv_a, state_conv_b, state_pool_c, cache_k_d, cache_v_d, state_ffn_conv)
    sample = _trunk(
        x_sample, PAST_LEN, states, p, bias, nb=bs, nb_ffn=bs, tt=ss, tt_ffn=ss, cq=ss)
    return (prompt[0], sample[0]) + tuple(prompt[1:]) + tuple(sample[1:])
```

```python
import functools
import math

import jax
import jax.numpy as jnp
from jax import lax
from jax.experimental import pallas as pl
from jax.experimental.pallas import tpu as pltpu

F32 = jnp.float32
BF16 = jnp.bfloat16

D_MODEL = 1024
DEPTH = 4
PAST_LEN = 4096
CHUNK = 64
D_A = D_MODEL // 2
CONV_A = 31
D_B = D_MODEL // 2
CONV_B = 3
POOL_WINDOWS = (2, 4, 8, 16)
D_C = D_MODEL // 4
POOL_GROUP = D_C // 4
POOL_HIST = 15
HEAD_DIM = 64
N_Q_HEADS = (D_MODEL - D_C) // HEAD_DIM
N_KV_HEADS = 4
GQA = N_Q_HEADS // N_KV_HEADS
D_ATT = N_Q_HEADS * HEAD_DIM
D_KV = N_KV_HEADS * HEAD_DIM
WINDOW = 128
NUM_BUCKETS = 32
MAX_DISTANCE = 128
ATT_SCALE = HEAD_DIM ** -0.5
NEG_INF = -1e30
D_FF = 2816
CONV_F = 3
LN_EPS = 1e-5
ALPHA = (2 * DEPTH) ** 0.25

LANES = 128
SUBLANES = 8
HIST_A_PAD = 32
HIST_B_PAD = SUBLANES
HIST_C_PAD = 16
HIST_F_PAD = SUBLANES
FF_CHUNK = 256
N_FF_CHUNKS = D_FF // FF_CHUNK
FFN_LOOP_CHUNKS = 4
N_PAIRS = N_KV_HEADS // 2
PAIR_ROWS = 2 * GQA
CONV_ROW_BLOCK = 32
CONV_STREAMS = 3
ROW_BLOCK = 64
KEY_PAD = 256
GELU_C0 = math.sqrt(2.0 / math.pi)
GELU_C1 = GELU_C0 * 0.044715
VMEM_LIMIT = 56 * 1024 * 1024
MIXER_TILE_ROWS = 256
FFN_TILE_ROWS = 1024
MIXER_BATCH_ROWS = 4


def _layer_norm(v, g, b):
    mu = jnp.mean(v, axis=-1, keepdims=True)
    d = v - mu
    var = jnp.mean(d * d, axis=-1, keepdims=True)
    return d * lax.rsqrt(var + LN_EPS) * g + b


def _dot(a, b):
    return jnp.dot(a, b, preferred_element_type=F32)


def _init_history(st_ref, hist_ref):
    if hist_ref is None:
        st_ref[...] = jnp.zeros_like(st_ref)
    else:
        pad = st_ref.shape[1] - hist_ref.shape[2]
        if pad:
            st_ref[:, 0:pad, :] = jnp.zeros((st_ref.shape[0], pad, st_ref.shape[2]), F32)
        st_ref[:, pad:, :] = hist_ref[0]


def _exact_zero(v):
    bits = pltpu.bitcast(v, jnp.uint32)
    bits = lax.shift_right_logical(lax.shift_right_logical(bits, jnp.uint32(16)), jnp.uint32(16))
    return pltpu.bitcast(bits, F32)


def _residual_norm(x_ref, branch, g, b, y_ref, *, nb, tt):
    rb = min(ROW_BLOCK, tt)
    for n in range(nb):
        for r0 in range(0, tt, rb):
            y = ALPHA * x_ref[n, r0:r0 + rb, :] + branch[n * tt + r0:n * tt + r0 + rb, :]
            y_ref[n, r0:r0 + rb, :] = _layer_norm(y, g, b)


def _layer_spec(layer, *shape):
    return pl.BlockSpec((1,) + shape, lambda i, t: (layer,) + (0,) * len(shape),
                        pipeline_mode=pl.Buffered(1))


def _x_spec(nb, tt):
    return pl.BlockSpec((nb, tt, D_MODEL), lambda i, t: (i, t, 0))


def _state_spec(nb, rows, cols):
    return pl.BlockSpec((nb, rows, cols), lambda i, t: (i, 0, 0))


def _compiler_params():
    return pltpu.CompilerParams(dimension_semantics=("parallel", "arbitrary"),
                                vmem_limit_bytes=VMEM_LIMIT)


def _ffn_kernel(*refs, nb, tt, has_history):
    x_ref = refs[0]
    hist_ref = refs[1] if has_history else None
    (wup_ref, cw_ref, cb_ref, wd_ref, lg_ref, lb_ref, y_ref, st_ref,
     xb_scr, g_scr0, g_scr1, v_scr0, v_scr1, acc_scr) = refs[-14:]
    m = nb * tt

    @pl.when(pl.program_id(1) == 0)
    def _():
        _init_history(st_ref, hist_ref)

    xb_scr[...] = x_ref[...].reshape(m, D_MODEL).astype(BF16)

    n_slabs = tt // SUBLANES
    sub = lax.broadcasted_iota(jnp.int32, (1, 1, SUBLANES, FF_CHUNK), 2)

    def chunk(j, base=0):
        off = base + j * FF_CHUNK
        return pl.ds(off if isinstance(off, int) else pl.multiple_of(off, FF_CHUNK), FF_CHUNK)

    def up(j, g_scr, v_scr):
        xb = xb_scr[...]
        g_scr[...] = _dot(xb, wup_ref[0, :, chunk(j)]).reshape(nb, tt, FF_CHUNK)
        v_scr[...] = _dot(xb, wup_ref[0, :, chunk(j, D_FF)]).reshape(nb, tt, FF_CHUNK)

    def down(j, g_scr, v_scr, first=False):
        g = g_scr[...].reshape(nb, n_slabs, SUBLANES, FF_CHUNK)
        hist = st_ref[:, :, chunk(j)].reshape(nb, 1, SUBLANES, FF_CHUNK)
        st_ref[:, :, chunk(j)] = g_scr[:, tt - HIST_F_PAD:tt, :]
        cw = cw_ref[0, :, chunk(j)]
        conv = cw[CONV_F - 1:CONV_F, :] * g + cb_ref[0, :, chunk(j)]
        for d in range(1, CONV_F):
            rot = pltpu.roll(g, d, 2)
            before = jnp.concatenate([pltpu.roll(hist, d, 2), rot[:, :n_slabs - 1]], axis=1)
            conv = conv + cw[CONV_F - 1 - d:CONV_F - d, :] * jnp.where(sub >= d, rot, before)
        conv = conv.reshape(nb, tt, FF_CHUNK)
        t = jnp.tanh(conv * (GELU_C0 + GELU_C1 * (conv * conv)))
        h = (conv * v_scr[...]) * (1.0 + t)
        hb = h.reshape(m, FF_CHUNK).astype(BF16)
        wd = wd_ref[0, chunk(j), :]
        if first:
            acc_scr[...] = _dot(hb, wd)
        else:
            acc_scr[...] += _dot(hb, wd)

    up(0, g_scr0, v_scr0)
    up(1, g_scr1, v_scr1)
    down(0, g_scr0, v_scr0, first=True)

    def body(i, carry):
        for d in range(1, FFN_LOOP_CHUNKS, 2):
            j = FFN_LOOP_CHUNKS * i + d
            up(j + 1, g_scr0, v_scr0)
            down(j, g_scr1, v_scr1)
            up(j + 2, g_scr1, v_scr1)
            down(j + 1, g_scr0, v_scr0)
        return carry

    lax.fori_loop(0, (N_FF_CHUNKS - 3) // FFN_LOOP_CHUNKS, body, 0)
    up(N_FF_CHUNKS - 1, g_scr0, v_scr0)
    down(N_FF_CHUNKS - 2, g_scr1, v_scr1)
    down(N_FF_CHUNKS - 1, g_scr0, v_scr0)
    _residual_norm(x_ref, acc_scr, lg_ref[0], lb_ref[0], y_ref, nb=nb, tt=tt)


def _ffn_call(x, hist, p, *, layer, nb, tt):
    b, s, _ = x.shape
    hist_specs, hist_args = [], []
    if hist is not None:
        hist_specs = [pl.BlockSpec((1, nb, CONV_F - 1, D_FF), lambda i, t: (layer, i, 0, 0))]
        hist_args = [hist]
    return pl.pallas_call(
        functools.partial(_ffn_kernel, nb=nb, tt=tt, has_history=hist is not None),
        grid=(b // nb, s // tt),
        in_specs=[_x_spec(nb, tt)] + hist_specs + [
            _layer_spec(layer, D_MODEL, 2 * D_FF),
            _layer_spec(layer, CONV_F, D_FF),
            _layer_spec(layer, 1, D_FF),
            _layer_spec(layer, D_FF, D_MODEL),
            _layer_spec(layer, 1, D_MODEL),
            _layer_spec(layer, 1, D_MODEL),
        ],
        out_specs=[_x_spec(nb, tt), _state_spec(nb, HIST_F_PAD, D_FF)],
        out_shape=[
            jax.ShapeDtypeStruct((b, s, D_MODEL), F32),
            jax.ShapeDtypeStruct((b, HIST_F_PAD, D_FF), F32),
        ],
        scratch_shapes=[
            pltpu.VMEM((nb * tt, D_MODEL), BF16),
            pltpu.VMEM((nb, tt, FF_CHUNK), F32),
            pltpu.VMEM((nb, tt, FF_CHUNK), F32),
            pltpu.VMEM((nb, tt, FF_CHUNK), F32),
            pltpu.VMEM((nb, tt, FF_CHUNK), F32),
            pltpu.VMEM((nb * tt, D_MODEL), F32),
        ],
        compiler_params=_compiler_params(),
    )(x, *hist_args, p['w_ffn_up'], p['ffn_conv_w'], p['ffn_conv_b'], p['w_ffn_down'],
      p['ln_ffn_g'], p['ln_ffn_b'])


def _even_kernel(*refs, nb, tt, has_history):
    x_ref = refs[0]
    ha_ref, hb_ref = refs[1:3] if has_history else (None, None)
    (win_ref, caw_ref, cab_ref, lag_ref, lab_ref, cbw_ref, wout_ref, lg_ref, lb_ref,
     y_ref, sta_ref, stb_ref, u_scr, sh_scr, v_scr, g_scr, a_scr, b_scr) = refs[-18:]

    @pl.when(pl.program_id(1) == 0)
    def _():
        _init_history(sta_ref, ha_ref)
        _init_history(stb_ref, hb_ref)

    m = nb * tt
    xb = x_ref[...].reshape(m, D_MODEL).astype(BF16)

    def proj(i):
        return _dot(xb, win_ref[0, :, i * D_A:(i + 1) * D_A])

    blk = min(ROW_BLOCK, tt)
    val, gate = proj(0), proj(1)
    u_scr[:, 0:HIST_A_PAD, :] = sta_ref[...]
    for n in range(nb):
        for r0 in range(0, tt, blk):
            rows = slice(n * tt + r0, n * tt + r0 + blk)
            u_scr[n, HIST_A_PAD + r0:HIST_A_PAD + r0 + blk, :] = val[rows] * jax.nn.sigmoid(gate[rows])
    sta_ref[...] = u_scr[:, tt:tt + HIST_A_PAD, :]
    c_gate, b_val = proj(3), proj(4)
    v_scr[:, 0:HIST_B_PAD, :] = stb_ref[...]
    for n in range(nb):
        for r0 in range(0, tt, blk):
            rows = slice(n * tt + r0, n * tt + r0 + blk)
            v_scr[n, HIST_B_PAD + r0:HIST_B_PAD + r0 + blk, :] = c_gate[rows] * b_val[rows]
    stb_ref[...] = v_scr[:, tt:tt + HIST_B_PAD, :]
    g_scr[...] = proj(2)

    n_slabs = (HIST_A_PAD + tt) // SUBLANES
    sub = lax.broadcasted_iota(jnp.int32, (SUBLANES, D_A), 0)
    for n in range(nb):
        for r in range(1, SUBLANES):
            rot = pltpu.roll(u_scr[n, 0:SUBLANES, :], SUBLANES - r, 0)
            for i in range(n_slabs - 1):
                nxt = pltpu.roll(u_scr[n, SUBLANES * (i + 1):SUBLANES * (i + 2), :], SUBLANES - r, 0)
                sh_scr[r - 1, n, SUBLANES * i:SUBLANES * (i + 1), :] = jnp.where(
                    sub < SUBLANES - r, rot, nxt)
                rot = nxt

    rb = min(CONV_ROW_BLOCK, tt)
    first = HIST_A_PAD - (CONV_A - 1)
    cab = cab_ref[0]
    lag = lag_ref[0]
    lab = lab_ref[0]
    recent = []
    for n in range(nb):
        for r0 in range(0, tt, rb):
            start = cab
            if len(recent) >= CONV_STREAMS:
                start = cab + _exact_zero(recent[-CONV_STREAMS])
            acc = jnp.broadcast_to(start[None], (rb // SUBLANES, SUBLANES, D_A))
            for k in range(CONV_A):
                tiles, r = divmod(first + k, SUBLANES)
                lo = r0 + SUBLANES * tiles
                win = u_scr[n, lo:lo + rb, :] if r == 0 else sh_scr[r - 1, n, lo:lo + rb, :]
                acc = acc + caw_ref[0, k][None] * win.reshape(rb // SUBLANES, SUBLANES, D_A)
            a = jax.nn.silu(_layer_norm(acc.reshape(rb, D_A), lag, lab))
            recent.append(a[0:SUBLANES, :])
            a_scr[n * tt + r0:n * tt + r0 + rb, :] = a.astype(BF16)

    cbw = cbw_ref[0]
    conv_b = (cbw[0:1, :][None] * v_scr[:, HIST_B_PAD - 2:HIST_B_PAD - 2 + tt, :]
              + cbw[1:2, :][None] * v_scr[:, HIST_B_PAD - 1:HIST_B_PAD - 1 + tt, :]
              + cbw[2:3, :][None] * v_scr[:, HIST_B_PAD:HIST_B_PAD + tt, :])
    b_scr[...] = (g_scr[...] * conv_b.reshape(m, D_B)).astype(BF16)

    mix = _dot(a_scr[...], wout_ref[0, 0:D_A, :]) + _dot(b_scr[...], wout_ref[0, D_A:D_A + D_B, :])
    _residual_norm(x_ref, mix, lg_ref[0], lb_ref[0], y_ref, nb=nb, tt=tt)


def _even_call(x, ha, hb, p, *, idx, layer, nb, tt):
    b, s, _ = x.shape
    hist_specs, hist_args = [], []
    if ha is not None:
        hist_specs = [pl.BlockSpec((1, nb, CONV_A - 1, D_A), lambda i, t: (idx, i, 0, 0)),
                      pl.BlockSpec((1, nb, CONV_B - 1, D_B), lambda i, t: (idx, i, 0, 0))]
        hist_args = [ha, hb]
    return pl.pallas_call(
        functools.partial(_even_kernel, nb=nb, tt=tt, has_history=ha is not None),
        grid=(b // nb, s // tt),
        in_specs=[_x_spec(nb, tt)] + hist_specs + [
            _layer_spec(idx, D_MODEL, 2 * D_A + 3 * D_B),
            _layer_spec(idx, CONV_A, SUBLANES, D_A),
            _layer_spec(idx, 1, D_A),
            _layer_spec(idx, 1, D_A),
            _layer_spec(idx, 1, D_A),
            _layer_spec(idx, CONV_B, D_B),
            _layer_spec(idx, D_A + D_B, D_MODEL),
            _layer_spec(layer, 1, D_MODEL),
            _layer_spec(layer, 1, D_MODEL),
        ],
        out_specs=[_x_spec(nb, tt), _state_spec(nb, HIST_A_PAD, D_A),
                   _state_spec(nb, HIST_B_PAD, D_B)],
        out_shape=[
            jax.ShapeDtypeStruct((b, s, D_MODEL), F32),
            jax.ShapeDtypeStruct((b, HIST_A_PAD, D_A), F32),
            jax.ShapeDtypeStruct((b, HIST_B_PAD, D_B), F32),
        ],
        scratch_shapes=[
            pltpu.VMEM((nb, HIST_A_PAD + tt, D_A), F32),
            pltpu.VMEM((SUBLANES - 1, nb, HIST_A_PAD + tt, D_A), F32),
            pltpu.VMEM((nb, HIST_B_PAD + tt, D_B), F32),
            pltpu.VMEM((nb * tt, D_B), F32),
            pltpu.VMEM((nb * tt, D_A), BF16),
            pltpu.VMEM((nb * tt, D_B), BF16),
        ],
        compiler_params=_compiler_params(),
    )(x, *hist_args, p['w_in_ab'], p['conv_a_w'], p['conv_a_b'], p['ln_a_g'], p['ln_a_b'],
      p['conv_b_w'], p['w_out_ab'], p['ln_mix_g'], p['ln_mix_b'])


def _odd_kernel(*refs, nb, tt, cq, pos0, masked_history):
    x_ref = refs[0]
    hc_ref, ck_ref, cv_ref = (None, None, None) if masked_history else refs[1:4]
    (win_ref, wpool_ref, pscale_ref, bias_ref, wout_ref, lg_ref, lb_ref,
     y_ref, stc_ref, nk_ref, nv_ref, c_scr, kb_scr, vb_scr, q_scr, pool_scr, s_scr, p_scr,
     att_scr) = refs[-19:]

    @pl.when(pl.program_id(1) == 0)
    def _():
        _init_history(stc_ref, hc_ref)
        _init_history(nk_ref, ck_ref)
        _init_history(nv_ref, cv_ref)

    m = nb * tt
    kw = WINDOW + cq
    t_idx = pl.program_id(1)
    xb = x_ref[...].reshape(m, D_MODEL).astype(BF16)
    q_off = D_C
    k_off = D_C + D_ATT
    v_off = k_off + D_KV

    c = _dot(xb, win_ref[0, :, 0:D_C]).reshape(nb, tt, D_C)
    c_scr[:, 0:HIST_C_PAD, :] = stc_ref[...]
    c_scr[:, HIST_C_PAD:HIST_C_PAD + tt, :] = c
    stc_ref[...] = c_scr[:, tt:tt + HIST_C_PAD, :]

    def window_sum(lane0, lo, hi):
        acc = None
        for j in range(lo, hi):
            r = c_scr[:, HIST_C_PAD - j:HIST_C_PAD - j + tt, lane0:lane0 + LANES]
            acc = r if acc is None else acc + r
        return acc

    row = lax.broadcasted_iota(jnp.int32, (nb, tt, LANES), 1)
    lane = lax.broadcasted_iota(jnp.int32, (nb, tt, LANES), 2)
    pos1 = row + (pos0 + 1) + t_idx * tt
    low = lane < POOL_GROUP
    pooled = []
    for tile, (w_small, w_big) in enumerate(zip(POOL_WINDOWS[0::2], POOL_WINDOWS[1::2])):
        s_small = window_sum(tile * LANES, 0, w_small)
        s_big = s_small + window_sum(tile * LANES, w_small, w_big)
        total = jnp.where(low, s_small, s_big)
        cnt = jnp.minimum(pos1, jnp.where(low, w_small, w_big)).astype(F32)
        pooled.append(total / cnt - c[:, :, tile * LANES:(tile + 1) * LANES])
    pooled = jnp.concatenate(pooled, axis=-1).reshape(m, D_C)
    pool_scr[...] = (_dot(pooled.astype(BF16), wpool_ref[0]) * pscale_ref[0]).astype(BF16)

    k_new = _dot(xb, win_ref[0, :, k_off:k_off + D_KV]).reshape(nb, tt, D_KV)
    v_new = _dot(xb, win_ref[0, :, v_off:v_off + D_KV]).reshape(nb, tt, D_KV)
    kb_scr[:, 0:WINDOW, :] = nk_ref[...].astype(BF16)
    vb_scr[:, 0:WINDOW, :] = nv_ref[...].astype(BF16)
    kb_scr[:, WINDOW:WINDOW + tt, :] = k_new.astype(BF16)
    vb_scr[:, WINDOW:WINDOW + tt, :] = v_new.astype(BF16)
    if tt >= WINDOW:
        nk_ref[...] = k_new[:, tt - WINDOW:, :]
        nv_ref[...] = v_new[:, tt - WINDOW:, :]
    else:
        nk_ref[...] = jnp.concatenate([nk_ref[:, tt:, :], k_new], axis=1)
        nv_ref[...] = jnp.concatenate([nv_ref[:, tt:, :], v_new], axis=1)
    q_scr[...] = _dot(xb, win_ref[0, :, q_off:q_off + D_ATT]).reshape(nb, tt, D_ATT)

    q_lane = lax.broadcasted_iota(jnp.int32, (cq, LANES), 1)
    key_row = lax.broadcasted_iota(jnp.int32, (1, KEY_PAD), 1)
    zero_keys = jnp.zeros((KEY_PAD - kw, LANES), BF16)
    n_chunks = tt // cq
    blocks = [(n, ci, p) for n in range(nb) for ci in range(n_chunks) for p in range(N_PAIRS)]

    def scores(blk):
        n, ci, p = blocks[blk]
        r0 = ci * cq
        tiles = [q_scr[n, r0:r0 + cq, (p * GQA + g) * LANES:(p * GQA + g + 1) * LANES]
                 for g in range(GQA)]
        lhs = ([jnp.where(q_lane < HEAD_DIM, tl, 0.0) for tl in tiles]
               + [jnp.where(q_lane >= HEAD_DIM, tl, 0.0) for tl in tiles])
        lhs = jnp.concatenate(lhs, axis=0).astype(BF16)
        kt = jnp.concatenate([kb_scr[n, r0:r0 + kw, p * LANES:(p + 1) * LANES], zero_keys], axis=0)
        s = lax.dot_general(lhs, kt, (((1,), (1,)), ((), ())), preferred_element_type=F32)
        s = s + bias_ref[p]
        if masked_history and r0 < WINDOW:
            first_valid = (WINDOW - r0) - t_idx * tt
            s = s + jnp.where(key_row >= first_valid, 0.0, NEG_INF)
        s_scr[blk] = s

    def softmax(blk):
        s = s_scr[blk]
        e = jnp.exp(s - jnp.max(s, axis=-1, keepdims=True))
        inv = 1.0 / jnp.sum(e, axis=-1, keepdims=True)
        p_scr[blk] = (e * inv).astype(BF16)

    def values(blk):
        n, ci, p = blocks[blk]
        r0 = ci * cq
        vt = jnp.concatenate([vb_scr[n, r0:r0 + kw, p * LANES:(p + 1) * LANES], zero_keys], axis=0)
        pv = _dot(p_scr[blk], vt)
        for g in range(GQA):
            o = jnp.where(q_lane < HEAD_DIM, pv[g * cq:(g + 1) * cq, :],
                          pv[(GQA + g) * cq:(GQA + g + 1) * cq, :])
            col = (p * GQA + g) * LANES
            att_scr[n * tt + r0:n * tt + r0 + cq, col:col + LANES] = o.astype(BF16)

    for step in range(len(blocks) + 2):
        if step < len(blocks):
            scores(step)
        if 0 <= step - 1 < len(blocks):
            softmax(step - 1)
        if 0 <= step - 2 < len(blocks):
            values(step - 2)

    mix = (_dot(pool_scr[...], wout_ref[0, 0:D_C, :])
           + _dot(att_scr[...], wout_ref[0, D_C:D_C + D_ATT, :]))
    _residual_norm(x_ref, mix, lg_ref[0], lb_ref[0], y_ref, nb=nb, tt=tt)


def _odd_call(x, hc, ck, cv, p, bias_l, *, idx, layer, nb, tt, cq, pos0):
    b, s, _ = x.shape
    rows = PAIR_ROWS * cq
    n_blocks = nb * (tt // cq) * N_PAIRS
    hist_specs, hist_args = [], []
    if hc is not None:
        hist_specs = [pl.BlockSpec((1, nb, POOL_HIST, D_C), lambda i, t: (idx, i, 0, 0)),
                      pl.BlockSpec((1, nb, WINDOW, D_KV), lambda i, t: (idx, i, 0, 0)),
                      pl.BlockSpec((1, nb, WINDOW, D_KV), lambda i, t: (idx, i, 0, 0))]
        hist_args = [hc, ck, cv]
    return pl.pallas_call(
        functools.partial(_odd_kernel, nb=nb, tt=tt, cq=cq, pos0=pos0,
                          masked_history=hc is None),
        grid=(b // nb, s // tt),
        in_specs=[_x_spec(nb, tt)] + hist_specs + [
            _layer_spec(idx, D_MODEL, D_C + D_ATT + 2 * D_KV),
            _layer_spec(idx, D_C, D_C),
            _layer_spec(idx, 1, D_C),
            pl.BlockSpec((N_PAIRS, rows, KEY_PAD), lambda i, t: (0, 0, 0)),
            _layer_spec(idx, D_C + D_ATT, D_MODEL),
            _layer_spec(layer, 1, D_MODEL),
            _layer_spec(layer, 1, D_MODEL),
        ],
        out_specs=[_x_spec(nb, tt), _state_spec(nb, HIST_C_PAD, D_C),
                   _state_spec(nb, WINDOW, D_KV), _state_spec(nb, WINDOW, D_KV)],
        out_shape=[
            jax.ShapeDtypeStruct((b, s, D_MODEL), F32),
            jax.ShapeDtypeStruct((b, HIST_C_PAD, D_C), F32),
            jax.ShapeDtypeStruct((b, WINDOW, D_KV), F32),
            jax.ShapeDtypeStruct((b, WINDOW, D_KV), F32),
        ],
        scratch_shapes=[
            pltpu.VMEM((nb, HIST_C_PAD + tt, D_C), F32),
            pltpu.VMEM((nb, WINDOW + tt, D_KV), BF16),
            pltpu.VMEM((nb, WINDOW + tt, D_KV), BF16),
            pltpu.VMEM((nb, tt, D_ATT), F32),
            pltpu.VMEM((nb * tt, D_C), BF16),
            pltpu.VMEM((n_blocks, rows, KEY_PAD), F32),
            pltpu.VMEM((n_blocks, rows, KEY_PAD), BF16),
            pltpu.VMEM((nb * tt, D_ATT), BF16),
        ],
        compiler_params=_compiler_params(),
    )(x, *hist_args, p['w_in_cd'], p['w_pool'], p['pool_scale'], bias_l, p['w_out_cd'],
      p['ln_mix_g'], p['ln_mix_b'])


def _bias_kernel(bucket_ref, rb_ref, out_ref):
    bucket = bucket_ref[...]
    for p in range(N_PAIRS):
        for r in range(PAIR_ROWS):
            head = (2 * p + r // GQA) * GQA + r % GQA
            acc = jnp.zeros((CHUNK, WINDOW + CHUNK), F32)
            for b in range(NUM_BUCKETS):
                acc = jnp.where(bucket == b, rb_ref[b, head], acc)
            out_ref[p, r * CHUNK:(r + 1) * CHUNK, :] = acc


def _bias_call(bucket, rel_bias):
    return pl.pallas_call(
        _bias_kernel,
        in_specs=[
            pl.BlockSpec(memory_space=pltpu.VMEM),
            pl.BlockSpec(memory_space=pltpu.SMEM),
        ],
        out_specs=pl.BlockSpec(memory_space=pltpu.VMEM),
        out_shape=jax.ShapeDtypeStruct((N_PAIRS, PAIR_ROWS * CHUNK, WINDOW + CHUNK), F32),
    )(bucket, rel_bias)


def _t5_bucket(rel):
    nb = NUM_BUCKETS // 2
    max_exact = nb // 2
    ret = jnp.where(rel > 0, nb, 0)
    n = jnp.abs(rel)
    nf = jnp.maximum(n, 1).astype(jnp.float32)
    large = max_exact + (jnp.log(nf / max_exact) / math.log(MAX_DISTANCE / max_exact)
                         * (nb - max_exact)).astype(jnp.int32)
    large = jnp.minimum(large, nb - 1)
    return ret + jnp.where(n < max_exact, n, large)


def _prepare(w):
    p = {}
    p['w_in_ab'] = w['w_in_ab'].astype(BF16)
    p['w_out_ab'] = w['w_out_ab'].astype(BF16)
    win = w['w_in_cd']
    n_odd = win.shape[0]
    q_cols = win[:, :, D_C:D_C + D_ATT].reshape(n_odd, D_MODEL, N_PAIRS, 2, GQA, HEAD_DIM)
    q_cols = q_cols.transpose(0, 1, 2, 4, 3, 5).reshape(n_odd, D_MODEL, D_ATT) * ATT_SCALE
    p['w_in_cd'] = jnp.concatenate([win[:, :, :D_C], q_cols, win[:, :, D_C + D_ATT:]], axis=2).astype(BF16)
    wout = w['w_out_cd']
    att_rows = wout[:, D_C:, :].reshape(n_odd, N_PAIRS, 2, GQA, HEAD_DIM, D_MODEL)
    att_rows = att_rows.transpose(0, 1, 3, 2, 4, 5).reshape(n_odd, D_ATT, D_MODEL)
    p['w_out_cd'] = jnp.concatenate([wout[:, :D_C, :], att_rows], axis=1).astype(BF16)
    groups = D_C // POOL_GROUP
    wp = w['w_pool'][:, :, :, None, :] * jnp.eye(groups, dtype=F32)[None, :, None, :, None]
    p['w_pool'] = wp.reshape(n_odd, D_C, D_C).astype(BF16)
    half = jnp.concatenate([jnp.ones((D_FF,), F32), jnp.full((D_FF,), 0.5, F32)])
    p['w_ffn_up'] = (w['w_ffn_up'] * half).astype(BF16)
    p['w_ffn_down'] = w['w_ffn_down'].astype(BF16)
    p['ffn_conv_w'] = w['ffn_conv_w']
    p['sinks'] = w['attn_sinks'].reshape(n_odd, N_PAIRS, PAIR_ROWS)
    caw = w['conv_a_w']
    p['conv_a_w'] = jnp.broadcast_to(caw[:, :, None, :], caw.shape[:2] + (SUBLANES, D_A))
    p['conv_b_w'] = w['conv_b_w']
    for name in ('conv_a_b', 'ln_a_g', 'ln_a_b', 'pool_scale', 'ffn_conv_b',
                 'ln_mix_g', 'ln_mix_b', 'ln_ffn_g', 'ln_ffn_b'):
        p[name] = w[name][:, None, :]
    return p


def _trunk(x, pos0, states, p, bias, *, nb, nb_ffn, tt, tt_ffn, cq):
    b = x.shape[0]
    if states is None:
        hist_a = hist_b = hist_c = cache_k = cache_v = hist_f = None
    else:
        hist_a, hist_b, hist_c, cache_k, cache_v, hist_f = states
        cache_k = cache_k.reshape(cache_k.shape[0], b, WINDOW, D_KV)
        cache_v = cache_v.reshape(cache_v.shape[0], b, WINDOW, D_KV)
    bias_t = bias.reshape(N_PAIRS, PAIR_ROWS, CHUNK, WINDOW + CHUNK)[:, :, :cq, :WINDOW + cq]
    bias_t = bias_t.reshape(N_PAIRS, PAIR_ROWS * cq, WINDOW + cq)
    pad = jnp.full((N_PAIRS, PAIR_ROWS * cq, KEY_PAD - WINDOW - cq - 1), NEG_INF, F32)
    st_a, st_b, st_c, st_k, st_v, st_f = [], [], [], [], [], []
    for layer in range(DEPTH):
        i = layer // 2
        if layer % 2 == 0:
            x, sa, sb = _even_call(x, hist_a, hist_b, p, idx=i, layer=layer, nb=nb, tt=tt)
            st_a.append(sa[:, HIST_A_PAD - (CONV_A - 1):])
            st_b.append(sb[:, HIST_B_PAD - (CONV_B - 1):])
        else:
            sink = jnp.repeat(p['sinks'][i], cq, axis=1)[..., None]
            bias_l = jnp.concatenate([bias_t, sink, pad], axis=-1)
            x, sc, sk, sv = _odd_call(x, hist_c, cache_k, cache_v, p, bias_l, idx=i, layer=layer,
                                      nb=nb, tt=tt, cq=cq, pos0=pos0)
            st_c.append(sc[:, HIST_C_PAD - POOL_HIST:])
            st_k.append(sk.reshape(b, WINDOW, N_KV_HEADS, HEAD_DIM))
            st_v.append(sv.reshape(b, WINDOW, N_KV_HEADS, HEAD_DIM))
        x, sf = _ffn_call(x, hist_f, p, layer=layer, nb=nb_ffn, tt=tt_ffn)
        st_f.append(sf[:, HIST_F_PAD - (CONV_F - 1):])
    return (x, jnp.stack(st_a), jnp.stack(st_b), jnp.stack(st_c), jnp.stack(st_k),
            jnp.stack(st_v), jnp.stack(st_f))


def _tile_rows(s, largest):
    for cand in (1024, 512, 256, 128, 64):
        if cand <= largest and s % cand == 0:
            return cand
    return s


def kernel(x_prompt, x_sample, state_conv_a, state_conv_b, state_pool_c, cache_k_d, cache_v_d, state_ffn_conv, w_in_ab, conv_a_w, conv_a_b, ln_a_g, ln_a_b, conv_b_w, w_out_ab, w_in_cd, w_pool, pool_scale, attn_sinks, rel_bias, w_out_cd, w_ffn_up, ffn_conv_w, ffn_conv_b, w_ffn_down, ln_mix_g, ln_mix_b, ln_ffn_g, ln_ffn_b):
    w = dict(w_in_ab=w_in_ab, conv_a_w=conv_a_w, conv_a_b=conv_a_b, ln_a_g=ln_a_g, ln_a_b=ln_a_b,
             conv_b_w=conv_b_w, w_out_ab=w_out_ab, w_in_cd=w_in_cd, w_pool=w_pool, pool_scale=pool_scale,
             attn_sinks=attn_sinks, w_out_cd=w_out_cd, w_ffn_up=w_ffn_up,
             ffn_conv_w=ffn_conv_w, ffn_conv_b=ffn_conv_b, w_ffn_down=w_ffn_down,
             ln_mix_g=ln_mix_g, ln_mix_b=ln_mix_b, ln_ffn_g=ln_ffn_g, ln_ffn_b=ln_ffn_b)
    p = _prepare(w)
    rel = (jnp.arange(WINDOW + CHUNK)[None, :] - WINDOW - jnp.arange(CHUNK)[:, None]).astype(jnp.int32)
    bias = _bias_call(_t5_bucket(rel).astype(jnp.int32), rel_bias)

    b, s = x_prompt.shape[:2]
    nb = MIXER_BATCH_ROWS if b % MIXER_BATCH_ROWS == 0 else 1
    prompt = _trunk(
        x_prompt, 0, None, p, bias, nb=nb, nb_ffn=1, tt=_tile_rows(s, MIXER_TILE_ROWS),
        tt_ffn=_tile_rows(s, FFN_TILE_ROWS), cq=CHUNK)
    bs, ss = x_sample.shape[:2]
    states = (state_conv_a, state_conv_b, state_pool_c, cache_k_d, cache_v_d, state_ffn_conv)
    sample = _trunk(
        x_sample, PAST_LEN, states, p, bias, nb=bs, nb_ffn=bs, tt=ss, tt_ffn=ss, cq=ss)
    return (prompt[0], sample[0]) + tuple(prompt[1:]) + tuple(sample[1:])
```

```python
import functools
import math

import jax
import jax.numpy as jnp
from jax import lax
from jax.experimental import pallas as pl
from jax.experimental.pallas import tpu as pltpu

F32 = jnp.float32
BF16 = jnp.bfloat16

D_MODEL = 1024
DEPTH = 4
PAST_LEN = 4096
CHUNK = 64
D_A = D_MODEL // 2
CONV_A = 31
D_B = D_MODEL // 2
CONV_B = 3
POOL_WINDOWS = (2, 4, 8, 16)
D_C = D_MODEL // 4
POOL_GROUP = D_C // 4
POOL_HIST = 15
HEAD_DIM = 64
N_Q_HEADS = (D_MODEL - D_C) // HEAD_DIM
N_KV_HEADS = 4
GQA = N_Q_HEADS // N_KV_HEADS
D_ATT = N_Q_HEADS * HEAD_DIM
D_KV = N_KV_HEADS * HEAD_DIM
WINDOW = 128
NUM_BUCKETS = 32
MAX_DISTANCE = 128
ATT_SCALE = HEAD_DIM ** -0.5
NEG_INF = -1e30
D_FF = 2816
CONV_F = 3
LN_EPS = 1e-5
ALPHA = (2 * DEPTH) ** 0.25

LANES = 128
SUBLANES = 8
HIST_A_PAD = 32
HIST_B_PAD = SUBLANES
HIST_C_PAD = 16
HIST_F_PAD = SUBLANES
FF_CHUNK = 256
N_FF_CHUNKS = D_FF // FF_CHUNK
FFN_LOOP_CHUNKS = 4
N_PAIRS = N_KV_HEADS // 2
PAIR_ROWS = 2 * GQA
CONV_ROW_BLOCK = 32
CONV_STREAMS = 3
ROW_BLOCK = 64
KEY_PAD = 256
GELU_C0 = math.sqrt(2.0 / math.pi)
GELU_C1 = GELU_C0 * 0.044715
VMEM_LIMIT = 56 * 1024 * 1024
MIXER_TILE_ROWS = 256
FFN_TILE_ROWS = 1024
MIXER_BATCH_ROWS = 4


def _layer_norm(v, g, b):
    mu = jnp.mean(v, axis=-1, keepdims=True)
    d = v - mu
    var = jnp.mean(d * d, axis=-1, keepdims=True)
    return d * lax.rsqrt(var + LN_EPS) * g + b


def _dot(a, b):
    return jnp.dot(a, b, preferred_element_type=F32)


def _init_history(st_ref, hist_ref):
    if hist_ref is None:
        st_ref[...] = jnp.zeros_like(st_ref)
    else:
        pad = st_ref.shape[1] - hist_ref.shape[2]
        if pad:
            st_ref[:, 0:pad, :] = jnp.zeros((st_ref.shape[0], pad, st_ref.shape[2]), F32)
        st_ref[:, pad:, :] = hist_ref[0]


def _exact_zero(v):
    bits = pltpu.bitcast(v, jnp.uint32)
    bits = lax.shift_right_logical(lax.shift_right_logical(bits, jnp.uint32(16)), jnp.uint32(16))
    return pltpu.bitcast(bits, F32)


def _residual_norm(x_ref, branch, g, b, y_ref, *, nb, tt):
    rb = min(ROW_BLOCK, tt)
    for n in range(nb):
        for r0 in range(0, tt, rb):
            y = ALPHA * x_ref[n, r0:r0 + rb, :] + branch[n * tt + r0:n * tt + r0 + rb, :]
            y_ref[n, r0:r0 + rb, :] = _layer_norm(y, g, b)


def _layer_spec(layer, *shape):
    return pl.BlockSpec((1,) + shape, lambda i, t: (layer,) + (0,) * len(shape),
                        pipeline_mode=pl.Buffered(1))


def _x_spec(nb, tt):
    return pl.BlockSpec((nb, tt, D_MODEL), lambda i, t: (i, t, 0))


def _state_spec(nb, rows, cols):
    return pl.BlockSpec((nb, rows, cols), lambda i, t: (i, 0, 0))


def _compiler_params():
    return pltpu.CompilerParams(dimension_semantics=("parallel", "arbitrary"),
                                vmem_limit_bytes=VMEM_LIMIT)


def _ffn_kernel(*refs, nb, tt, has_history):
    x_ref = refs[0]
    hist_ref = refs[1] if has_history else None
    (wup_ref, cw_ref, cb_ref, wd_ref, lg_ref, lb_ref, y_ref, st_ref,
     xb_scr, g_scr0, g_scr1, v_scr0, v_scr1, hb_scr0, hb_scr1, acc_scr) = refs[-16:]
    m = nb * tt

    @pl.when(pl.program_id(1) == 0)
    def _():
        _init_history(st_ref, hist_ref)

    xb_scr[...] = x_ref[...].reshape(m, D_MODEL).astype(BF16)

    n_slabs = tt // SUBLANES
    sub = lax.broadcasted_iota(jnp.int32, (1, 1, SUBLANES, FF_CHUNK), 2)

    def chunk(j, base=0):
        off = base + j * FF_CHUNK
        return pl.ds(off if isinstance(off, int) else pl.multiple_of(off, FF_CHUNK), FF_CHUNK)

    def up(j, g_scr, v_scr):
        xb = xb_scr[...]
        g_scr[...] = _dot(xb, wup_ref[0, :, chunk(j)]).reshape(nb, tt, FF_CHUNK)
        v_scr[...] = _dot(xb, wup_ref[0, :, chunk(j, D_FF)]).reshape(nb, tt, FF_CHUNK)

    def gate(j, g_scr, v_scr, hb_scr, half):
        g = g_scr[...].reshape(nb, n_slabs, SUBLANES, FF_CHUNK)
        hist = st_ref[:, :, chunk(j)].reshape(nb, 1, SUBLANES, FF_CHUNK)
        st_ref[:, :, chunk(j)] = g_scr[:, tt - HIST_F_PAD:tt, :]
        cw = cw_ref[0, :, chunk(j)]
        conv = cw[CONV_F - 1:CONV_F, :] * g + cb_ref[0, :, chunk(j)]
        for d in range(1, CONV_F):
            rot = pltpu.roll(g, d, 2)
            before = jnp.concatenate([pltpu.roll(hist, d, 2), rot[:, :n_slabs - 1]], axis=1)
            conv = conv + cw[CONV_F - 1 - d:CONV_F - d, :] * jnp.where(sub >= d, rot, before)
        conv = conv.reshape(nb, tt, FF_CHUNK)
        t = jnp.tanh(conv * (GELU_C0 + GELU_C1 * (conv * conv)))
        h = (conv * v_scr[...]) * (1.0 + t)
        hb_scr[:, half * FF_CHUNK:(half + 1) * FF_CHUNK] = h.reshape(m, FF_CHUNK).astype(BF16)

    def down(j, hb_scr, n_chunks, first=False):
        off = j * FF_CHUNK
        rows = pl.ds(off if isinstance(off, int) else pl.multiple_of(off, FF_CHUNK),
                     n_chunks * FF_CHUNK)
        out = _dot(hb_scr[:, 0:n_chunks * FF_CHUNK], wd_ref[0, rows, :])
        if first:
            acc_scr[...] = out
        else:
            acc_scr[...] += out

    up(0, g_scr0, v_scr0)
    up(1, g_scr1, v_scr1)
    gate(0, g_scr0, v_scr0, hb_scr1, 0)
    down(0, hb_scr1, 1, first=True)

    def body(i, carry):
        for d, hb_scr in zip(range(1, FFN_LOOP_CHUNKS, 2), (hb_scr0, hb_scr1)):
            j = FFN_LOOP_CHUNKS * i + d
            up(j + 1, g_scr0, v_scr0)
            gate(j, g_scr1, v_scr1, hb_scr, 0)
            up(j + 2, g_scr1, v_scr1)
            gate(j + 1, g_scr0, v_scr0, hb_scr, 1)
            down(j, hb_scr, 2)
        return carry

    lax.fori_loop(0, (N_FF_CHUNKS - 3) // FFN_LOOP_CHUNKS, body, 0)
    up(N_FF_CHUNKS - 1, g_scr0, v_scr0)
    gate(N_FF_CHUNKS - 2, g_scr1, v_scr1, hb_scr0, 0)
    gate(N_FF_CHUNKS - 1, g_scr0, v_scr0, hb_scr0, 1)
    down(N_FF_CHUNKS - 2, hb_scr0, 2)
    _residual_norm(x_ref, acc_scr, lg_ref[0], lb_ref[0], y_ref, nb=nb, tt=tt)


def _ffn_call(x, hist, p, *, layer, nb, tt):
    b, s, _ = x.shape
    hist_specs, hist_args = [], []
    if hist is not None:
        hist_specs = [pl.BlockSpec((1, nb, CONV_F - 1, D_FF), lambda i, t: (layer, i, 0, 0))]
        hist_args = [hist]
    return pl.pallas_call(
        functools.partial(_ffn_kernel, nb=nb, tt=tt, has_history=hist is not None),
        grid=(b // nb, s // tt),
        in_specs=[_x_spec(nb, tt)] + hist_specs + [
            _layer_spec(layer, D_MODEL, 2 * D_FF),
            _layer_spec(layer, CONV_F, D_FF),
            _layer_spec(layer, 1, D_FF),
            _layer_spec(layer, D_FF, D_MODEL),
            _layer_spec(layer, 1, D_MODEL),
            _layer_spec(layer, 1, D_MODEL),
        ],
        out_specs=[_x_spec(nb, tt), _state_spec(nb, HIST_F_PAD, D_FF)],
        out_shape=[
            jax.ShapeDtypeStruct((b, s, D_MODEL), F32),
            jax.ShapeDtypeStruct((b, HIST_F_PAD, D_FF), F32),
        ],
        scratch_shapes=[
            pltpu.VMEM((nb * tt, D_MODEL), BF16),
            pltpu.VMEM((nb, tt, FF_CHUNK), F32),
            pltpu.VMEM((nb, tt, FF_CHUNK), F32),
            pltpu.VMEM((nb, tt, FF_CHUNK), F32),
            pltpu.VMEM((nb, tt, FF_CHUNK), F32),
            pltpu.VMEM((nb * tt, 2 * FF_CHUNK), BF16),
            pltpu.VMEM((nb * tt, 2 * FF_CHUNK), BF16),
            pltpu.VMEM((nb * tt, D_MODEL), F32),
        ],
        compiler_params=_compiler_params(),
    )(x, *hist_args, p['w_ffn_up'], p['ffn_conv_w'], p['ffn_conv_b'], p['w_ffn_down'],
      p['ln_ffn_g'], p['ln_ffn_b'])


def _even_kernel(*refs, nb, tt, has_history):
    x_ref = refs[0]
    ha_ref, hb_ref = refs[1:3] if has_history else (None, None)
    (win_ref, caw_ref, cab_ref, lag_ref, lab_ref, cbw_ref, wout_ref, lg_ref, lb_ref,
     y_ref, sta_ref, stb_ref, u_scr, sh_scr, v_scr, g_scr, a_scr, b_scr) = refs[-18:]

    @pl.when(pl.program_id(1) == 0)
    def _():
        _init_history(sta_ref, ha_ref)
        _init_history(stb_ref, hb_ref)

    m = nb * tt
    xb = x_ref[...].reshape(m, D_MODEL).astype(BF16)

    def proj(i):
        return _dot(xb, win_ref[0, :, i * D_A:(i + 1) * D_A])

    blk = min(ROW_BLOCK, tt)
    val, gate = proj(0), proj(1)
    u_scr[:, 0:HIST_A_PAD, :] = sta_ref[...]
    for n in range(nb):
        for r0 in range(0, tt, blk):
            rows = slice(n * tt + r0, n * tt + r0 + blk)
            u_scr[n, HIST_A_PAD + r0:HIST_A_PAD + r0 + blk, :] = val[rows] * jax.nn.sigmoid(gate[rows])
    sta_ref[...] = u_scr[:, tt:tt + HIST_A_PAD, :]
    c_gate, b_val = proj(3), proj(4)
    v_scr[:, 0:HIST_B_PAD, :] = stb_ref[...]
    for n in range(nb):
        for r0 in range(0, tt, blk):
            rows = slice(n * tt + r0, n * tt + r0 + blk)
            v_scr[n, HIST_B_PAD + r0:HIST_B_PAD + r0 + blk, :] = c_gate[rows] * b_val[rows]
    stb_ref[...] = v_scr[:, tt:tt + HIST_B_PAD, :]
    g_scr[...] = proj(2)

    n_slabs = (HIST_A_PAD + tt) // SUBLANES
    sub = lax.broadcasted_iota(jnp.int32, (SUBLANES, D_A), 0)
    for n in range(nb):
        for r in range(1, SUBLANES):
            rot = pltpu.roll(u_scr[n, 0:SUBLANES, :], SUBLANES - r, 0)
            for i in range(n_slabs - 1):
                nxt = pltpu.roll(u_scr[n, SUBLANES * (i + 1):SUBLANES * (i + 2), :], SUBLANES - r, 0)
                sh_scr[r - 1, n, SUBLANES * i:SUBLANES * (i + 1), :] = jnp.where(
                    sub < SUBLANES - r, rot, nxt)
                rot = nxt

    rb = min(CONV_ROW_BLOCK, tt)
    first = HIST_A_PAD - (CONV_A - 1)
    cab = cab_ref[0]
    lag = lag_ref[0]
    lab = lab_ref[0]
    recent = []
    for n in range(nb):
        for r0 in range(0, tt, rb):
            start = cab
            if len(recent) >= CONV_STREAMS:
                start = cab + _exact_zero(recent[-CONV_STREAMS])
            acc = jnp.broadcast_to(start[None], (rb // SUBLANES, SUBLANES, D_A))
            for k in range(CONV_A):
                tiles, r = divmod(first + k, SUBLANES)
                lo = r0 + SUBLANES * tiles
                win = u_scr[n, lo:lo + rb, :] if r == 0 else sh_scr[r - 1, n, lo:lo + rb, :]
                acc = acc + caw_ref[0, k][None] * win.reshape(rb // SUBLANES, SUBLANES, D_A)
            a = jax.nn.silu(_layer_norm(acc.reshape(rb, D_A), lag, lab))
            recent.append(a[0:SUBLANES, :])
            a_scr[n * tt + r0:n * tt + r0 + rb, :] = a.astype(BF16)

    cbw = cbw_ref[0]
    conv_b = (cbw[0:1, :][None] * v_scr[:, HIST_B_PAD - 2:HIST_B_PAD - 2 + tt, :]
              + cbw[1:2, :][None] * v_scr[:, HIST_B_PAD - 1:HIST_B_PAD - 1 + tt, :]
              + cbw[2:3, :][None] * v_scr[:, HIST_B_PAD:HIST_B_PAD + tt, :])
    b_scr[...] = (g_scr[...] * conv_b.reshape(m, D_B)).astype(BF16)

    mix = _dot(a_scr[...], wout_ref[0, 0:D_A, :]) + _dot(b_scr[...], wout_ref[0, D_A:D_A + D_B, :])
    _residual_norm(x_ref, mix, lg_ref[0], lb_ref[0], y_ref, nb=nb, tt=tt)


def _even_call(x, ha, hb, p, *, idx, layer, nb, tt):
    b, s, _ = x.shape
    hist_specs, hist_args = [], []
    if ha is not None:
        hist_specs = [pl.BlockSpec((1, nb, CONV_A - 1, D_A), lambda i, t: (idx, i, 0, 0)),
                      pl.BlockSpec((1, nb, CONV_B - 1, D_B), lambda i, t: (idx, i, 0, 0))]
        hist_args = [ha, hb]
    return pl.pallas_call(
        functools.partial(_even_kernel, nb=nb, tt=tt, has_history=ha is not None),
        grid=(b // nb, s // tt),
        in_specs=[_x_spec(nb, tt)] + hist_specs + [
            _layer_spec(idx, D_MODEL, 2 * D_A + 3 * D_B),
            _layer_spec(idx, CONV_A, SUBLANES, D_A),
            _layer_spec(idx, 1, D_A),
            _layer_spec(idx, 1, D_A),
            _layer_spec(idx, 1, D_A),
            _layer_spec(idx, CONV_B, D_B),
            _layer_spec(idx, D_A + D_B, D_MODEL),
            _layer_spec(layer, 1, D_MODEL),
            _layer_spec(layer, 1, D_MODEL),
        ],
        out_specs=[_x_spec(nb, tt), _state_spec(nb, HIST_A_PAD, D_A),
                   _state_spec(nb, HIST_B_PAD, D_B)],
        out_shape=[
            jax.ShapeDtypeStruct((b, s, D_MODEL), F32),
            jax.ShapeDtypeStruct((b, HIST_A_PAD, D_A), F32),
            jax.ShapeDtypeStruct((b, HIST_B_PAD, D_B), F32),
        ],
        scratch_shapes=[
            pltpu.VMEM((nb, HIST_A_PAD + tt, D_A), F32),
            pltpu.VMEM((SUBLANES - 1, nb, HIST_A_PAD + tt, D_A), F32),
            pltpu.VMEM((nb, HIST_B_PAD + tt, D_B), F32),
            pltpu.VMEM((nb * tt, D_B), F32),
            pltpu.VMEM((nb * tt, D_A), BF16),
            pltpu.VMEM((nb * tt, D_B), BF16),
        ],
        compiler_params=_compiler_params(),
    )(x, *hist_args, p['w_in_ab'], p['conv_a_w'], p['conv_a_b'], p['ln_a_g'], p['ln_a_b'],
      p['conv_b_w'], p['w_out_ab'], p['ln_mix_g'], p['ln_mix_b'])


def _odd_kernel(*refs, nb, tt, cq, pos0, masked_history):
    x_ref = refs[0]
    hc_ref, ck_ref, cv_ref = (None, None, None) if masked_history else refs[1:4]
    (win_ref, wpool_ref, pscale_ref, bias_ref, wout_ref, lg_ref, lb_ref,
     y_ref, stc_ref, nk_ref, nv_ref, c_scr, kb_scr, vb_scr, q_scr, pool_scr, s_scr, p_scr,
     att_scr) = refs[-19:]

    @pl.when(pl.program_id(1) == 0)
    def _():
        _init_history(stc_ref, hc_ref)
        _init_history(nk_ref, ck_ref)
        _init_history(nv_ref, cv_ref)

    m = nb * tt
    kw = WINDOW + cq
    t_idx = pl.program_id(1)
    xb = x_ref[...].reshape(m, D_MODEL).astype(BF16)
    q_off = D_C
    k_off = D_C + D_ATT
    v_off = k_off + D_KV

    c = _dot(xb, win_ref[0, :, 0:D_C]).reshape(nb, tt, D_C)
    c_scr[:, 0:HIST_C_PAD, :] = stc_ref[...]
    c_scr[:, HIST_C_PAD:HIST_C_PAD + tt, :] = c
    stc_ref[...] = c_scr[:, tt:tt + HIST_C_PAD, :]

    def window_sum(lane0, lo, hi):
        acc = None
        for j in range(lo, hi):
            r = c_scr[:, HIST_C_PAD - j:HIST_C_PAD - j + tt, lane0:lane0 + LANES]
            acc = r if acc is None else acc + r
        return acc

    row = lax.broadcasted_iota(jnp.int32, (nb, tt, LANES), 1)
    lane = lax.broadcasted_iota(jnp.int32, (nb, tt, LANES), 2)
    pos1 = row + (pos0 + 1) + t_idx * tt
    low = lane < POOL_GROUP
    pooled = []
    for tile, (w_small, w_big) in enumerate(zip(POOL_WINDOWS[0::2], POOL_WINDOWS[1::2])):
        s_small = window_sum(tile * LANES, 0, w_small)
        s_big = s_small + window_sum(tile * LANES, w_small, w_big)
        total = jnp.where(low, s_small, s_big)
        cnt = jnp.minimum(pos1, jnp.where(low, w_small, w_big)).astype(F32)
        pooled.append(total / cnt - c[:, :, tile * LANES:(tile + 1) * LANES])
    pooled = jnp.concatenate(pooled, axis=-1).reshape(m, D_C)
    pool_scr[...] = (_dot(pooled.astype(BF16), wpool_ref[0]) * pscale_ref[0]).astype(BF16)

    k_new = _dot(xb, win_ref[0, :, k_off:k_off + D_KV]).reshape(nb, tt, D_KV)
    v_new = _dot(xb, win_ref[0, :, v_off:v_off + D_KV]).reshape(nb, tt, D_KV)
    kb_scr[:, 0:WINDOW, :] = nk_ref[...].astype(BF16)
    vb_scr[:, 0:WINDOW, :] = nv_ref[...].astype(BF16)
    kb_scr[:, WINDOW:WINDOW + tt, :] = k_new.astype(BF16)
    vb_scr[:, WINDOW:WINDOW + tt, :] = v_new.astype(BF16)
    if tt >= WINDOW:
        nk_ref[...] = k_new[:, tt - WINDOW:, :]
        nv_ref[...] = v_new[:, tt - WINDOW:, :]
    else:
        nk_ref[...] = jnp.concatenate([nk_ref[:, tt:, :], k_new], axis=1)
        nv_ref[...] = jnp.concatenate([nv_ref[:, tt:, :], v_new], axis=1)
    q_scr[...] = _dot(xb, win_ref[0, :, q_off:q_off + D_ATT]).reshape(nb, tt, D_ATT)

    q_lane = lax.broadcasted_iota(jnp.int32, (cq, LANES), 1)
    key_row = lax.broadcasted_iota(jnp.int32, (1, KEY_PAD), 1)
    zero_keys = jnp.zeros((KEY_PAD - kw, LANES), BF16)
    n_chunks = tt // cq
    blocks = [(n, ci, p) for n in range(nb) for ci in range(n_chunks) for p in range(N_PAIRS)]

    def scores(blk):
        n, ci, p = blocks[blk]
        r0 = ci * cq
        tiles = [q_scr[n, r0:r0 + cq, (p * GQA + g) * LANES:(p * GQA + g + 1) * LANES]
                 for g in range(GQA)]
        lhs = ([jnp.where(q_lane < HEAD_DIM, tl, 0.0) for tl in tiles]
               + [jnp.where(q_lane >= HEAD_DIM, tl, 0.0) for tl in tiles])
        lhs = jnp.concatenate(lhs, axis=0).astype(BF16)
        kt = jnp.concatenate([kb_scr[n, r0:r0 + kw, p * LANES:(p + 1) * LANES], zero_keys], axis=0)
        s = lax.dot_general(lhs, kt, (((1,), (1,)), ((), ())), preferred_element_type=F32)
        s = s + bias_ref[p]
        if masked_history and r0 < WINDOW:
            first_valid = (WINDOW - r0) - t_idx * tt
            s = s + jnp.where(key_row >= first_valid, 0.0, NEG_INF)
        s_scr[blk] = s

    def softmax(blk):
        s = s_scr[blk]
        e = jnp.exp(s - jnp.max(s, axis=-1, keepdims=True))
        inv = 1.0 / jnp.sum(e, axis=-1, keepdims=True)
        p_scr[blk] = (e * inv).astype(BF16)

    def values(blk):
        n, ci, p = blocks[blk]
        r0 = ci * cq
        vt = jnp.concatenate([vb_scr[n, r0:r0 + kw, p * LANES:(p + 1) * LANES], zero_keys], axis=0)
        pv = _dot(p_scr[blk], vt)
        for g in range(GQA):
            o = jnp.where(q_lane < HEAD_DIM, pv[g * cq:(g + 1) * cq, :],
                          pv[(GQA + g) * cq:(GQA + g + 1) * cq, :])
            col = (p * GQA + g) * LANES
            att_scr[n * tt + r0:n * tt + r0 + cq, col:col + LANES] = o.astype(BF16)

    for step in range(len(blocks) + 2):
        if step < len(blocks):
            scores(step)
        if 0 <= step - 1 < len(blocks):
            softmax(step - 1)
        if 0 <= step - 2 < len(blocks):
            values(step - 2)

    mix = (_dot(pool_scr[...], wout_ref[0, 0:D_C, :])
           + _dot(att_scr[...], wout_ref[0, D_C:D_C + D_ATT, :]))
    _residual_norm(x_ref, mix, lg_ref[0], lb_ref[0], y_ref, nb=nb, tt=tt)


def _odd_call(x, hc, ck, cv, p, bias_l, *, idx, layer, nb, tt, cq, pos0):
    b, s, _ = x.shape
    rows = PAIR_ROWS * cq
    n_blocks = nb * (tt // cq) * N_PAIRS
    hist_specs, hist_args = [], []
    if hc is not None:
        hist_specs = [pl.BlockSpec((1, nb, POOL_HIST, D_C), lambda i, t: (idx, i, 0, 0)),
                      pl.BlockSpec((1, nb, WINDOW, D_KV), lambda i, t: (idx, i, 0, 0)),
                      pl.BlockSpec((1, nb, WINDOW, D_KV), lambda i, t: (idx, i, 0, 0))]
        hist_args = [hc, ck, cv]
    return pl.pallas_call(
        functools.partial(_odd_kernel, nb=nb, tt=tt, cq=cq, pos0=pos0,
                          masked_history=hc is None),
        grid=(b // nb, s // tt),
        in_specs=[_x_spec(nb, tt)] + hist_specs + [
            _layer_spec(idx, D_MODEL, D_C + D_ATT + 2 * D_KV),
            _layer_spec(idx, D_C, D_C),
            _layer_spec(idx, 1, D_C),
            pl.BlockSpec((N_PAIRS, rows, KEY_PAD), lambda i, t: (0, 0, 0)),
            _layer_spec(idx, D_C + D_ATT, D_MODEL),
            _layer_spec(layer, 1, D_MODEL),
            _layer_spec(layer, 1, D_MODEL),
        ],
        out_specs=[_x_spec(nb, tt), _state_spec(nb, HIST_C_PAD, D_C),
                   _state_spec(nb, WINDOW, D_KV), _state_spec(nb, WINDOW, D_KV)],
        out_shape=[
            jax.ShapeDtypeStruct((b, s, D_MODEL), F32),
            jax.ShapeDtypeStruct((b, HIST_C_PAD, D_C), F32),
            jax.ShapeDtypeStruct((b, WINDOW, D_KV), F32),
            jax.ShapeDtypeStruct((b, WINDOW, D_KV), F32),
        ],
        scratch_shapes=[
            pltpu.VMEM((nb, HIST_C_PAD + tt, D_C), F32),
            pltpu.VMEM((nb, WINDOW + tt, D_KV), BF16),
            pltpu.VMEM((nb, WINDOW + tt, D_KV), BF16),
            pltpu.VMEM((nb, tt, D_ATT), F32),
            pltpu.VMEM((nb * tt, D_C), BF16),
            pltpu.VMEM((n_blocks, rows, KEY_PAD), F32),
            pltpu.VMEM((n_blocks, rows, KEY_PAD), BF16),
            pltpu.VMEM((nb * tt, D_ATT), BF16),
        ],
        compiler_params=_compiler_params(),
    )(x, *hist_args, p['w_in_cd'], p['w_pool'], p['pool_scale'], bias_l, p['w_out_cd'],
      p['ln_mix_g'], p['ln_mix_b'])


def _bias_kernel(bucket_ref, rb_ref, out_ref):
    bucket = bucket_ref[...]
    for p in range(N_PAIRS):
        for r in range(PAIR_ROWS):
            head = (2 * p + r // GQA) * GQA + r % GQA
            acc = jnp.zeros((CHUNK, WINDOW + CHUNK), F32)
            for b in range(NUM_BUCKETS):
                acc = jnp.where(bucket == b, rb_ref[b, head], acc)
            out_ref[p, r * CHUNK:(r + 1) * CHUNK, :] = acc


def _bias_call(bucket, rel_bias):
    return pl.pallas_call(
        _bias_kernel,
        in_specs=[
            pl.BlockSpec(memory_space=pltpu.VMEM),
            pl.BlockSpec(memory_space=pltpu.SMEM),
        ],
        out_specs=pl.BlockSpec(memory_space=pltpu.VMEM),
        out_shape=jax.ShapeDtypeStruct((N_PAIRS, PAIR_ROWS * CHUNK, WINDOW + CHUNK), F32),
    )(bucket, rel_bias)


def _t5_bucket(rel):
    nb = NUM_BUCKETS // 2
    max_exact = nb // 2
    ret = jnp.where(rel > 0, nb, 0)
    n = jnp.abs(rel)
    nf = jnp.maximum(n, 1).astype(jnp.float32)
    large = max_exact + (jnp.log(nf / max_exact) / math.log(MAX_DISTANCE / max_exact)
                         * (nb - max_exact)).astype(jnp.int32)
    large = jnp.minimum(large, nb - 1)
    return ret + jnp.where(n < max_exact, n, large)


def _prepare(w):
    p = {}
    p['w_in_ab'] = w['w_in_ab'].astype(BF16)
    p['w_out_ab'] = w['w_out_ab'].astype(BF16)
    win = w['w_in_cd']
    n_odd = win.shape[0]
    q_cols = win[:, :, D_C:D_C + D_ATT].reshape(n_odd, D_MODEL, N_PAIRS, 2, GQA, HEAD_DIM)
    q_cols = q_cols.transpose(0, 1, 2, 4, 3, 5).reshape(n_odd, D_MODEL, D_ATT) * ATT_SCALE
    p['w_in_cd'] = jnp.concatenate([win[:, :, :D_C], q_cols, win[:, :, D_C + D_ATT:]], axis=2).astype(BF16)
    wout = w['w_out_cd']
    att_rows = wout[:, D_C:, :].reshape(n_odd, N_PAIRS, 2, GQA, HEAD_DIM, D_MODEL)
    att_rows = att_rows.transpose(0, 1, 3, 2, 4, 5).reshape(n_odd, D_ATT, D_MODEL)
    p['w_out_cd'] = jnp.concatenate([wout[:, :D_C, :], att_rows], axis=1).astype(BF16)
    groups = D_C // POOL_GROUP
    wp = w['w_pool'][:, :, :, None, :] * jnp.eye(groups, dtype=F32)[None, :, None, :, None]
    p['w_pool'] = wp.reshape(n_odd, D_C, D_C).astype(BF16)
    half = jnp.concatenate([jnp.ones((D_FF,), F32), jnp.full((D_FF,), 0.5, F32)])
    p['w_ffn_up'] = (w['w_ffn_up'] * half).astype(BF16)
    p['w_ffn_down'] = w['w_ffn_down'].astype(BF16)
    p['ffn_conv_w'] = w['ffn_conv_w']
    p['sinks'] = w['attn_sinks'].reshape(n_odd, N_PAIRS, PAIR_ROWS)
    caw = w['conv_a_w']
    p['conv_a_w'] = jnp.broadcast_to(caw[:, :, None, :], caw.shape[:2] + (SUBLANES, D_A))
    p['conv_b_w'] = w['conv_b_w']
    for name in ('conv_a_b', 'ln_a_g', 'ln_a_b', 'pool_scale', 'ffn_conv_b',
                 'ln_mix_g', 'ln_mix_b', 'ln_ffn_g', 'ln_ffn_b'):
        p[name] = w[name][:, None, :]
    return p


def _trunk(x, pos0, states, p, bias, *, nb, nb_ffn, tt, tt_ffn, cq):
    b = x.shape[0]
    if states is None:
        hist_a = hist_b = hist_c = cache_k = cache_v = hist_f = None
    else:
        hist_a, hist_b, hist_c, cache_k, cache_v, hist_f = states
        cache_k = cache_k.reshape(cache_k.shape[0], b, WINDOW, D_KV)
        cache_v = cache_v.reshape(cache_v.shape[0], b, WINDOW, D_KV)
    bias_t = bias.reshape(N_PAIRS, PAIR_ROWS, CHUNK, WINDOW + CHUNK)[:, :, :cq, :WINDOW + cq]
    bias_t = bias_t.reshape(N_PAIRS, PAIR_ROWS * cq, WINDOW + cq)
    pad = jnp.full((N_PAIRS, PAIR_ROWS * cq, KEY_PAD - WINDOW - cq - 1), NEG_INF, F32)
    st_a, st_b, st_c, st_k, st_v, st_f = [], [], [], [], [], []
    for layer in range(DEPTH):
        i = layer // 2
        if layer % 2 == 0:
            x, sa, sb = _even_call(x, hist_a, hist_b, p, idx=i, layer=layer, nb=nb, tt=tt)
            st_a.append(sa[:, HIST_A_PAD - (CONV_A - 1):])
            st_b.append(sb[:, HIST_B_PAD - (CONV_B - 1):])
        else:
            sink = jnp.repeat(p['sinks'][i], cq, axis=1)[..., None]
            bias_l = jnp.concatenate([bias_t, sink, pad], axis=-1)
            x, sc, sk, sv = _odd_call(x, hist_c, cache_k, cache_v, p, bias_l, idx=i, layer=layer,
                                      nb=nb, tt=tt, cq=cq, pos0=pos0)
            st_c.append(sc[:, HIST_C_PAD - POOL_HIST:])
            st_k.append(sk.reshape(b, WINDOW, N_KV_HEADS, HEAD_DIM))
            st_v.append(sv.reshape(b, WINDOW, N_KV_HEADS, HEAD_DIM))
        x, sf = _ffn_call(x, hist_f, p, layer=layer, nb=nb_ffn, tt=tt_ffn)
        st_f.append(sf[:, HIST_F_PAD - (CONV_F - 1):])
    return (x, jnp.stack(st_a), jnp.stack(st_b), jnp.stack(st_c), jnp.stack(st_k),
            jnp.stack(st_v), jnp.stack(st_f))


def _tile_rows(s, largest):
    for cand in (1024, 512, 256, 128, 64):
        if cand <= largest and s % cand == 0:
            return cand
    return s


def kernel(x_prompt, x_sample, state_conv_a, state_conv_b, state_pool_c, cache_k_d, cache_v_d, state_ffn_conv, w_in_ab, conv_a_w, conv_a_b, ln_a_g, ln_a_b, conv_b_w, w_out_ab, w_in_cd, w_pool, pool_scale, attn_sinks, rel_bias, w_out_cd, w_ffn_up, ffn_conv_w, ffn_conv_b, w_ffn_down, ln_mix_g, ln_mix_b, ln_ffn_g, ln_ffn_b):
    w = dict(w_in_ab=w_in_ab, conv_a_w=conv_a_w, conv_a_b=conv_a_b, ln_a_g=ln_a_g, ln_a_b=ln_a_b,
             conv_b_w=conv_b_w, w_out_ab=w_out_ab, w_in_cd=w_in_cd, w_pool=w_pool, pool_scale=pool_scale,
             attn_sinks=attn_sinks, w_out_cd=w_out_cd, w_ffn_up=w_ffn_up,
             ffn_conv_w=ffn_conv_w, ffn_conv_b=ffn_conv_b, w_ffn_down=w_ffn_down,
             ln_mix_g=ln_mix_g, ln_mix_b=ln_mix_b, ln_ffn_g=ln_ffn_g, ln_ffn_b=ln_ffn_b)
    p = _prepare(w)
    rel = (jnp.arange(WINDOW + CHUNK)[None, :] - WINDOW - jnp.arange(CHUNK)[:, None]).astype(jnp.int32)
    bias = _bias_call(_t5_bucket(rel).astype(jnp.int32), rel_bias)

    b, s = x_prompt.shape[:2]
    nb = MIXER_BATCH_ROWS if b % MIXER_BATCH_ROWS == 0 else 1
    prompt = _trunk(
        x_prompt, 0, None, p, bias, nb=nb, nb_ffn=1, tt=_tile_rows(s, MIXER_TILE_ROWS),
        tt_ffn=_tile_rows(s, FFN_TILE_ROWS), cq=CHUNK)
    bs, ss = x_sample.shape[:2]
    states = (state_conv_a, state_conv_b, state_pool_c, cache_k_d, cache_v_d, state_ffn_conv)
    sample = _trunk(
        x_sample, PAST_LEN, states, p, bias, nb=bs, nb_ffn=bs, tt=ss, tt_ffn=ss, cq=ss)
    return (prompt[0], sample[0]) + tuple(prompt[1:]) + tuple(sample[1:])
```

```python
import functools
import math

import jax
import jax.numpy as jnp
from jax import lax
from jax.experimental import pallas as pl
from jax.experimental.pallas import tpu as pltpu

F32 = jnp.float32
BF16 = jnp.bfloat16

D_MODEL = 1024
DEPTH = 4
PAST_LEN = 4096
CHUNK = 64
D_A = D_MODEL // 2
CONV_A = 31
D_B = D_MODEL // 2
CONV_B = 3
POOL_WINDOWS = (2, 4, 8, 16)
D_C = D_MODEL // 4
POOL_GROUP = D_C // 4
POOL_HIST = 15
HEAD_DIM = 64
N_Q_HEADS = (D_MODEL - D_C) // HEAD_DIM
N_KV_HEADS = 4
GQA = N_Q_HEADS // N_KV_HEADS
D_ATT = N_Q_HEADS * HEAD_DIM
D_KV = N_KV_HEADS * HEAD_DIM
WINDOW = 128
NUM_BUCKETS = 32
MAX_DISTANCE = 128
ATT_SCALE = HEAD_DIM ** -0.5
NEG_INF = -1e30
D_FF = 2816
CONV_F = 3
LN_EPS = 1e-5
ALPHA = (2 * DEPTH) ** 0.25

LANES = 128
SUBLANES = 8
HIST_A_PAD = 32
HIST_B_PAD = SUBLANES
HIST_C_PAD = 16
HIST_F_PAD = SUBLANES
FF_CHUNK = 256
N_FF_CHUNKS = D_FF // FF_CHUNK
FFN_LOOP_CHUNKS = 4
N_PAIRS = N_KV_HEADS // 2
PAIR_ROWS = 2 * GQA
CONV_ROW_BLOCK = 32
CONV_STREAMS = 3
ROW_BLOCK = 64
KEY_PAD = 256
GELU_C0 = math.sqrt(2.0 / math.pi)
GELU_C1 = GELU_C0 * 0.044715
VMEM_LIMIT = 56 * 1024 * 1024
MIXER_TILE_ROWS = 256
FFN_TILE_ROWS = 1024
MIXER_BATCH_ROWS = 4


def _layer_norm(v, g, b):
    mu = jnp.mean(v, axis=-1, keepdims=True)
    d = v - mu
    var = jnp.mean(d * d, axis=-1, keepdims=True)
    return d * lax.rsqrt(var + LN_EPS) * g + b


def _dot(a, b):
    return jnp.dot(a, b, preferred_element_type=F32)


def _init_history(st_ref, hist_ref):
    if hist_ref is None:
        st_ref[...] = jnp.zeros_like(st_ref)
    else:
        pad = st_ref.shape[1] - hist_ref.shape[2]
        if pad:
            st_ref[:, 0:pad, :] = jnp.zeros((st_ref.shape[0], pad, st_ref.shape[2]), F32)
        st_ref[:, pad:, :] = hist_ref[0]


def _exact_zero(v):
    bits = pltpu.bitcast(v, jnp.uint32)
    bits = lax.shift_right_logical(lax.shift_right_logical(bits, jnp.uint32(16)), jnp.uint32(16))
    return pltpu.bitcast(bits, F32)


def _residual_norm(x_ref, branch, g, b, y_ref, *, nb, tt):
    rb = min(ROW_BLOCK, tt)
    for n in range(nb):
        for r0 in range(0, tt, rb):
            y = ALPHA * x_ref[n, r0:r0 + rb, :] + branch[n * tt + r0:n * tt + r0 + rb, :]
            y_ref[n, r0:r0 + rb, :] = _layer_norm(y, g, b)


def _layer_spec(layer, *shape):
    return pl.BlockSpec((1,) + shape, lambda i, t: (layer,) + (0,) * len(shape),
                        pipeline_mode=pl.Buffered(1))


def _x_spec(nb, tt):
    return pl.BlockSpec((nb, tt, D_MODEL), lambda i, t: (i, t, 0))


def _state_spec(nb, rows, cols):
    return pl.BlockSpec((nb, rows, cols), lambda i, t: (i, 0, 0))


def _compiler_params():
    return pltpu.CompilerParams(dimension_semantics=("parallel", "arbitrary"),
                                vmem_limit_bytes=VMEM_LIMIT)


def _ffn_kernel(*refs, nb, tt, has_history):
    x_ref = refs[0]
    hist_ref = refs[1] if has_history else None
    (wup_ref, cw_ref, cb_ref, wd_ref, lg_ref, lb_ref, y_ref, st_ref,
     xb_scr, g_scr0, g_scr1, v_scr0, v_scr1, acc_scr) = refs[-14:]
    m = nb * tt

    @pl.when(pl.program_id(1) == 0)
    def _():
        _init_history(st_ref, hist_ref)

    xb_scr[...] = x_ref[...].reshape(m, D_MODEL).astype(BF16)

    n_slabs = tt // SUBLANES
    sub = lax.broadcasted_iota(jnp.int32, (1, 1, SUBLANES, FF_CHUNK), 2)

    def chunk(j, base=0):
        off = base + j * FF_CHUNK
        return pl.ds(off if isinstance(off, int) else pl.multiple_of(off, FF_CHUNK), FF_CHUNK)

    def up(j, g_scr, v_scr):
        xb = xb_scr[...]
        g_scr[...] = _dot(xb, wup_ref[0, :, chunk(j)]).reshape(nb, tt, FF_CHUNK)
        v_scr[...] = _dot(xb, wup_ref[0, :, chunk(j, D_FF)]).reshape(nb, tt, FF_CHUNK)

    def down(j, g_scr, v_scr, first=False):
        g = g_scr[...].reshape(nb, n_slabs, SUBLANES, FF_CHUNK)
        hist = st_ref[:, :, chunk(j)].reshape(nb, 1, SUBLANES, FF_CHUNK)
        st_ref[:, :, chunk(j)] = g_scr[:, tt - HIST_F_PAD:tt, :]
        cw = cw_ref[0, :, chunk(j)]
        conv = cw[CONV_F - 1:CONV_F, :] * g + cb_ref[0, :, chunk(j)]
        for d in range(1, CONV_F):
            rot = pltpu.roll(g, d, 2)
            before = jnp.concatenate([pltpu.roll(hist, d, 2), rot[:, :n_slabs - 1]], axis=1)
            conv = conv + cw[CONV_F - 1 - d:CONV_F - d, :] * jnp.where(sub >= d, rot, before)
        conv = conv.reshape(nb, tt, FF_CHUNK)
        t = jnp.tanh(conv * (GELU_C0 + GELU_C1 * (conv * conv)))
        h = (conv * v_scr[...]) * (1.0 + t)
        hb = h.reshape(m, FF_CHUNK).astype(BF16)
        wd = wd_ref[0, chunk(j), :]
        if first:
            acc_scr[...] = _dot(hb, wd)
        else:
            acc_scr[...] += _dot(hb, wd)

    up(0, g_scr0, v_scr0)
    up(1, g_scr1, v_scr1)
    down(0, g_scr0, v_scr0, first=True)

    def body(i, carry):
        for d in range(1, FFN_LOOP_CHUNKS, 2):
            j = FFN_LOOP_CHUNKS * i + d
            up(j + 1, g_scr0, v_scr0)
            down(j, g_scr1, v_scr1)
            up(j + 2, g_scr1, v_scr1)
            down(j + 1, g_scr0, v_scr0)
        return carry

    lax.fori_loop(0, (N_FF_CHUNKS - 3) // FFN_LOOP_CHUNKS, body, 0)
    up(N_FF_CHUNKS - 1, g_scr0, v_scr0)
    down(N_FF_CHUNKS - 2, g_scr1, v_scr1)
    down(N_FF_CHUNKS - 1, g_scr0, v_scr0)
    _residual_norm(x_ref, acc_scr, lg_ref[0], lb_ref[0], y_ref, nb=nb, tt=tt)


def _ffn_call(x, hist, p, *, layer, nb, tt):
    b, s, _ = x.shape
    hist_specs, hist_args = [], []
    if hist is not None:
        hist_specs = [pl.BlockSpec((1, nb, CONV_F - 1, D_FF), lambda i, t: (layer, i, 0, 0))]
        hist_args = [hist]
    return pl.pallas_call(
        functools.partial(_ffn_kernel, nb=nb, tt=tt, has_history=hist is not None),
        grid=(b // nb, s // tt),
        in_specs=[_x_spec(nb, tt)] + hist_specs + [
            _layer_spec(layer, D_MODEL, 2 * D_FF),
            _layer_spec(layer, CONV_F, D_FF),
            _layer_spec(layer, 1, D_FF),
            _layer_spec(layer, D_FF, D_MODEL),
            _layer_spec(layer, 1, D_MODEL),
            _layer_spec(layer, 1, D_MODEL),
        ],
        out_specs=[_x_spec(nb, tt), _state_spec(nb, HIST_F_PAD, D_FF)],
        out_shape=[
            jax.ShapeDtypeStruct((b, s, D_MODEL), F32),
            jax.ShapeDtypeStruct((b, HIST_F_PAD, D_FF), F32),
        ],
        scratch_shapes=[
            pltpu.VMEM((nb * tt, D_MODEL), BF16),
            pltpu.VMEM((nb, tt, FF_CHUNK), F32),
            pltpu.VMEM((nb, tt, FF_CHUNK), F32),
            pltpu.VMEM((nb, tt, FF_CHUNK), F32),
            pltpu.VMEM((nb, tt, FF_CHUNK), F32),
            pltpu.VMEM((nb * tt, D_MODEL), F32),
        ],
        compiler_params=_compiler_params(),
    )(x, *hist_args, p['w_ffn_up'], p['ffn_conv_w'], p['ffn_conv_b'], p['w_ffn_down'],
      p['ln_ffn_g'], p['ln_ffn_b'])


def _even_kernel(*refs, nb, tt, has_history):
    x_ref = refs[0]
    ha_ref, hb_ref = refs[1:3] if has_history else (None, None)
    (win_ref, caw_ref, cab_ref, lag_ref, lab_ref, cbw_ref, wout_ref, lg_ref, lb_ref,
     y_ref, sta_ref, stb_ref, u_scr, sh_scr, v_scr, g_scr, a_scr, b_scr) = refs[-18:]

    @pl.when(pl.program_id(1) == 0)
    def _():
        _init_history(sta_ref, ha_ref)
        _init_history(stb_ref, hb_ref)

    m = nb * tt
    xb = x_ref[...].reshape(m, D_MODEL).astype(BF16)

    def proj(i):
        return _dot(xb, win_ref[0, :, i * D_A:(i + 1) * D_A])

    blk = min(ROW_BLOCK, tt)
    val, gate = proj(0), proj(1)
    u_scr[:, 0:HIST_A_PAD, :] = sta_ref[...]
    for n in range(nb):
        for r0 in range(0, tt, blk):
            rows = slice(n * tt + r0, n * tt + r0 + blk)
            u_scr[n, HIST_A_PAD + r0:HIST_A_PAD + r0 + blk, :] = val[rows] * jax.nn.sigmoid(gate[rows])
    sta_ref[...] = u_scr[:, tt:tt + HIST_A_PAD, :]
    c_gate, b_val = proj(3), proj(4)
    v_scr[:, 0:HIST_B_PAD, :] = stb_ref[...]
    for n in range(nb):
        for r0 in range(0, tt, blk):
            rows = slice(n * tt + r0, n * tt + r0 + blk)
            v_scr[n, HIST_B_PAD + r0:HIST_B_PAD + r0 + blk, :] = c_gate[rows] * b_val[rows]
    stb_ref[...] = v_scr[:, tt:tt + HIST_B_PAD, :]
    g_scr[...] = proj(2)

    n_slabs = (HIST_A_PAD + tt) // SUBLANES
    sub = lax.broadcasted_iota(jnp.int32, (SUBLANES, D_A), 0)
    for n in range(nb):
        for r in range(1, SUBLANES):
            rot = pltpu.roll(u_scr[n, 0:SUBLANES, :], SUBLANES - r, 0)
            for i in range(n_slabs - 1):
                nxt = pltpu.roll(u_scr[n, SUBLANES * (i + 1):SUBLANES * (i + 2), :], SUBLANES - r, 0)
                sh_scr[r - 1, n, SUBLANES * i:SUBLANES * (i + 1), :] = jnp.where(
                    sub < SUBLANES - r, rot, nxt)
                rot = nxt

    rb = min(CONV_ROW_BLOCK, tt)
    first = HIST_A_PAD - (CONV_A - 1)
    cab = cab_ref[0]
    lag = lag_ref[0]
    lab = lab_ref[0]
    recent = []
    for n in range(nb):
        for r0 in range(0, tt, rb):
            start = cab
            if len(recent) >= CONV_STREAMS:
                start = cab + _exact_zero(recent[-CONV_STREAMS])
            acc = jnp.broadcast_to(start[None], (rb // SUBLANES, SUBLANES, D_A))
            for k in range(CONV_A):
                tiles, r = divmod(first + k, SUBLANES)
                lo = r0 + SUBLANES * tiles
                win = u_scr[n, lo:lo + rb, :] if r == 0 else sh_scr[r - 1, n, lo:lo + rb, :]
                acc = acc + caw_ref[0, k][None] * win.reshape(rb // SUBLANES, SUBLANES, D_A)
            a = jax.nn.silu(_layer_norm(acc.reshape(rb, D_A), lag, lab))
            recent.append(a[0:SUBLANES, :])
            a_scr[n * tt + r0:n * tt + r0 + rb, :] = a.astype(BF16)

    cbw = cbw_ref[0]
    conv_b = (cbw[0:1, :][None] * v_scr[:, HIST_B_PAD - 2:HIST_B_PAD - 2 + tt, :]
              + cbw[1:2, :][None] * v_scr[:, HIST_B_PAD - 1:HIST_B_PAD - 1 + tt, :]
              + cbw[2:3, :][None] * v_scr[:, HIST_B_PAD:HIST_B_PAD + tt, :])
    b_scr[...] = (g_scr[...] * conv_b.reshape(m, D_B)).astype(BF16)

    mix = _dot(a_scr[...], wout_ref[0, 0:D_A, :]) + _dot(b_scr[...], wout_ref[0, D_A:D_A + D_B, :])
    _residual_norm(x_ref, mix, lg_ref[0], lb_ref[0], y_ref, nb=nb, tt=tt)


def _even_call(x, ha, hb, p, *, idx, layer, nb, tt):
    b, s, _ = x.shape
    hist_specs, hist_args = [], []
    if ha is not None:
        hist_specs = [pl.BlockSpec((1, nb, CONV_A - 1, D_A), lambda i, t: (idx, i, 0, 0)),
                      pl.BlockSpec((1, nb, CONV_B - 1, D_B), lambda i, t: (idx, i, 0, 0))]
        hist_args = [ha, hb]
    return pl.pallas_call(
        functools.partial(_even_kernel, nb=nb, tt=tt, has_history=ha is not None),
        grid=(b // nb, s // tt),
        in_specs=[_x_spec(nb, tt)] + hist_specs + [
            _layer_spec(idx, D_MODEL, 2 * D_A + 3 * D_B),
            _layer_spec(idx, CONV_A, SUBLANES, D_A),
            _layer_spec(idx, 1, D_A),
            _layer_spec(idx, 1, D_A),
            _layer_spec(idx, 1, D_A),
            _layer_spec(idx, CONV_B, D_B),
            _layer_spec(idx, D_A + D_B, D_MODEL),
            _layer_spec(layer, 1, D_MODEL),
            _layer_spec(layer, 1, D_MODEL),
        ],
        out_specs=[_x_spec(nb, tt), _state_spec(nb, HIST_A_PAD, D_A),
                   _state_spec(nb, HIST_B_PAD, D_B)],
        out_shape=[
            jax.ShapeDtypeStruct((b, s, D_MODEL), F32),
            jax.ShapeDtypeStruct((b, HIST_A_PAD, D_A), F32),
            jax.ShapeDtypeStruct((b, HIST_B_PAD, D_B), F32),
        ],
        scratch_shapes=[
            pltpu.VMEM((nb, HIST_A_PAD + tt, D_A), F32),
            pltpu.VMEM((SUBLANES - 1, nb, HIST_A_PAD + tt, D_A), F32),
            pltpu.VMEM((nb, HIST_B_PAD + tt, D_B), F32),
            pltpu.VMEM((nb * tt, D_B), F32),
            pltpu.VMEM((nb * tt, D_A), BF16),
            pltpu.VMEM((nb * tt, D_B), BF16),
        ],
        compiler_params=_compiler_params(),
    )(x, *hist_args, p['w_in_ab'], p['conv_a_w'], p['conv_a_b'], p['ln_a_g'], p['ln_a_b'],
      p['conv_b_w'], p['w_out_ab'], p['ln_mix_g'], p['ln_mix_b'])


def _odd_kernel(*refs, nb, tt, cq, pos0, masked_history):
    x_ref = refs[0]
    hc_ref, ck_ref, cv_ref = (None, None, None) if masked_history else refs[1:4]
    (win_ref, wpool_ref, pscale_ref, bias_ref, wout_ref, lg_ref, lb_ref,
     y_ref, stc_ref, nk_ref, nv_ref, c_scr, kb_scr, vb_scr, q_scr, pool_scr, s_scr, p_scr,
     att_scr) = refs[-19:]

    @pl.when(pl.program_id(1) == 0)
    def _():
        _init_history(stc_ref, hc_ref)
        _init_history(nk_ref, ck_ref)
        _init_history(nv_ref, cv_ref)

    m = nb * tt
    kw = WINDOW + cq
    t_idx = pl.program_id(1)
    xb = x_ref[...].reshape(m, D_MODEL).astype(BF16)
    q_off = D_C
    k_off = D_C + D_ATT
    v_off = k_off + D_KV

    c = _dot(xb, win_ref[0, :, 0:D_C]).reshape(nb, tt, D_C)
    c_scr[:, 0:HIST_C_PAD, :] = stc_ref[...]
    c_scr[:, HIST_C_PAD:HIST_C_PAD + tt, :] = c
    stc_ref[...] = c_scr[:, tt:tt + HIST_C_PAD, :]

    def window_sum(lane0, lo, hi):
        acc = None
        for j in range(lo, hi):
            r = c_scr[:, HIST_C_PAD - j:HIST_C_PAD - j + tt, lane0:lane0 + LANES]
            acc = r if acc is None else acc + r
        return acc

    row = lax.broadcasted_iota(jnp.int32, (nb, tt, LANES), 1)
    lane = lax.broadcasted_iota(jnp.int32, (nb, tt, LANES), 2)
    pos1 = row + (pos0 + 1) + t_idx * tt
    low = lane < POOL_GROUP
    pooled = []
    for tile, (w_small, w_big) in enumerate(zip(POOL_WINDOWS[0::2], POOL_WINDOWS[1::2])):
        s_small = window_sum(tile * LANES, 0, w_small)
        s_big = s_small + window_sum(tile * LANES, w_small, w_big)
        total = jnp.where(low, s_small, s_big)
        cnt = jnp.minimum(pos1, jnp.where(low, w_small, w_big)).astype(F32)
        pooled.append(total / cnt - c[:, :, tile * LANES:(tile + 1) * LANES])
    pooled = jnp.concatenate(pooled, axis=-1).reshape(m, D_C)
    pool_scr[...] = (_dot(pooled.astype(BF16), wpool_ref[0]) * pscale_ref[0]).astype(BF16)

    k_new = _dot(xb, win_ref[0, :, k_off:k_off + D_KV]).reshape(nb, tt, D_KV)
    v_new = _dot(xb, win_ref[0, :, v_off:v_off + D_KV]).reshape(nb, tt, D_KV)
    kb_scr[:, 0:WINDOW, :] = nk_ref[...].astype(BF16)
    vb_scr[:, 0:WINDOW, :] = nv_ref[...].astype(BF16)
    kb_scr[:, WINDOW:WINDOW + tt, :] = k_new.astype(BF16)
    vb_scr[:, WINDOW:WINDOW + tt, :] = v_new.astype(BF16)
    if tt >= WINDOW:
        nk_ref[...] = k_new[:, tt - WINDOW:, :]
        nv_ref[...] = v_new[:, tt - WINDOW:, :]
    else:
        nk_ref[...] = jnp.concatenate([nk_ref[:, tt:, :], k_new], axis=1)
        nv_ref[...] = jnp.concatenate([nv_ref[:, tt:, :], v_new], axis=1)
    q_scr[...] = _dot(xb, win_ref[0, :, q_off:q_off + D_ATT]).reshape(nb, tt, D_ATT)

    q_lane = lax.broadcasted_iota(jnp.int32, (cq, LANES), 1)
    key_row = lax.broadcasted_iota(jnp.int32, (1, KEY_PAD), 1)
    zero_keys = jnp.zeros((KEY_PAD - kw, LANES), BF16)
    n_chunks = tt // cq
    blocks = [(n, ci, p) for n in range(nb) for ci in range(n_chunks) for p in range(N_PAIRS)]

    def scores(blk):
        n, ci, p = blocks[blk]
        r0 = ci * cq
        tiles = [q_scr[n, r0:r0 + cq, (p * GQA + g) * LANES:(p * GQA + g + 1) * LANES]
                 for g in range(GQA)]
        lhs = ([jnp.where(q_lane < HEAD_DIM, tl, 0.0) for tl in tiles]
               + [jnp.where(q_lane >= HEAD_DIM, tl, 0.0) for tl in tiles])
        lhs = jnp.concatenate(lhs, axis=0).astype(BF16)
        kt = jnp.concatenate([kb_scr[n, r0:r0 + kw, p * LANES:(p + 1) * LANES], zero_keys], axis=0)
        s = lax.dot_general(lhs, kt, (((1,), (1,)), ((), ())), preferred_element_type=F32)
        s = s + bias_ref[p]
        if masked_history and r0 < WINDOW:
            first_valid = (WINDOW - r0) - t_idx * tt
            s = s + jnp.where(key_row >= first_valid, 0.0, NEG_INF)
        s_scr[blk] = s

    inv_den = {}

    def softmax(blk):
        s = s_scr[blk]
        e = jnp.exp(s - jnp.max(s, axis=-1, keepdims=True))
        inv_den[blk] = 1.0 / jnp.sum(e, axis=-1, keepdims=True)
        p_scr[blk] = e.astype(BF16)

    def values(blk):
        n, ci, p = blocks[blk]
        r0 = ci * cq
        vt = jnp.concatenate([vb_scr[n, r0:r0 + kw, p * LANES:(p + 1) * LANES], zero_keys], axis=0)
        pv = _dot(p_scr[blk], vt) * inv_den.pop(blk)
        for g in range(GQA):
            o = jnp.where(q_lane < HEAD_DIM, pv[g * cq:(g + 1) * cq, :],
                          pv[(GQA + g) * cq:(GQA + g + 1) * cq, :])
            col = (p * GQA + g) * LANES
            att_scr[n * tt + r0:n * tt + r0 + cq, col:col + LANES] = o.astype(BF16)

    for step in range(len(blocks) + 2):
        if step < len(blocks):
            scores(step)
        if 0 <= step - 1 < len(blocks):
            softmax(step - 1)
        if 0 <= step - 2 < len(blocks):
            values(step - 2)

    mix = (_dot(pool_scr[...], wout_ref[0, 0:D_C, :])
           + _dot(att_scr[...], wout_ref[0, D_C:D_C + D_ATT, :]))
    _residual_norm(x_ref, mix, lg_ref[0], lb_ref[0], y_ref, nb=nb, tt=tt)


def _odd_call(x, hc, ck, cv, p, bias_l, *, idx, layer, nb, tt, cq, pos0):
    b, s, _ = x.shape
    rows = PAIR_ROWS * cq
    n_blocks = nb * (tt // cq) * N_PAIRS
    hist_specs, hist_args = [], []
    if hc is not None:
        hist_specs = [pl.BlockSpec((1, nb, POOL_HIST, D_C), lambda i, t: (idx, i, 0, 0)),
                      pl.BlockSpec((1, nb, WINDOW, D_KV), lambda i, t: (idx, i, 0, 0)),
                      pl.BlockSpec((1, nb, WINDOW, D_KV), lambda i, t: (idx, i, 0, 0))]
        hist_args = [hc, ck, cv]
    return pl.pallas_call(
        functools.partial(_odd_kernel, nb=nb, tt=tt, cq=cq, pos0=pos0,
                          masked_history=hc is None),
        grid=(b // nb, s // tt),
        in_specs=[_x_spec(nb, tt)] + hist_specs + [
            _layer_spec(idx, D_MODEL, D_C + D_ATT + 2 * D_KV),
            _layer_spec(idx, D_C, D_C),
            _layer_spec(idx, 1, D_C),
            pl.BlockSpec((N_PAIRS, rows, KEY_PAD), lambda i, t: (0, 0, 0)),
            _layer_spec(idx, D_C + D_ATT, D_MODEL),
            _layer_spec(layer, 1, D_MODEL),
            _layer_spec(layer, 1, D_MODEL),
        ],
        out_specs=[_x_spec(nb, tt), _state_spec(nb, HIST_C_PAD, D_C),
                   _state_spec(nb, WINDOW, D_KV), _state_spec(nb, WINDOW, D_KV)],
        out_shape=[
            jax.ShapeDtypeStruct((b, s, D_MODEL), F32),
            jax.ShapeDtypeStruct((b, HIST_C_PAD, D_C), F32),
            jax.ShapeDtypeStruct((b, WINDOW, D_KV), F32),
            jax.ShapeDtypeStruct((b, WINDOW, D_KV), F32),
        ],
        scratch_shapes=[
            pltpu.VMEM((nb, HIST_C_PAD + tt, D_C), F32),
            pltpu.VMEM((nb, WINDOW + tt, D_KV), BF16),
            pltpu.VMEM((nb, WINDOW + tt, D_KV), BF16),
            pltpu.VMEM((nb, tt, D_ATT), F32),
            pltpu.VMEM((nb * tt, D_C), BF16),
            pltpu.VMEM((n_blocks, rows, KEY_PAD), F32),
            pltpu.VMEM((n_blocks, rows, KEY_PAD), BF16),
            pltpu.VMEM((nb * tt, D_ATT), BF16),
        ],
        compiler_params=_compiler_params(),
    )(x, *hist_args, p['w_in_cd'], p['w_pool'], p['pool_scale'], bias_l, p['w_out_cd'],
      p['ln_mix_g'], p['ln_mix_b'])


def _bias_kernel(bucket_ref, rb_ref, out_ref):
    bucket = bucket_ref[...]
    for p in range(N_PAIRS):
        for r in range(PAIR_ROWS):
            head = (2 * p + r // GQA) * GQA + r % GQA
            acc = jnp.zeros((CHUNK, WINDOW + CHUNK), F32)
            for b in range(NUM_BUCKETS):
                acc = jnp.where(bucket == b, rb_ref[b, head], acc)
            out_ref[p, r * CHUNK:(r + 1) * CHUNK, :] = acc


def _bias_call(bucket, rel_bias):
    return pl.pallas_call(
        _bias_kernel,
        in_specs=[
            pl.BlockSpec(memory_space=pltpu.VMEM),
            pl.BlockSpec(memory_space=pltpu.SMEM),
        ],
        out_specs=pl.BlockSpec(memory_space=pltpu.VMEM),
        out_shape=jax.ShapeDtypeStruct((N_PAIRS, PAIR_ROWS * CHUNK, WINDOW + CHUNK), F32),
    )(bucket, rel_bias)


def _t5_bucket(rel):
    nb = NUM_BUCKETS // 2
    max_exact = nb // 2
    ret = jnp.where(rel > 0, nb, 0)
    n = jnp.abs(rel)
    nf = jnp.maximum(n, 1).astype(jnp.float32)
    large = max_exact + (jnp.log(nf / max_exact) / math.log(MAX_DISTANCE / max_exact)
                         * (nb - max_exact)).astype(jnp.int32)
    large = jnp.minimum(large, nb - 1)
    return ret + jnp.where(n < max_exact, n, large)


def _prepare(w):
    p = {}
    p['w_in_ab'] = w['w_in_ab'].astype(BF16)
    p['w_out_ab'] = w['w_out_ab'].astype(BF16)
    win = w['w_in_cd']
    n_odd = win.shape[0]
    q_cols = win[:, :, D_C:D_C + D_ATT].reshape(n_odd, D_MODEL, N_PAIRS, 2, GQA, HEAD_DIM)
    q_cols = q_cols.transpose(0, 1, 2, 4, 3, 5).reshape(n_odd, D_MODEL, D_ATT) * ATT_SCALE
    p['w_in_cd'] = jnp.concatenate([win[:, :, :D_C], q_cols, win[:, :, D_C + D_ATT:]], axis=2).astype(BF16)
    wout = w['w_out_cd']
    att_rows = wout[:, D_C:, :].reshape(n_odd, N_PAIRS, 2, GQA, HEAD_DIM, D_MODEL)
    att_rows = att_rows.transpose(0, 1, 3, 2, 4, 5).reshape(n_odd, D_ATT, D_MODEL)
    p['w_out_cd'] = jnp.concatenate([wout[:, :D_C, :], att_rows], axis=1).astype(BF16)
    groups = D_C // POOL_GROUP
    wp = w['w_pool'][:, :, :, None, :] * jnp.eye(groups, dtype=F32)[None, :, None, :, None]
    p['w_pool'] = wp.reshape(n_odd, D_C, D_C).astype(BF16)
    half = jnp.concatenate([jnp.ones((D_FF,), F32), jnp.full((D_FF,), 0.5, F32)])
    p['w_ffn_up'] = (w['w_ffn_up'] * half).astype(BF16)
    p['w_ffn_down'] = w['w_ffn_down'].astype(BF16)
    p['ffn_conv_w'] = w['ffn_conv_w']
    p['sinks'] = w['attn_sinks'].reshape(n_odd, N_PAIRS, PAIR_ROWS)
    caw = w['conv_a_w']
    p['conv_a_w'] = jnp.broadcast_to(caw[:, :, None, :], caw.shape[:2] + (SUBLANES, D_A))
    p['conv_b_w'] = w['conv_b_w']
    for name in ('conv_a_b', 'ln_a_g', 'ln_a_b', 'pool_scale', 'ffn_conv_b',
                 'ln_mix_g', 'ln_mix_b', 'ln_ffn_g', 'ln_ffn_b'):
        p[name] = w[name][:, None, :]
    return p


def _trunk(x, pos0, states, p, bias, *, nb, nb_ffn, tt, tt_ffn, cq):
    b = x.shape[0]
    if states is None:
        hist_a = hist_b = hist_c = cache_k = cache_v = hist_f = None
    else:
        hist_a, hist_b, hist_c, cache_k, cache_v, hist_f = states
        cache_k = cache_k.reshape(cache_k.shape[0], b, WINDOW, D_KV)
        cache_v = cache_v.reshape(cache_v.shape[0], b, WINDOW, D_KV)
    bias_t = bias.reshape(N_PAIRS, PAIR_ROWS, CHUNK, WINDOW + CHUNK)[:, :, :cq, :WINDOW + cq]
    bias_t = bias_t.reshape(N_PAIRS, PAIR_ROWS * cq, WINDOW + cq)
    pad = jnp.full((N_PAIRS, PAIR_ROWS * cq, KEY_PAD - WINDOW - cq - 1), NEG_INF, F32)
    st_a, st_b, st_c, st_k, st_v, st_f = [], [], [], [], [], []
    for layer in range(DEPTH):
        i = layer // 2
        if layer % 2 == 0:
            x, sa, sb = _even_call(x, hist_a, hist_b, p, idx=i, layer=layer, nb=nb, tt=tt)
            st_a.append(sa[:, HIST_A_PAD - (CONV_A - 1):])
            st_b.append(sb[:, HIST_B_PAD - (CONV_B - 1):])
        else:
            sink = jnp.repeat(p['sinks'][i], cq, axis=1)[..., None]
            bias_l = jnp.concatenate([bias_t, sink, pad], axis=-1)
            x, sc, sk, sv = _odd_call(x, hist_c, cache_k, cache_v, p, bias_l, idx=i, layer=layer,
                                      nb=nb, tt=tt, cq=cq, pos0=pos0)
            st_c.append(sc[:, HIST_C_PAD - POOL_HIST:])
            st_k.append(sk.reshape(b, WINDOW, N_KV_HEADS, HEAD_DIM))
            st_v.append(sv.reshape(b, WINDOW, N_KV_HEADS, HEAD_DIM))
        x, sf = _ffn_call(x, hist_f, p, layer=layer, nb=nb_ffn, tt=tt_ffn)
        st_f.append(sf[:, HIST_F_PAD - (CONV_F - 1):])
    return (x, jnp.stack(st_a), jnp.stack(st_b), jnp.stack(st_c), jnp.stack(st_k),
            jnp.stack(st_v), jnp.stack(st_f))


def _tile_rows(s, largest):
    for cand in (1024, 512, 256, 128, 64):
        if cand <= largest and s % cand == 0:
            return cand
    return s


def kernel(x_prompt, x_sample, state_conv_a, state_conv_b, state_pool_c, cache_k_d, cache_v_d, state_ffn_conv, w_in_ab, conv_a_w, conv_a_b, ln_a_g, ln_a_b, conv_b_w, w_out_ab, w_in_cd, w_pool, pool_scale, attn_sinks, rel_bias, w_out_cd, w_ffn_up, ffn_conv_w, ffn_conv_b, w_ffn_down, ln_mix_g, ln_mix_b, ln_ffn_g, ln_ffn_b):
    w = dict(w_in_ab=w_in_ab, conv_a_w=conv_a_w, conv_a_b=conv_a_b, ln_a_g=ln_a_g, ln_a_b=ln_a_b,
             conv_b_w=conv_b_w, w_out_ab=w_out_ab, w_in_cd=w_in_cd, w_pool=w_pool, pool_scale=pool_scale,
             attn_sinks=attn_sinks, w_out_cd=w_out_cd, w_ffn_up=w_ffn_up,
             ffn_conv_w=ffn_conv_w, ffn_conv_b=ffn_conv_b, w_ffn_down=w_ffn_down,
             ln_mix_g=ln_mix_g, ln_mix_b=ln_mix_b, ln_ffn_g=ln_ffn_g, ln_ffn_b=ln_ffn_b)
    p = _prepare(w)
    rel = (jnp.arange(WINDOW + CHUNK)[None, :] - WINDOW - jnp.arange(CHUNK)[:, None]).astype(jnp.int32)
    bias = _bias_call(_t5_bucket(rel).astype(jnp.int32), rel_bias)

    b, s = x_prompt.shape[:2]
    nb = MIXER_BATCH_ROWS if b % MIXER_BATCH_ROWS == 0 else 1
    prompt = _trunk(
        x_prompt, 0, None, p, bias, nb=nb, nb_ffn=1, tt=_tile_rows(s, MIXER_TILE_ROWS),
        tt_ffn=_tile_rows(s, FFN_TILE_ROWS), cq=CHUNK)
    bs, ss = x_sample.shape[:2]
    states = (state_conv_a, state_conv_b, state_pool_c, cache_k_d, cache_v_d, state_ffn_conv)
    sample = _trunk(
        x_sample, PAST_LEN, states, p, bias, nb=bs, nb_ffn=bs, tt=ss, tt_ffn=ss, cq=ss)
    return (prompt[0], sample[0]) + tuple(prompt[1:]) + tuple(sample[1:])
```
